```python
import jax, jax.numpy as jnp
from jax import lax
import numpy as np

D_MODEL = 2048
BATCH = 2
SEQ = 4096
DEPTH = 1
DEC_BATCH = 8
DEC_SEQ = 64
PAST_LEN = 1024

CHUNK = 64
D_PLE = 256
D_FF = 5504
D_CONV = D_MODEL // 2
CONV_WIDTH = 31
HEAD_DIM = 64
N_HEADS = (D_MODEL // 2) // HEAD_DIM
N_KV = 4
GROUP = N_HEADS // N_KV
ROT_DIM = HEAD_DIM // 4
ROPE_THETA = 500000.0
WINDOW = 128
W_CHUNKS = WINDOW // CHUNK
EPS = 1e-6
Q_W = N_HEADS * HEAD_DIM
KV_W = N_KV * HEAD_DIM
IN_SPLITS = (D_CONV, 2 * D_CONV, 2 * D_CONV + Q_W, 2 * D_CONV + Q_W + KV_W,
             2 * D_CONV + Q_W + 2 * KV_W, 2 * D_CONV + Q_W + 2 * KV_W + D_MODEL)
IN_COLS = 2 * D_CONV + Q_W + 2 * KV_W + 2 * D_MODEL
NEG = -1e30

kernel_name = "hybrid_conformer_swa_sink_stream_step"


def rms_norm(x, g):
    xf = x.astype(jnp.float32)
    y = xf * lax.rsqrt(jnp.mean(xf * xf, -1, keepdims=True) + EPS)
    return (y * g.astype(jnp.float32)).astype(x.dtype)


def layer_norm(x, g, b):
    xf = x.astype(jnp.float32)
    mu = jnp.mean(xf, -1, keepdims=True)
    xc = xf - mu
    y = xc * lax.rsqrt(jnp.mean(xc * xc, -1, keepdims=True) + EPS)
    return (y * g.astype(jnp.float32) + b.astype(jnp.float32)).astype(x.dtype)


def swiglu(x, w_gu, w_down):
    g, u = jnp.split(x @ w_gu, 2, axis=-1)
    return (jax.nn.silu(g) * u) @ w_down


def rotary(x, pos):
    half = ROT_DIM // 2
    inv = ROPE_THETA ** (-jnp.arange(0, ROT_DIM, 2, dtype=jnp.float32) / ROT_DIM)
    ang = pos.astype(jnp.float32)[:, None] * inv[None, :]
    cos = jnp.cos(ang)[:, None, :]
    sin = jnp.sin(ang)[:, None, :]
    xr = x[..., :ROT_DIM].astype(jnp.float32)
    x1, x2 = xr[..., :half], xr[..., half:]
    rot = jnp.concatenate([x1 * cos - x2 * sin, x2 * cos + x1 * sin], -1)
    return jnp.concatenate([rot.astype(x.dtype), x[..., ROT_DIM:]], -1)


def depthwise_causal_conv(u_ext, w, b):
    y = lax.conv_general_dilated(u_ext, w[:, None, :], (1,), 'VALID',
                                 dimension_numbers=('NWC', 'WIO', 'NWC'),
                                 feature_group_count=u_ext.shape[-1])
    return y + b


def sink_attention(q, k, v, sinks, mask):
    s = jnp.einsum('...qhgd,...khd->...hgqk', q, k).astype(jnp.float32) * (HEAD_DIM ** -0.5)
    if mask is not None:
        s = jnp.where(mask, s, NEG)
    sink = jnp.broadcast_to(sinks.astype(jnp.float32)[..., None, None], s.shape[:-1] + (1,))
    p = jax.nn.softmax(jnp.concatenate([s, sink], -1), axis=-1)[..., :-1]
    return jnp.einsum('...hgqk,...khd->...qhgd', p.astype(v.dtype), v)


def window_attention_prompt(q, k, v, sinks):
    n, t = q.shape[:2]
    nc = t // CHUNK
    qb = q.reshape(n, nc, CHUNK, N_KV, GROUP, HEAD_DIM)
    pad = ((0, 0), (W_CHUNKS * CHUNK, 0), (0, 0), (0, 0))
    kp = jnp.pad(k, pad).reshape(n, nc + W_CHUNKS, CHUNK, N_KV, HEAD_DIM)
    vp = jnp.pad(v, pad).reshape(n, nc + W_CHUNKS, CHUNK, N_KV, HEAD_DIM)
    kb = jnp.concatenate([kp[:, j:j + nc] for j in range(W_CHUNKS + 1)], axis=2)
    vb = jnp.concatenate([vp[:, j:j + nc] for j in range(W_CHUNKS + 1)], axis=2)
    key_chunk = (jnp.arange(nc)[:, None]
                 + (jnp.arange((W_CHUNKS + 1) * CHUNK) // CHUNK)[None, :] - W_CHUNKS)
    mask = (key_chunk >= 0)[:, None, None, None, :]
    o = sink_attention(qb, kb, vb, sinks, mask)
    return o.reshape(n, t, Q_W)


def layer(x, pe, pos, conv_past, k_past, v_past,
          ffn1_norm, ffn1_w_gu, ffn1_w_down, mix_norm, w_in, conv_w, conv_b, conv_ln_g, conv_ln_b,
          conv_w_out, attn_sinks, attn_w_out, w_out, ffn2_norm, ffn2_w_gu, ffn2_w_down,
          ple_norm, ple_w_gate, ple_w_proj):
    n, t = x.shape[:2]
    x = x + 0.5 * swiglu(rms_norm(x, ffn1_norm), ffn1_w_gu, ffn1_w_down)
    h = rms_norm(x, mix_norm)
    a_lin, a_gate, q, k, v, g_conv, g_attn = jnp.split(h @ w_in, IN_SPLITS, axis=-1)
    u = a_lin * jax.nn.sigmoid(a_gate)
    u_ext = jnp.concatenate([conv_past, u], axis=1)
    y_conv = jax.nn.silu(layer_norm(depthwise_causal_conv(u_ext, conv_w, conv_b),
                                    conv_ln_g, conv_ln_b)) @ conv_w_out
    new_conv = u_ext[:, -(CONV_WIDTH - 1):]
    q = rotary(q.reshape(n, t, N_HEADS, HEAD_DIM), pos)
    k = rotary(k.reshape(n, t, N_KV, HEAD_DIM), pos)
    v = v.reshape(n, t, N_KV, HEAD_DIM)
    sinks = attn_sinks.reshape(N_KV, GROUP)
    if k_past is None:
        o = window_attention_prompt(q, k, v, sinks)
        new_k, new_v = k[:, -WINDOW:], v[:, -WINDOW:]
    else:
        kc = jnp.concatenate([k_past, k], axis=1)
        vc = jnp.concatenate([v_past, v], axis=1)
        o = sink_attention(q.reshape(n, t, N_KV, GROUP, HEAD_DIM), kc, vc, sinks, None)
        o = o.reshape(n, t, Q_W)
        w_rows = k_past.shape[1]
        new_k, new_v = kc[:, -w_rows:], vc[:, -w_rows:]
    y_attn = o @ attn_w_out
    x = x + (jax.nn.sigmoid(g_conv) * y_conv + jax.nn.sigmoid(g_attn) * y_attn) @ w_out
    x = x + 0.5 * swiglu(rms_norm(x, ffn2_norm), ffn2_w_gu, ffn2_w_down)
    x = x + jax.nn.sigmoid(rms_norm(x, ple_norm) @ ple_w_gate) * (pe @ ple_w_proj)
    return x, new_conv, new_k, new_v


def setup_inputs(seed: int = 0) -> dict:
    key = jax.random.key(seed)
    ks = iter(jax.random.split(key, 40))
    f32 = jnp.float32
    w_rows = min(WINDOW, PAST_LEN)

    def nrm(shape, scale=1.0):
        return jax.random.normal(next(ks), shape, f32) * scale

    def gain(shape):
        return 1.0 + nrm(shape, 0.02)

    return {
        "x_prompt": nrm((BATCH, SEQ, D_MODEL)),
        "x_sample": nrm((DEC_BATCH, DEC_SEQ, D_MODEL)),
        "p_prompt": nrm((DEPTH, BATCH, SEQ, D_PLE)),
        "p_sample": nrm((DEPTH, DEC_BATCH, DEC_SEQ, D_PLE)),
        "state_conv": nrm((DEPTH, DEC_BATCH, CONV_WIDTH - 1, D_CONV)),
        "cache_k": nrm((DEPTH, DEC_BATCH, w_rows, N_KV, HEAD_DIM)),
        "cache_v": nrm((DEPTH, DEC_BATCH, w_rows, N_KV, HEAD_DIM)),
        "ffn1_norm": gain((DEPTH, D_MODEL)),
        "ffn1_w_gu": nrm((DEPTH, D_MODEL, 2 * D_FF), D_MODEL ** -0.5),
        "ffn1_w_down": nrm((DEPTH, D_FF, D_MODEL), D_FF ** -0.5),
        "mix_norm": gain((DEPTH, D_MODEL)),
        "w_in": nrm((DEPTH, D_MODEL, IN_COLS), D_MODEL ** -0.5),
        "conv_w": nrm((DEPTH, CONV_WIDTH, D_CONV), CONV_WIDTH ** -0.5),
        "conv_b": nrm((DEPTH, D_CONV), 0.02),
        "conv_ln_g": gain((DEPTH, D_CONV)),
        "conv_ln_b": nrm((DEPTH, D_CONV), 0.02),
        "conv_w_out": nrm((DEPTH, D_CONV, D_MODEL), D_CONV ** -0.5),
        "attn_sinks": nrm((DEPTH, N_HEADS), 0.5),
        "attn_w_out": nrm((DEPTH, Q_W, D_MODEL), Q_W ** -0.5),
        "w_out": nrm((DEPTH, D_MODEL, D_MODEL), D_MODEL ** -0.5),
        "ffn2_norm": gain((DEPTH, D_MODEL)),
        "ffn2_w_gu": nrm((DEPTH, D_MODEL, 2 * D_FF), D_MODEL ** -0.5),
        "ffn2_w_down": nrm((DEPTH, D_FF, D_MODEL), D_FF ** -0.5),
        "ple_norm": gain((DEPTH, D_MODEL)),
        "ple_w_gate": nrm((DEPTH, D_MODEL, D_MODEL), D_MODEL ** -0.5),
        "ple_w_proj": nrm((DEPTH, D_PLE, D_MODEL), D_PLE ** -0.5),
        "final_norm": gain((D_MODEL,)),
    }


def reference(x_prompt, x_sample, p_prompt, p_sample, state_conv, cache_k, cache_v,
              ffn1_norm, ffn1_w_gu, ffn1_w_down, mix_norm, w_in, conv_w, conv_b, conv_ln_g, conv_ln_b,
              conv_w_out, attn_sinks, attn_w_out, w_out, ffn2_norm, ffn2_w_gu, ffn2_w_down,
              ple_norm, ple_w_gate, ple_w_proj, final_norm):
    pos_p = jnp.arange(x_prompt.shape[1], dtype=jnp.int32)
    pos_s = PAST_LEN + jnp.arange(x_sample.shape[1], dtype=jnp.int32)
    zero_conv = jnp.zeros((x_prompt.shape[0], CONV_WIDTH - 1, D_CONV), x_prompt.dtype)
    hp, hs = x_prompt, x_sample
    kp_l, vp_l, cp_l, ks_l, vs_l, cs_l = [], [], [], [], [], []
    for i in range(DEPTH):
        w = (ffn1_norm[i], ffn1_w_gu[i], ffn1_w_down[i], mix_norm[i], w_in[i], conv_w[i], conv_b[i],
             conv_ln_g[i], conv_ln_b[i], conv_w_out[i], attn_sinks[i], attn_w_out[i], w_out[i],
             ffn2_norm[i], ffn2_w_gu[i], ffn2_w_down[i], ple_norm[i], ple_w_gate[i], ple_w_proj[i])
        hp, c_p, k_p, v_p = layer(hp, p_prompt[i], pos_p, zero_conv, None, None, *w)
        hs, c_s, k_s, v_s = layer(hs, p_sample[i], pos_s, state_conv[i], cache_k[i], cache_v[i], *w)
        kp_l.append(k_p); vp_l.append(v_p); cp_l.append(c_p)
        ks_l.append(k_s); vs_l.append(v_s); cs_l.append(c_s)
    y_prompt = rms_norm(hp, final_norm)
    y_sample = rms_norm(hs, final_norm)
    return (y_prompt, y_sample, jnp.stack(kp_l), jnp.stack(vp_l), jnp.stack(cp_l),
            jnp.stack(ks_l), jnp.stack(vs_l), jnp.stack(cs_l))
```

```python
import functools

import jax
import jax.numpy as jnp
from jax import lax
from jax.experimental import pallas as pl
from jax.experimental.pallas import tpu as pltpu

D_MODEL = 2048
D_PLE = 256
D_FF = 5504
D_CONV = 1024
CONV_WIDTH = 31
HEAD_DIM = 64
N_HEADS = 16
N_KV = 4
GROUP = N_HEADS // N_KV
ROT_DIM = 16
ROPE_THETA = 500000.0
CHUNK = 64
WINDOW = 128
PAST_LEN = 1024
EPS = 1e-6
NEG = -1e30
Q_W = N_HEADS * HEAD_DIM
KV_W = N_KV * HEAD_DIM

LANES = 128
FF_TILE = 512
D_FF_PAD = 5632
ROW_TILE = 512
CONV_HALO = 32
VMEM_LIMIT = 56 * 1024 * 1024

F32 = jnp.float32
BF16 = jnp.bfloat16


def _params(*sem):
    return pltpu.CompilerParams(dimension_semantics=sem, vmem_limit_bytes=VMEM_LIMIT)


def _rms(x, g):
    return x * lax.rsqrt(jnp.mean(x * x, axis=-1, keepdims=True) + EPS) * g


def _dot(a, b):
    return jnp.dot(a, b, preferred_element_type=F32)


def _ffn_body(load_x, g1_ref, wg_ref, wu_ref, wd_ref, g2_ref, xo_ref, hn_ref, xn_ref):
    f = pl.program_id(1)

    @pl.when(f == 0)
    def _():
        x = load_x()
        xn_ref[...] = _rms(x, g1_ref[...]).astype(BF16)
        xo_ref[...] = x

    xn = xn_ref[...]
    g = _dot(xn, wg_ref[...])
    u = _dot(xn, wu_ref[...])
    h = (g * jax.nn.sigmoid(g)) * (u * 0.5)
    xo_ref[...] += _dot(h.astype(BF16), wd_ref[...])

    if hn_ref is not None:
        @pl.when(f == pl.num_programs(1) - 1)
        def _():
            hn_ref[...] = _rms(xo_ref[...], g2_ref[...]).astype(BF16)


def _ffn1_kernel(n_prompt_tiles, xp_ref, xs_ref, g1_ref, wg_ref, wu_ref, wd_ref, g2_ref,
                 xo_ref, hn_ref, xn_ref):
    i = pl.program_id(0)

    def load_x():
        return jnp.where(i < n_prompt_tiles, xp_ref[...], xs_ref[...])

    _ffn_body(load_x, g1_ref, wg_ref, wu_ref, wd_ref, g2_ref, xo_ref, hn_ref, xn_ref)


def _ffn2_kernel(x_ref, g1_ref, wg_ref, wu_ref, wd_ref, xo_ref, xn_ref):
    _ffn_body(lambda: x_ref[...], g1_ref, wg_ref, wu_ref, wd_ref, None, xo_ref, None, xn_ref)


def _ffn_weight_specs():
    nf = D_FF_PAD // FF_TILE
    return [
        pl.BlockSpec((D_MODEL, FF_TILE), lambda i, f: (0, f)),
        pl.BlockSpec((D_MODEL, FF_TILE), lambda i, f: (0, f)),
        pl.BlockSpec((FF_TILE, D_MODEL), lambda i, f: (f, 0)),
    ], nf


def _ffn1(xp, xs, g1, wg, wu, wd, g2):
    n_p, n_s = xp.shape[0] // ROW_TILE, xs.shape[0] // ROW_TILE
    assert n_s == 1 and xs.shape[0] == ROW_TILE
    m = xp.shape[0] + xs.shape[0]
    wspecs, nf = _ffn_weight_specs()
    vec = pl.BlockSpec((1, D_MODEL), lambda i, f: (0, 0))
    row = pl.BlockSpec((ROW_TILE, D_MODEL), lambda i, f: (i, 0))
    return pl.pallas_call(
        functools.partial(_ffn1_kernel, n_p),
        grid=(n_p + n_s, nf),
        in_specs=[pl.BlockSpec((ROW_TILE, D_MODEL), lambda i, f: (jnp.minimum(i, n_p - 1), 0)),
                  pl.BlockSpec((ROW_TILE, D_MODEL), lambda i, f: (0, 0)),
                  vec, *wspecs, vec],
        out_specs=[row, row],
        out_shape=[jax.ShapeDtypeStruct((m, D_MODEL), F32), jax.ShapeDtypeStruct((m, D_MODEL), BF16)],
        scratch_shapes=[pltpu.VMEM((ROW_TILE, D_MODEL), BF16)],
        compiler_params=_params("parallel", "arbitrary"),
        name="ffn1",
    )(xp, xs, g1, wg, wu, wd, g2)


def _ffn2(x, g1, wg, wu, wd):
    m = x.shape[0]
    wspecs, nf = _ffn_weight_specs()
    vec = pl.BlockSpec((1, D_MODEL), lambda i, f: (0, 0))
    row = pl.BlockSpec((ROW_TILE, D_MODEL), lambda i, f: (i, 0))
    return pl.pallas_call(
        _ffn2_kernel,
        grid=(m // ROW_TILE, nf),
        in_specs=[row, vec, *wspecs],
        out_specs=row,
        out_shape=jax.ShapeDtypeStruct((m, D_MODEL), F32),
        scratch_shapes=[pltpu.VMEM((ROW_TILE, D_MODEL), BF16)],
        compiler_params=_params("parallel", "arbitrary"),
        name="ffn2",
    )(x, g1, wg, wu, wd)


def _glu_kernel(h_ref, wa_ref, wb_ref, o_ref):
    h = h_ref[...]
    o_ref[...] = _dot(h, wa_ref[...]) * jax.nn.sigmoid(_dot(h, wb_ref[...]))


def _glu(hn, w_lin, w_gate, tn=512):
    m = hn.shape[0]
    return pl.pallas_call(
        _glu_kernel,
        grid=(m // ROW_TILE, D_CONV // tn),
        in_specs=[pl.BlockSpec((ROW_TILE, D_MODEL), lambda i, j: (i, 0)),
                  pl.BlockSpec((D_MODEL, tn), lambda i, j: (0, j)),
                  pl.BlockSpec((D_MODEL, tn), lambda i, j: (0, j))],
        out_specs=pl.BlockSpec((ROW_TILE, tn), lambda i, j: (i, j)),
        out_shape=jax.ShapeDtypeStruct((m, D_CONV), F32),
        compiler_params=_params("parallel", "arbitrary"),
        name="glu",
    )(hn, w_lin, w_gate)


def _sigmoid_proj_kernel(h_ref, w_ref, o_ref):
    o_ref[...] = jax.nn.sigmoid(_dot(h_ref[...], w_ref[...]))


def _sigmoid_proj(hn, w, tn=512):
    m, n = hn.shape[0], w.shape[1]
    return pl.pallas_call(
        _sigmoid_proj_kernel,
        grid=(m // ROW_TILE, n // tn),
        in_specs=[pl.BlockSpec((ROW_TILE, D_MODEL), lambda i, j: (i, 0)),
                  pl.BlockSpec((D_MODEL, tn), lambda i, j: (0, j))],
        out_specs=pl.BlockSpec((ROW_TILE, tn), lambda i, j: (i, j)),
        out_shape=jax.ShapeDtypeStruct((m, n), F32),
        compiler_params=_params("parallel", "arbitrary"),
        name="gates",
    )(hn, w)


def _rotate(x, cos, sin_lo, sin_hi):
    cols = []
    for c in range(x.shape[1] // LANES):
        xb = x[:, c * LANES:(c + 1) * LANES]
        cols.append(xb * cos
                    + pltpu.roll(xb, LANES - ROT_DIM // 2, 1) * sin_lo
                    + pltpu.roll(xb, ROT_DIM // 2, 1) * sin_hi)
    return jnp.concatenate(cols, axis=1)


def _qkv_kernel(h_ref, w_ref, cos_ref, slo_ref, shi_ref, q_ref, k_ref, v_ref):
    z = _dot(h_ref[...], w_ref[...])
    cos, slo, shi = cos_ref[...], slo_ref[...], shi_ref[...]
    q = _rotate(z[:, :Q_W], cos, slo, shi)
    q_ref[...] = (q * (HEAD_DIM ** -0.5)).astype(BF16)
    k_ref[...] = _rotate(z[:, Q_W:Q_W + KV_W], cos, slo, shi)
    v_ref[...] = z[:, Q_W + KV_W:]


def _qkv(hn, w, cos, slo, shi, n_prompt_tiles, prompt_tiles_per_seq):
    m = hn.shape[0]
    n = Q_W + 2 * KV_W

    def tab_idx(i):
        return (jnp.where(i < n_prompt_tiles, i % prompt_tiles_per_seq, prompt_tiles_per_seq), 0)

    tab = pl.BlockSpec((ROW_TILE, LANES), tab_idx)
    return pl.pallas_call(
        _qkv_kernel,
        grid=(m // ROW_TILE,),
        in_specs=[pl.BlockSpec((ROW_TILE, D_MODEL), lambda i: (i, 0)),
                  pl.BlockSpec((D_MODEL, n), lambda i: (0, 0)),
                  tab, tab, tab],
        out_specs=[pl.BlockSpec((ROW_TILE, Q_W), lambda i: (i, 0)),
                   pl.BlockSpec((ROW_TILE, KV_W), lambda i: (i, 0)),
                   pl.BlockSpec((ROW_TILE, KV_W), lambda i: (i, 0))],
        out_shape=[jax.ShapeDtypeStruct((m, Q_W), BF16),
                   jax.ShapeDtypeStruct((m, KV_W), F32),
                   jax.ShapeDtypeStruct((m, KV_W), F32)],
        compiler_params=_params("parallel"),
        name="qkv",
    )(hn, w, cos, slo, shi)


def _rope_tables(seq, dec_seq, n_dec_rows):
    half = ROT_DIM // 2
    inv = ROPE_THETA ** (-jnp.arange(0, ROT_DIM, 2, dtype=F32) / ROT_DIM)

    def tables(pos):
        ang = pos.astype(F32)[:, None] * inv[None, :]
        cos, sin = jnp.cos(ang), jnp.sin(ang)
        one = jnp.ones((pos.shape[0], HEAD_DIM - ROT_DIM), F32)
        zero = jnp.zeros_like(one)
        zh = jnp.zeros_like(sin)
        c = jnp.concatenate([cos, cos, one], 1)
        lo = jnp.concatenate([-sin, zh, zero], 1)
        hi = jnp.concatenate([zh, sin, zero], 1)
        return [jnp.tile(t, (1, LANES // HEAD_DIM)) for t in (c, lo, hi)]

    tp = tables(jnp.arange(seq, dtype=jnp.int32))
    ts = tables(PAST_LEN + jnp.arange(dec_seq, dtype=jnp.int32))
    reps = n_dec_rows // dec_seq
    return [jnp.concatenate([a, jnp.tile(b, (reps, 1))], 0) for a, b in zip(tp, ts)]


def _conv_kernel(first_every, u_ref, past_ref, w_ref, b_ref, lg_ref, lb_ref, o_ref, s_ref):
    rows = u_ref.shape[0]
    if first_every is None:
        s_ref[CONV_HALO - (CONV_WIDTH - 1):CONV_HALO, :] = past_ref[0]
    else:
        first = pl.program_id(0) % first_every == 0
        s_ref[0:CONV_HALO, :] = jnp.where(first, 0.0, past_ref[...])
    s_ref[CONV_HALO:, :] = u_ref[...]
    off = CONV_HALO - (CONV_WIDTH - 1)
    acc = jnp.broadcast_to(b_ref[...], (rows, D_CONV))
    for j in range(CONV_WIDTH):
        acc = acc + s_ref[off + j:off + j + rows, :] * w_ref[j:j + 1, :]
    mu = jnp.mean(acc, axis=-1, keepdims=True)
    xc = acc - mu
    y = xc * lax.rsqrt(jnp.mean(xc * xc, axis=-1, keepdims=True) + EPS) * lg_ref[...] + lb_ref[...]
    o_ref[...] = (y * jax.nn.sigmoid(y)).astype(BF16)


def _conv_common_specs():
    return [pl.BlockSpec((CONV_WIDTH, D_CONV), lambda i: (0, 0)),
            pl.BlockSpec((1, D_CONV), lambda i: (0, 0)),
            pl.BlockSpec((1, D_CONV), lambda i: (0, 0)),
            pl.BlockSpec((1, D_CONV), lambda i: (0, 0))]


def _conv_prompt(u, n_rows, seq, w, b, lg, lb, rows=128):
    per_seq = seq // rows
    halo_per_block = rows // CONV_HALO
    return pl.pallas_call(
        functools.partial(_conv_kernel, per_seq),
        grid=(n_rows // rows,),
        in_specs=[pl.BlockSpec((rows, D_CONV), lambda i: (i, 0)),
                  pl.BlockSpec((CONV_HALO, D_CONV), lambda i: (jnp.maximum(i * halo_per_block - 1, 0), 0)),
                  *_conv_common_specs()],
        out_specs=pl.BlockSpec((rows, D_CONV), lambda i: (i, 0)),
        out_shape=jax.ShapeDtypeStruct((n_rows, D_CONV), BF16),
        scratch_shapes=[pltpu.VMEM((CONV_HALO + rows, D_CONV), F32)],
        compiler_params=_params("parallel"),
        name="conv_prompt",
    )(u, u, w, b, lg, lb)


def _conv_sample(u, row0, state, w, b, lg, lb):
    n_seq, hist, _ = state.shape
    assert hist == CONV_WIDTH - 1
    rows = CHUNK
    blk0 = row0 // rows
    return pl.pallas_call(
        functools.partial(_conv_kernel, None),
        grid=(n_seq,),
        in_specs=[pl.BlockSpec((rows, D_CONV), lambda i: (blk0 + i, 0)),
                  pl.BlockSpec((1, hist, D_CONV), lambda i: (i, 0, 0)),
                  *_conv_common_specs()],
        out_specs=pl.BlockSpec((rows, D_CONV), lambda i: (i, 0)),
        out_shape=jax.ShapeDtypeStruct((n_seq * rows, D_CONV), BF16),
        scratch_shapes=[pltpu.VMEM((CONV_HALO + rows, D_CONV), F32)],
        compiler_params=_params("parallel"),
        name="conv_sample",
    )(u, state, w, b, lg, lb)


def _attn_kernel(blocks_per_seq, sink_ref, q_ref, kp_ref, kc_ref, vp_ref, vc_ref, o_ref):
    qb = q_ref.shape[0]
    prefix = kp_ref.shape[-2]
    kp = kp_ref[...].reshape(prefix, KV_W)
    vp = vp_ref[...].reshape(prefix, KV_W)
    k = jnp.concatenate([kp, kc_ref[...]], axis=0).astype(BF16)
    v = jnp.concatenate([vp, vc_ref[...]], axis=0).astype(BF16)
    nk = prefix + qb
    if qb > CHUNK or blocks_per_seq is not None:
        d = (lax.broadcasted_iota(jnp.int32, (qb, nk), 1) // CHUNK
             - lax.broadcasted_iota(jnp.int32, (qb, nk), 0) // CHUNK)
        valid = (d >= 0) & (d <= WINDOW // CHUNK)
        if blocks_per_seq is not None:
            first = pl.program_id(0) % blocks_per_seq == 0
            col = lax.broadcasted_iota(jnp.int32, (qb, nk), 1)
            valid = valid & ((col >= prefix) | jnp.logical_not(first))
    else:
        valid = None
    for h in range(N_HEADS):
        kv = h // GROUP
        qh = q_ref[:, h * HEAD_DIM:(h + 1) * HEAD_DIM]
        kh = k[:, kv * HEAD_DIM:(kv + 1) * HEAD_DIM]
        vh = v[:, kv * HEAD_DIM:(kv + 1) * HEAD_DIM]
        s = lax.dot_general(qh, kh, (((1,), (1,)), ((), ())), preferred_element_type=F32)
        if valid is not None:
            s = jnp.where(valid, s, NEG)
        sink = sink_ref[h]
        mx = jnp.maximum(jnp.max(s, axis=-1, keepdims=True), sink)
        e = jnp.exp(s - mx)
        den = jnp.sum(e, axis=-1, keepdims=True) + jnp.exp(sink - mx)
        p = (e / den).astype(BF16)
        o_ref[:, h * HEAD_DIM:(h + 1) * HEAD_DIM] = _dot(p, vh).astype(BF16)


def _attn_prompt(q, k, v, sinks, n_rows, seq, qb=256):
    per_seq = seq // qb
    pre_per_blk = qb // WINDOW
    prev = pl.BlockSpec((WINDOW, KV_W), lambda i: (jnp.maximum(i * pre_per_blk - 1, 0), 0))
    cur = pl.BlockSpec((qb, KV_W), lambda i: (i, 0))
    return pl.pallas_call(
        functools.partial(_attn_kernel, per_seq),
        grid=(n_rows // qb,),
        in_specs=[pl.BlockSpec(memory_space=pltpu.SMEM),
                  pl.BlockSpec((qb, Q_W), lambda i: (i, 0)), prev, cur, prev, cur],
        out_specs=pl.BlockSpec((qb, Q_W), lambda i: (i, 0)),
        out_shape=jax.ShapeDtypeStruct((n_rows, Q_W), BF16),
        compiler_params=_params("parallel"),
        name="attn_prompt",
    )(sinks, q, k, k, v, v)


def _attn_sample(q, k, v, sinks, row0, cache_k, cache_v):
    n_seq, w_rows, _ = cache_k.shape
    assert w_rows == WINDOW
    blk0 = row0 // CHUNK
    cache = pl.BlockSpec((1, w_rows, KV_W), lambda i: (i, 0, 0))
    cur = pl.BlockSpec((CHUNK, KV_W), lambda i: (blk0 + i, 0))
    return pl.pallas_call(
        functools.partial(_attn_kernel, None),
        grid=(n_seq,),
        in_specs=[pl.BlockSpec(memory_space=pltpu.SMEM),
                  pl.BlockSpec((CHUNK, Q_W), lambda i: (blk0 + i, 0)), cache, cur, cache, cur],
        out_specs=pl.BlockSpec((CHUNK, Q_W), lambda i: (i, 0)),
        out_shape=jax.ShapeDtypeStruct((n_seq * CHUNK, Q_W), BF16),
        compiler_params=_params("parallel"),
        name="attn_sample",
    )(sinks, q, cache_k, k, cache_v, v)


def _merge_kernel(yc_ref, o_ref, wc_ref, wa_ref, gc_ref, ga_ref, m_ref):
    m = gc_ref[...] * _dot(yc_ref[...], wc_ref[...]) + ga_ref[...] * _dot(o_ref[...], wa_ref[...])
    m_ref[...] = m.astype(BF16)


def _merge(yc, o, wc, wa, gates, tn=512):
    m = yc.shape[0]
    nj = D_MODEL // tn
    return pl.pallas_call(
        _merge_kernel,
        grid=(m // ROW_TILE, nj),
        in_specs=[pl.BlockSpec((ROW_TILE, D_CONV), lambda i, j: (i, 0)),
                  pl.BlockSpec((ROW_TILE, Q_W), lambda i, j: (i, 0)),
                  pl.BlockSpec((D_CONV, tn), lambda i, j: (0, j)),
                  pl.BlockSpec((Q_W, tn), lambda i, j: (0, j)),
                  pl.BlockSpec((ROW_TILE, tn), lambda i, j: (i, j)),
                  pl.BlockSpec((ROW_TILE, tn), lambda i, j: (i, j + nj))],
        out_specs=pl.BlockSpec((ROW_TILE, tn), lambda i, j: (i, j)),
        out_shape=jax.ShapeDtypeStruct((m, D_MODEL), BF16),
        compiler_params=_params("parallel", "arbitrary"),
        name="merge",
    )(yc, o, wc, wa, gates, gates)


def _resid_proj_kernel(x_ref, m_ref, w_ref, o_ref):
    o_ref[...] = x_ref[...] + _dot(m_ref[...], w_ref[...])


def _resid_proj(x, mm, w, tn=512):
    m = x.shape[0]
    return pl.pallas_call(
        _resid_proj_kernel,
        grid=(m // ROW_TILE, D_MODEL // tn),
        in_specs=[pl.BlockSpec((ROW_TILE, tn), lambda i, j: (i, j)),
                  pl.BlockSpec((ROW_TILE, D_MODEL), lambda i, j: (i, 0)),
                  pl.BlockSpec((D_MODEL, tn), lambda i, j: (0, j))],
        out_specs=pl.BlockSpec((ROW_TILE, tn), lambda i, j: (i, j)),
        out_shape=jax.ShapeDtypeStruct((m, D_MODEL), F32),
        compiler_params=_params("parallel", "arbitrary"),
        name="out_proj",
    )(x, mm, w)


def _ple_kernel(n_prompt_tiles, x_ref, pp_ref, ps_ref, gn_ref, wg_ref, wp_ref, fn_ref, yp_ref, ys_ref):
    i = pl.program_id(0)
    x = x_ref[...]
    gate = jax.nn.sigmoid(_dot(_rms(x, gn_ref[...]).astype(BF16), wg_ref[...]))
    pe = jnp.where(i < n_prompt_tiles, pp_ref[...], ps_ref[...]).astype(BF16)
    y = _rms(x + gate * _dot(pe, wp_ref[...]), fn_ref[...])

    @pl.when(i < n_prompt_tiles)
    def _():
        yp_ref[...] = y

    @pl.when(i >= n_prompt_tiles)
    def _():
        ys_ref[...] = y


def _ple(x, pp, ps, gn, wg, wp, fn, tm=256):
    n_p, n_s = pp.shape[0] // tm, ps.shape[0] // tm
    vec = pl.BlockSpec((1, D_MODEL), lambda i: (0, 0))

    def p_idx(i):
        return (jnp.minimum(i, n_p - 1), 0)

    def s_idx(i):
        return (jnp.maximum(i - n_p, 0), 0)

    return pl.pallas_call(
        functools.partial(_ple_kernel, n_p),
        grid=(n_p + n_s,),
        in_specs=[pl.BlockSpec((tm, D_MODEL), lambda i: (i, 0)),
                  pl.BlockSpec((tm, D_PLE), p_idx),
                  pl.BlockSpec((tm, D_PLE), s_idx),
                  vec,
                  pl.BlockSpec((D_MODEL, D_MODEL), lambda i: (0, 0)),
                  pl.BlockSpec((D_PLE, D_MODEL), lambda i: (0, 0)),
                  vec],
        out_specs=[pl.BlockSpec((tm, D_MODEL), p_idx), pl.BlockSpec((tm, D_MODEL), s_idx)],
        out_shape=[jax.ShapeDtypeStruct((pp.shape[0], D_MODEL), F32),
                   jax.ShapeDtypeStruct((ps.shape[0], D_MODEL), F32)],
        compiler_params=_params("arbitrary"),
        name="ple_final",
    )(x, pp, ps, gn, wg, wp, fn)


def _ffn_weights(w_gu, w_down):
    pad = D_FF_PAD - D_FF
    wg = jnp.pad(w_gu[:, :D_FF].astype(BF16), ((0, 0), (0, pad)))
    wu = jnp.pad(w_gu[:, D_FF:].astype(BF16), ((0, 0), (0, pad)))
    wd = jnp.pad(w_down.astype(BF16), ((0, pad), (0, 0)))
    return wg, wu, wd


def kernel(x_prompt, x_sample, p_prompt, p_sample, state_conv, cache_k, cache_v, ffn1_norm, ffn1_w_gu, ffn1_w_down, mix_norm, w_in, conv_w, conv_b, conv_ln_g, conv_ln_b, conv_w_out, attn_sinks, attn_w_out, w_out, ffn2_norm, ffn2_w_gu, ffn2_w_down, ple_norm, ple_w_gate, ple_w_proj, final_norm):
    assert x_prompt.shape[-1] == D_MODEL and ffn1_norm.shape[0] == 1
    batch, seq, _ = x_prompt.shape
    dec_batch, dec_seq, _ = x_sample.shape
    assert dec_seq == CHUNK and seq % ROW_TILE == 0
    n_p, n_s = batch * seq, dec_batch * dec_seq
    n_p_tiles = n_p // ROW_TILE

    def vec(a):
        return a.reshape(1, -1)

    wg1, wu1, wd1 = _ffn_weights(ffn1_w_gu[0], ffn1_w_down[0])
    wg2, wu2, wd2 = _ffn_weights(ffn2_w_gu[0], ffn2_w_down[0])
    wi = w_in[0]
    w_lin = wi[:, :D_CONV].astype(BF16)
    w_gate = wi[:, D_CONV:2 * D_CONV].astype(BF16)
    c0 = 2 * D_CONV
    w_qkv = wi[:, c0:c0 + Q_W + 2 * KV_W].astype(BF16)
    w_gates = wi[:, c0 + Q_W + 2 * KV_W:].astype(BF16)
    w_co = conv_w_out[0].astype(BF16)
    w_ao = attn_w_out[0].astype(BF16)
    w_o = w_out[0].astype(BF16)
    w_pg = ple_w_gate[0].astype(BF16)
    w_pp = ple_w_proj[0].astype(BF16)

    xp = x_prompt.reshape(n_p, D_MODEL)
    xs = x_sample.reshape(n_s, D_MODEL)

    x1, hn = _ffn1(xp, xs, vec(ffn1_norm), wg1, wu1, wd1, vec(mix_norm))

    u = _glu(hn, w_lin, w_gate)
    cos, slo, shi = _rope_tables(seq, dec_seq, n_s)
    q, k, v = _qkv(hn, w_qkv, cos, slo, shi, n_p_tiles, seq // ROW_TILE)
    gates = _sigmoid_proj(hn, w_gates)

    cw, cb, lg, lb = conv_w[0], vec(conv_b), vec(conv_ln_g), vec(conv_ln_b)
    yc = jnp.concatenate([_conv_prompt(u, n_p, seq, cw, cb, lg, lb),
                          _conv_sample(u, n_p, state_conv[0], cw, cb, lg, lb)], axis=0)

    sinks = attn_sinks[0]
    ck = cache_k[0].reshape(dec_batch, -1, KV_W)
    cv = cache_v[0].reshape(dec_batch, -1, KV_W)
    o = jnp.concatenate([_attn_prompt(q, k, v, sinks, n_p, seq),
                         _attn_sample(q, k, v, sinks, n_p, ck, cv)], axis=0)

    mm = _merge(yc, o, w_co, w_ao, gates)
    x2 = _resid_proj(x1, mm, w_o)
    x3 = _ffn2(x2, vec(ffn2_norm), wg2, wu2, wd2)

    yp, ys = _ple(x3, p_prompt[0].reshape(n_p, D_PLE), p_sample[0].reshape(n_s, D_PLE),
                  vec(ple_norm), w_pg, w_pp, vec(final_norm))

    w_rows = ck.shape[1]
    k_p = k[:n_p].reshape(batch, seq, N_KV, HEAD_DIM)[:, -WINDOW:]
    v_p = v[:n_p].reshape(batch, seq, N_KV, HEAD_DIM)[:, -WINDOW:]
    c_p = u[:n_p].reshape(batch, seq, D_CONV)[:, -(CONV_WIDTH - 1):]
    k_s = jnp.concatenate([cache_k[0], k[n_p:].reshape(dec_batch, dec_seq, N_KV, HEAD_DIM)], 1)[:, -w_rows:]
    v_s = jnp.concatenate([cache_v[0], v[n_p:].reshape(dec_batch, dec_seq, N_KV, HEAD_DIM)], 1)[:, -w_rows:]
    c_s = jnp.concatenate([state_conv[0], u[n_p:].reshape(dec_batch, dec_seq, D_CONV)], 1)[:, -(CONV_WIDTH - 1):]
    return (yp.reshape(batch, seq, D_MODEL), ys.reshape(dec_batch, dec_seq, D_MODEL),
            k_p[None], v_p[None], c_p[None], k_s[None], v_s[None], c_s[None])
```

```python
import functools

import jax
import jax.numpy as jnp
from jax import lax
from jax.experimental import pallas as pl
from jax.experimental.pallas import tpu as pltpu

D_MODEL = 2048
D_PLE = 256
D_FF = 5504
D_CONV = 1024
CONV_WIDTH = 31
HEAD_DIM = 64
N_HEADS = 16
N_KV = 4
GROUP = N_HEADS // N_KV
ROT_DIM = 16
ROPE_THETA = 500000.0
CHUNK = 64
WINDOW = 128
PAST_LEN = 1024
EPS = 1e-6
NEG = -1e30
Q_W = N_HEADS * HEAD_DIM
KV_W = N_KV * HEAD_DIM
IN_COLS = 2 * D_CONV + Q_W + 2 * KV_W + 2 * D_MODEL

LANES = 128
SUBLANES = 8
FF_TILE = 512
FF_STEPS = -(-D_FF // FF_TILE)
FF_OVERLAP = FF_STEPS * FF_TILE - D_FF
ROW_TILE = 512
IN_TILE = 1088
IN_COL_TILE = 512
POST_TILE = 256
CONV_HALO = 32
CONV_ROWS = 128
ATTN_ROWS = WINDOW
VMEM_LIMIT = 56 * 1024 * 1024

F32 = jnp.float32
BF16 = jnp.bfloat16


def _params(*sem):
    return pltpu.CompilerParams(dimension_semantics=sem, vmem_limit_bytes=VMEM_LIMIT)


def _rms(x, g):
    return x * lax.rsqrt(jnp.mean(x * x, axis=-1, keepdims=True) + EPS) * g


def _dot(a, b):
    return jnp.dot(a, b, preferred_element_type=F32)


def _resident(shape):
    return pl.BlockSpec(shape, lambda *_: (0,) * len(shape), pipeline_mode=pl.Buffered(1))


def _ffn_body(load_x, g1_ref, wg_ref, wu_ref, wd_ref, g2_ref, xo_ref, hn_ref, xn_ref):
    f = pl.program_id(1)

    @pl.when(f == 0)
    def _():
        x = load_x()
        xn_ref[...] = _rms(x, g1_ref[...]).astype(BF16)
        xo_ref[...] = x

    xn = xn_ref[...]
    g = _dot(xn, wg_ref[...])
    u = _dot(xn, wu_ref[...])
    h = (g * jax.nn.sigmoid(g)) * (u * 0.5)
    col = lax.broadcasted_iota(jnp.int32, (1, FF_TILE), 1)
    h = jnp.where((col >= FF_OVERLAP) | (f < FF_STEPS - 1), h, 0.0)
    xo_ref[...] += _dot(h.astype(BF16), wd_ref[...])

    if hn_ref is not None:
        @pl.when(f == pl.num_programs(1) - 1)
        def _():
            hn_ref[...] = _rms(xo_ref[...], g2_ref[...]).astype(BF16)


def _ffn1_kernel(n_prompt_tiles, xp_ref, xs_ref, g1_ref, wg_ref, wu_ref, wd_ref, g2_ref,
                 xo_ref, hn_ref, xn_ref):
    i = pl.program_id(0)

    def load_x():
        return jnp.where(i < n_prompt_tiles, xp_ref[...], xs_ref[...])

    _ffn_body(load_x, g1_ref, wg_ref, wu_ref, wd_ref, g2_ref, xo_ref, hn_ref, xn_ref)


def _ffn2_kernel(x_ref, g1_ref, wg_ref, wu_ref, wd_ref, xo_ref, xn_ref):
    _ffn_body(lambda: x_ref[...], g1_ref, wg_ref, wu_ref, wd_ref, None, xo_ref, None, xn_ref)


def _ffn_weight_specs():
    def start(f):
        return pl.multiple_of(jnp.minimum(f * FF_TILE, D_FF - FF_TILE), LANES)

    cols = (pl.Element(D_MODEL), pl.Element(FF_TILE))
    rows = (pl.Element(FF_TILE), pl.Element(D_MODEL))
    return [
        pl.BlockSpec(cols, lambda i, f: (0, start(f))),
        pl.BlockSpec(cols, lambda i, f: (0, pl.multiple_of(D_FF + start(f), LANES))),
        pl.BlockSpec(rows, lambda i, f: (start(f), 0)),
    ], FF_STEPS


def _ffn1(xp, xs, g1, wgu, wd, g2):
    n_p, n_s = xp.shape[0] // ROW_TILE, xs.shape[0] // ROW_TILE
    assert n_s == 1 and xs.shape[0] == ROW_TILE
    m = xp.shape[0] + xs.shape[0]
    wspecs, nf = _ffn_weight_specs()
    vec = pl.BlockSpec((1, D_MODEL), lambda i, f: (0, 0))
    row = pl.BlockSpec((ROW_TILE, D_MODEL), lambda i, f: (i, 0))
    return pl.pallas_call(
        functools.partial(_ffn1_kernel, n_p),
        grid=(n_p + n_s, nf),
        in_specs=[pl.BlockSpec((ROW_TILE, D_MODEL), lambda i, f: (jnp.minimum(i, n_p - 1), 0)),
                  pl.BlockSpec((ROW_TILE, D_MODEL), lambda i, f: (0, 0)),
                  vec, *wspecs, vec],
        out_specs=[row, row],
        out_shape=[jax.ShapeDtypeStruct((m, D_MODEL), F32), jax.ShapeDtypeStruct((m, D_MODEL), BF16)],
        scratch_shapes=[pltpu.VMEM((ROW_TILE, D_MODEL), BF16)],
        compiler_params=_params("parallel", "arbitrary"),
        name="ffn1",
    )(xp, xs, g1, wgu, wgu, wd, g2)


def _ffn2(x, g1, wgu, wd):
    m = x.shape[0]
    wspecs, nf = _ffn_weight_specs()
    vec = pl.BlockSpec((1, D_MODEL), lambda i, f: (0, 0))
    row = pl.BlockSpec((ROW_TILE, D_MODEL), lambda i, f: (i, 0))
    return pl.pallas_call(
        _ffn2_kernel,
        grid=(m // ROW_TILE, nf),
        in_specs=[row, vec, *wspecs],
        out_specs=row,
        out_shape=jax.ShapeDtypeStruct((m, D_MODEL), F32),
        scratch_shapes=[pltpu.VMEM((ROW_TILE, D_MODEL), BF16)],
        compiler_params=_params("parallel", "arbitrary"),
        name="ffn2",
    )(x, g1, wgu, wgu, wd)


def _glu_kernel(h_ref, wa_ref, wb_ref, o_ref):
    h = h_ref[...]
    o_ref[...] = _dot(h, wa_ref[...]) * jax.nn.sigmoid(_dot(h, wb_ref[...]))


def _glu(hn, w_in):
    m, tn = hn.shape[0], IN_COL_TILE
    nj = D_CONV // tn
    return pl.pallas_call(
        _glu_kernel,
        grid=(m // IN_TILE, nj),
        in_specs=[pl.BlockSpec((IN_TILE, D_MODEL), lambda i, j: (i, 0)),
                  pl.BlockSpec((D_MODEL, tn), lambda i, j: (0, j)),
                  pl.BlockSpec((D_MODEL, tn), lambda i, j: (0, j + nj))],
        out_specs=pl.BlockSpec((IN_TILE, tn), lambda i, j: (i, j)),
        out_shape=jax.ShapeDtypeStruct((m, D_CONV), F32),
        compiler_params=_params("parallel", "arbitrary"),
        name="glu",
    )(hn, w_in, w_in)


def _gates_kernel(h_ref, w_ref, o_ref):
    o_ref[...] = jax.nn.sigmoid(_dot(h_ref[...], w_ref[...])).astype(BF16)


def _gates(hn, w_in):
    m, tn = hn.shape[0], IN_COL_TILE
    col0 = (2 * D_CONV + Q_W + 2 * KV_W) // tn
    return pl.pallas_call(
        _gates_kernel,
        grid=(m // IN_TILE, 2 * D_MODEL // tn),
        in_specs=[pl.BlockSpec((IN_TILE, D_MODEL), lambda i, j: (i, 0)),
                  pl.BlockSpec((D_MODEL, tn), lambda i, j: (0, col0 + j))],
        out_specs=pl.BlockSpec((IN_TILE, tn), lambda i, j: (i, j)),
        out_shape=jax.ShapeDtypeStruct((m, 2 * D_MODEL), BF16),
        compiler_params=_params("parallel", "arbitrary"),
        name="gates",
    )(hn, w_in)


def _rotate(x, cos, sin_lo, sin_hi):
    cols = []
    for c in range(x.shape[1] // LANES):
        xb = x[:, c * LANES:(c + 1) * LANES]
        cols.append(xb * cos
                    + pltpu.roll(xb, LANES - ROT_DIM // 2, 1) * sin_lo
                    + pltpu.roll(xb, ROT_DIM // 2, 1) * sin_hi)
    return jnp.concatenate(cols, axis=1)


def _qkv_kernel(h_ref, wq_ref, wk_ref, wv_ref, cos_ref, slo_ref, shi_ref, q_ref, k_ref, v_ref):
    h = h_ref[...]
    cos, slo, shi = cos_ref[...], slo_ref[...], shi_ref[...]
    q = _rotate(_dot(h, wq_ref[...]), cos, slo, shi)
    q_ref[...] = (q * (HEAD_DIM ** -0.5)).astype(BF16)
    k_ref[...] = _rotate(_dot(h, wk_ref[...]), cos, slo, shi)
    v_ref[...] = _dot(h, wv_ref[...])


def _qkv(hn, w_in, cos, slo, shi):
    m = hn.shape[0]
    c0 = 2 * D_CONV
    tab = pl.BlockSpec((IN_TILE, LANES), lambda i: (i, 0))
    return pl.pallas_call(
        _qkv_kernel,
        grid=(m // IN_TILE,),
        in_specs=[pl.BlockSpec((IN_TILE, D_MODEL), lambda i: (i, 0)),
                  pl.BlockSpec((D_MODEL, Q_W), lambda i: (0, c0 // Q_W)),
                  pl.BlockSpec((D_MODEL, KV_W), lambda i: (0, (c0 + Q_W) // KV_W)),
                  pl.BlockSpec((D_MODEL, KV_W), lambda i: (0, (c0 + Q_W) // KV_W + 1)),
                  tab, tab, tab],
        out_specs=[pl.BlockSpec((IN_TILE, Q_W), lambda i: (i, 0)),
                   pl.BlockSpec((IN_TILE, KV_W), lambda i: (i, 0)),
                   pl.BlockSpec((IN_TILE, KV_W), lambda i: (i, 0))],
        out_shape=[jax.ShapeDtypeStruct((m, Q_W), BF16),
                   jax.ShapeDtypeStruct((m, KV_W), F32),
                   jax.ShapeDtypeStruct((m, KV_W), F32)],
        compiler_params=_params("parallel"),
        name="qkv",
    )(hn, w_in, w_in, w_in, cos, slo, shi)


def _rope_tables(batch, seq, dec_batch, dec_seq):
    inv = ROPE_THETA ** (-jnp.arange(0, ROT_DIM, 2, dtype=F32) / ROT_DIM)
    pos = jnp.concatenate([jnp.tile(jnp.arange(seq, dtype=jnp.int32), batch),
                           jnp.tile(PAST_LEN + jnp.arange(dec_seq, dtype=jnp.int32), dec_batch)])
    ang = pos.astype(F32)[:, None] * inv[None, :]
    cos, sin = jnp.cos(ang), jnp.sin(ang)
    one = jnp.ones((pos.shape[0], HEAD_DIM - ROT_DIM), F32)
    zero = jnp.zeros_like(one)
    zh = jnp.zeros_like(sin)
    c = jnp.concatenate([cos, cos, one], 1)
    lo = jnp.concatenate([-sin, zh, zero], 1)
    hi = jnp.concatenate([zh, sin, zero], 1)
    return [jnp.tile(t, (1, LANES // HEAD_DIM)) for t in (c, lo, hi)]


def _conv_tile(s_ref, c_ref, w_ref, b_ref, lg_ref, lb_ref, rows):
    ext = rows + CONV_HALO - SUBLANES
    for s in range(1, SUBLANES):
        c_ref[s - 1] = s_ref[s:s + ext, :]
    base = CONV_HALO - (CONV_WIDTH - 1)
    cols = []
    for cb in range(D_CONV // LANES):
        lanes = slice(cb * LANES, (cb + 1) * LANES)
        acc = jnp.broadcast_to(b_ref[:, lanes], (rows, LANES))
        for j in range(CONV_WIDTH):
            a, s = divmod(base + j, SUBLANES)
            r0 = a * SUBLANES
            tap = s_ref[r0:r0 + rows, lanes] if s == 0 else c_ref[s - 1, r0:r0 + rows, lanes]
            acc = acc + tap * w_ref[j:j + 1, lanes]
        cols.append(acc)
    acc = jnp.concatenate(cols, axis=1)
    mu = jnp.mean(acc, axis=-1, keepdims=True)
    xc = acc - mu
    y = xc * lax.rsqrt(jnp.mean(xc * xc, axis=-1, keepdims=True) + EPS) * lg_ref[...] + lb_ref[...]
    return (y * jax.nn.sigmoid(y)).astype(BF16)


def _conv_kernel(first_every, u_ref, past_ref, w_ref, b_ref, lg_ref, lb_ref, o_ref, s_ref, c_ref):
    rows = u_ref.shape[0]
    if first_every is None:
        s_ref[CONV_HALO - (CONV_WIDTH - 1):CONV_HALO, :] = past_ref[0]
    else:
        first = pl.program_id(0) % first_every == 0
        s_ref[0:CONV_HALO, :] = jnp.where(first, 0.0, past_ref[...])
    s_ref[CONV_HALO:, :] = u_ref[...]
    o_ref[...] = _conv_tile(s_ref, c_ref, w_ref, b_ref, lg_ref, lb_ref, rows)


def _conv_common_specs():
    return [pl.BlockSpec((CONV_WIDTH, D_CONV), lambda i: (0, 0)),
            pl.BlockSpec((1, D_CONV), lambda i: (0, 0)),
            pl.BlockSpec((1, D_CONV), lambda i: (0, 0)),
            pl.BlockSpec((1, D_CONV), lambda i: (0, 0))]


def _conv_scratch(rows):
    return [pltpu.VMEM((CONV_HALO + rows, D_CONV), F32),
            pltpu.VMEM((SUBLANES - 1, CONV_HALO + rows - SUBLANES, D_CONV), F32)]


def _conv_prompt(u, n_rows, seq, w, b, lg, lb):
    rows = CONV_ROWS
    per_seq = seq // rows
    halo_per_block = rows // CONV_HALO
    return pl.pallas_call(
        functools.partial(_conv_kernel, per_seq),
        grid=(n_rows // rows,),
        in_specs=[pl.BlockSpec((rows, D_CONV), lambda i: (i, 0)),
                  pl.BlockSpec((CONV_HALO, D_CONV), lambda i: (jnp.maximum(i * halo_per_block - 1, 0), 0)),
                  *_conv_common_specs()],
        out_specs=pl.BlockSpec((rows, D_CONV), lambda i: (i, 0)),
        out_shape=jax.ShapeDtypeStruct((n_rows, D_CONV), BF16),
        scratch_shapes=_conv_scratch(rows),
        compiler_params=_params("parallel"),
        name="conv_prompt",
    )(u, u, w, b, lg, lb)


def _conv_sample(u, row0, state, w, b, lg, lb):
    n_seq, hist, _ = state.shape
    assert hist == CONV_WIDTH - 1
    rows = CHUNK
    blk0 = row0 // rows
    return pl.pallas_call(
        functools.partial(_conv_kernel, None),
        grid=(n_seq,),
        in_specs=[pl.BlockSpec((rows, D_CONV), lambda i: (blk0 + i, 0)),
                  pl.BlockSpec((1, hist, D_CONV), lambda i: (i, 0, 0)),
                  *_conv_common_specs()],
        out_specs=pl.BlockSpec((rows, D_CONV), lambda i: (i, 0)),
        out_shape=jax.ShapeDtypeStruct((n_seq * rows, D_CONV), BF16),
        scratch_shapes=_conv_scratch(rows),
        compiler_params=_params("parallel"),
        name="conv_sample",
    )(u, state, w, b, lg, lb)


def _attn_tile(q, k, v, sink_ref, valid):
    r = q.shape[0]
    lane_head = lax.broadcasted_iota(jnp.int32, (1, KV_W), 1) // HEAD_DIM
    sel_row = lax.broadcasted_iota(jnp.int32, (KV_W, KV_W), 0)
    sel_col = lax.broadcasted_iota(jnp.int32, (KV_W, KV_W), 1) % HEAD_DIM
    head_mask = [jnp.where(lane_head == g, 1.0, 0.0).astype(BF16) for g in range(GROUP)]
    outs = []
    for kv in range(N_KV):
        sel = jnp.where(sel_row == sel_col + kv * HEAD_DIM, 1.0, 0.0).astype(BF16)
        krep = _dot(k, sel).astype(BF16)
        vrep = _dot(v, sel).astype(BF16)
        qh = q[:, kv * KV_W:(kv + 1) * KV_W]
        lhs = jnp.concatenate([qh * head_mask[g] for g in range(GROUP)], axis=0)
        s = lax.dot_general(lhs, krep, (((1,), (1,)), ((), ())), preferred_element_type=F32)
        ps = []
        for g in range(GROUP):
            sg = s[g * r:(g + 1) * r]
            if valid is not None:
                sg = jnp.where(valid, sg, NEG)
            sink = sink_ref[kv * GROUP + g]
            mx = jnp.maximum(jnp.max(sg, axis=-1, keepdims=True), sink)
            e = jnp.exp(sg - mx)
            den = jnp.sum(e, axis=-1, keepdims=True) + jnp.exp(sink - mx)
            ps.append((e / den).astype(BF16))
        o4 = _dot(jnp.concatenate(ps, axis=0), vrep)
        oh = o4[(GROUP - 1) * r:]
        for g in range(GROUP - 2, -1, -1):
            oh = jnp.where(lane_head == g, o4[g * r:(g + 1) * r], oh)
        outs.append(oh)
    return jnp.concatenate(outs, axis=1).astype(BF16)


def _attn_kernel(blocks_per_seq, sink_ref, q_ref, kp_ref, kc_ref, vp_ref, vc_ref, o_ref):
    qb = q_ref.shape[0]
    prefix = kp_ref.shape[-2]
    k = jnp.concatenate([kp_ref[...].reshape(prefix, KV_W), kc_ref[...]], axis=0).astype(BF16)
    v = jnp.concatenate([vp_ref[...].reshape(prefix, KV_W), vc_ref[...]], axis=0).astype(BF16)
    nk = prefix + qb
    if blocks_per_seq is None:
        assert qb == CHUNK and prefix == WINDOW
        valid = None
    else:
        col = lax.broadcasted_iota(jnp.int32, (qb, nk), 1)
        d = col // CHUNK - lax.broadcasted_iota(jnp.int32, (qb, nk), 0) // CHUNK
        first = pl.program_id(0) % blocks_per_seq == 0
        valid = (d >= 0) & (d <= WINDOW // CHUNK) & ((col >= prefix) | jnp.logical_not(first))
    o_ref[...] = _attn_tile(q_ref[...], k, v, sink_ref, valid)


def _attn_prompt(q, k, v, sinks, n_rows, seq):
    qb = ATTN_ROWS
    per_seq = seq // qb
    prev = pl.BlockSpec((qb, KV_W), lambda i: (jnp.maximum(i - 1, 0), 0))
    cur = pl.BlockSpec((qb, KV_W), lambda i: (i, 0))
    return pl.pallas_call(
        functools.partial(_attn_kernel, per_seq),
        grid=(n_rows // qb,),
        in_specs=[pl.BlockSpec(memory_space=pltpu.SMEM),
                  pl.BlockSpec((qb, Q_W), lambda i: (i, 0)), prev, cur, prev, cur],
        out_specs=pl.BlockSpec((qb, Q_W), lambda i: (i, 0)),
        out_shape=jax.ShapeDtypeStruct((n_rows, Q_W), BF16),
        compiler_params=_params("parallel"),
        name="attn_prompt",
    )(sinks, q, k, k, v, v)


def _attn_sample(q, k, v, sinks, row0, cache_k, cache_v):
    n_seq, w_rows, _ = cache_k.shape
    assert w_rows == WINDOW
    blk0 = row0 // CHUNK
    cache = pl.BlockSpec((1, w_rows, KV_W), lambda i: (i, 0, 0))
    cur = pl.BlockSpec((CHUNK, KV_W), lambda i: (blk0 + i, 0))
    return pl.pallas_call(
        functools.partial(_attn_kernel, None),
        grid=(n_seq,),
        in_specs=[pl.BlockSpec(memory_space=pltpu.SMEM),
                  pl.BlockSpec((CHUNK, Q_W), lambda i: (blk0 + i, 0)), cache, cur, cache, cur],
        out_specs=pl.BlockSpec((CHUNK, Q_W), lambda i: (i, 0)),
        out_shape=jax.ShapeDtypeStruct((n_seq * CHUNK, Q_W), BF16),
        compiler_params=_params("parallel"),
        name="attn_sample",
    )(sinks, q, cache_k, k, cache_v, v)


def _post_kernel(yc_ref, o_ref, gc_ref, ga_ref, x_ref, wc_ref, wa_ref, wo_ref, out_ref):
    m = (gc_ref[...].astype(F32) * _dot(yc_ref[...], wc_ref[...])
         + ga_ref[...].astype(F32) * _dot(o_ref[...], wa_ref[...]))
    out_ref[...] = x_ref[...] + _dot(m.astype(BF16), wo_ref[...])


def _post(yc, o, gates, x, wc, wa, wo):
    m, tm = x.shape[0], POST_TILE
    return pl.pallas_call(
        _post_kernel,
        grid=(m // tm,),
        in_specs=[pl.BlockSpec((tm, D_CONV), lambda i: (i, 0)),
                  pl.BlockSpec((tm, Q_W), lambda i: (i, 0)),
                  pl.BlockSpec((tm, D_MODEL), lambda i: (i, 0)),
                  pl.BlockSpec((tm, D_MODEL), lambda i: (i, 1)),
                  pl.BlockSpec((tm, D_MODEL), lambda i: (i, 0)),
                  _resident((D_CONV, D_MODEL)), _resident((Q_W, D_MODEL)), _resident((D_MODEL, D_MODEL))],
        out_specs=pl.BlockSpec((tm, D_MODEL), lambda i: (i, 0)),
        out_shape=jax.ShapeDtypeStruct((m, D_MODEL), F32),
        compiler_params=_params("parallel"),
        name="post",
    )(yc, o, gates, gates, x, wc, wa, wo)


def _ple_kernel(n_prompt_tiles, x_ref, pp_ref, ps_ref, gn_ref, wg_ref, wp_ref, fn_ref, yp_ref, ys_ref):
    i = pl.program_id(0)
    x = x_ref[...]
    gate = jax.nn.sigmoid(_dot(_rms(x, gn_ref[...]).astype(BF16), wg_ref[...]))
    pe = jnp.where(i < n_prompt_tiles, pp_ref[...], ps_ref[...]).astype(BF16)
    y = _rms(x + gate * _dot(pe, wp_ref[...]), fn_ref[...])

    @pl.when(i < n_prompt_tiles)
    def _():
        yp_ref[...] = y

    @pl.when(i >= n_prompt_tiles)
    def _():
        ys_ref[...] = y


def _ple(x, pp, ps, gn, wg, wp, fn):
    tm = POST_TILE
    n_p, n_s = pp.shape[0] // tm, ps.shape[0] // tm
    vec = pl.BlockSpec((1, D_MODEL), lambda i: (0, 0))

    def p_idx(i):
        return (jnp.minimum(i, n_p - 1), 0)

    def s_idx(i):
        return (jnp.maximum(i - n_p, 0), 0)

    return pl.pallas_call(
        functools.partial(_ple_kernel, n_p),
        grid=(n_p + n_s,),
        in_specs=[pl.BlockSpec((tm, D_MODEL), lambda i: (i, 0)),
                  pl.BlockSpec((tm, D_PLE), p_idx),
                  pl.BlockSpec((tm, D_PLE), s_idx),
                  vec, _resident((D_MODEL, D_MODEL)), _resident((D_PLE, D_MODEL)), vec],
        out_specs=[pl.BlockSpec((tm, D_MODEL), p_idx), pl.BlockSpec((tm, D_MODEL), s_idx)],
        out_shape=[jax.ShapeDtypeStruct((pp.shape[0], D_MODEL), F32),
                   jax.ShapeDtypeStruct((ps.shape[0], D_MODEL), F32)],
        compiler_params=_params("arbitrary"),
        name="ple_final",
    )(x, pp, ps, gn, wg, wp, fn)


def kernel(x_prompt, x_sample, p_prompt, p_sample, state_conv, cache_k, cache_v, ffn1_norm, ffn1_w_gu, ffn1_w_down, mix_norm, w_in, conv_w, conv_b, conv_ln_g, conv_ln_b, conv_w_out, attn_sinks, attn_w_out, w_out, ffn2_norm, ffn2_w_gu, ffn2_w_down, ple_norm, ple_w_gate, ple_w_proj, final_norm):
    assert x_prompt.shape[-1] == D_MODEL and ffn1_norm.shape[0] == 1 and w_in.shape[-1] == IN_COLS
    batch, seq, _ = x_prompt.shape
    dec_batch, dec_seq, _ = x_sample.shape
    assert dec_seq == CHUNK and seq % ROW_TILE == 0
    n_p, n_s = batch * seq, dec_batch * dec_seq

    def vec(a):
        return a.reshape(1, -1)

    wgu1, wd1 = ffn1_w_gu[0].astype(BF16), ffn1_w_down[0].astype(BF16)
    wgu2, wd2 = ffn2_w_gu[0].astype(BF16), ffn2_w_down[0].astype(BF16)
    w_in_b = w_in[0].astype(BF16)
    w_co = conv_w_out[0].astype(BF16)
    w_ao = attn_w_out[0].astype(BF16)
    w_o = w_out[0].astype(BF16)
    w_pg = ple_w_gate[0].astype(BF16)
    w_pp = ple_w_proj[0].astype(BF16)

    xp = x_prompt.reshape(n_p, D_MODEL)
    xs = x_sample.reshape(n_s, D_MODEL)

    x1, hn = _ffn1(xp, xs, vec(ffn1_norm), wgu1, wd1, vec(mix_norm))

    u = _glu(hn, w_in_b)
    cos, slo, shi = _rope_tables(batch, seq, dec_batch, dec_seq)
    q, k, v = _qkv(hn, w_in_b, cos, slo, shi)
    gates = _gates(hn, w_in_b)

    cw, cb, lg, lb = conv_w[0], vec(conv_b), vec(conv_ln_g), vec(conv_ln_b)
    yc = jnp.concatenate([_conv_prompt(u, n_p, seq, cw, cb, lg, lb),
                          _conv_sample(u, n_p, state_conv[0], cw, cb, lg, lb)], axis=0)

    sinks = attn_sinks[0]
    ck = cache_k[0].reshape(dec_batch, -1, KV_W)
    cv = cache_v[0].reshape(dec_batch, -1, KV_W)
    o = jnp.concatenate([_attn_prompt(q, k, v, sinks, n_p, seq),
                         _attn_sample(q, k, v, sinks, n_p, ck, cv)], axis=0)

    x2 = _post(yc, o, gates, x1, w_co, w_ao, w_o)
    x3 = _ffn2(x2, vec(ffn2_norm), wgu2, wd2)

    yp, ys = _ple(x3, p_prompt[0].reshape(n_p, D_PLE), p_sample[0].reshape(n_s, D_PLE),
                  vec(ple_norm), w_pg, w_pp, vec(final_norm))

    w_rows = ck.shape[1]
    k_p = k[:n_p].reshape(batch, seq, N_KV, HEAD_DIM)[:, -WINDOW:]
    v_p = v[:n_p].reshape(batch, seq, N_KV, HEAD_DIM)[:, -WINDOW:]
    c_p = u[:n_p].reshape(batch, seq, D_CONV)[:, -(CONV_WIDTH - 1):]
    k_s = jnp.concatenate([cache_k[0], k[n_p:].reshape(dec_batch, dec_seq, N_KV, HEAD_DIM)], 1)[:, -w_rows:]
    v_s = jnp.concatenate([cache_v[0], v[n_p:].reshape(dec_batch, dec_seq, N_KV, HEAD_DIM)], 1)[:, -w_rows:]
    c_s = jnp.concatenate([state_conv[0], u[n_p:].reshape(dec_batch, dec_seq, D_CONV)], 1)[:, -(CONV_WIDTH - 1):]
    return (yp.reshape(batch, seq, D_MODEL), ys.reshape(dec_batch, dec_seq, D_MODEL),
            k_p[None], v_p[None], c_p[None], k_s[None], v_s[None], c_s[None])
```

```python
import functools

import jax
import jax.numpy as jnp
from jax import lax
from jax.experimental import pallas as pl
from jax.experimental.pallas import tpu as pltpu

D_MODEL = 2048
D_PLE = 256
D_FF = 5504
D_CONV = 1024
CONV_WIDTH = 31
HEAD_DIM = 64
N_HEADS = 16
N_KV = 4
GROUP = N_HEADS // N_KV
ROT_DIM = 16
ROPE_THETA = 500000.0
CHUNK = 64
WINDOW = 128
PAST_LEN = 1024
EPS = 1e-6
NEG = -1e30
Q_W = N_HEADS * HEAD_DIM
KV_W = N_KV * HEAD_DIM
IN_COLS = 2 * D_CONV + Q_W + 2 * KV_W + 2 * D_MODEL

LANES = 128
SUBLANES = 8
FF_TILE = 512
FF_STEPS = -(-D_FF // FF_TILE)
FF_OVERLAP = FF_STEPS * FF_TILE - D_FF
ROW_TILE = 512
IN_TILE = 1088
IN_COL_TILE = 512
POST_TILE = 256
CONV_HALO = 32
MIX_ROWS = 256
MIX_TILE = WINDOW
VMEM_LIMIT = 56 * 1024 * 1024

F32 = jnp.float32
BF16 = jnp.bfloat16


def _params(*sem):
    return pltpu.CompilerParams(dimension_semantics=sem, vmem_limit_bytes=VMEM_LIMIT)


def _rms(x, g):
    return x * lax.rsqrt(jnp.mean(x * x, axis=-1, keepdims=True) + EPS) * g


def _dot(a, b):
    return jnp.dot(a, b, preferred_element_type=F32)


def _resident(shape):
    return pl.BlockSpec(shape, lambda *_: (0,) * len(shape), pipeline_mode=pl.Buffered(1))


def _ffn_body(load_x, g1_ref, wg_ref, wu_ref, wd_ref, g2_ref, xo_ref, hn_ref, xn_ref):
    f = pl.program_id(1)

    @pl.when(f == 0)
    def _():
        x = load_x()
        xn_ref[...] = _rms(x, g1_ref[...]).astype(BF16)
        xo_ref[...] = x

    xn = xn_ref[...]
    g = _dot(xn, wg_ref[...])
    u = _dot(xn, wu_ref[...])
    h = (g * jax.nn.sigmoid(g)) * (u * 0.5)
    col = lax.broadcasted_iota(jnp.int32, (1, FF_TILE), 1)
    h = jnp.where((col >= FF_OVERLAP) | (f < FF_STEPS - 1), h, 0.0)
    xo_ref[...] += _dot(h.astype(BF16), wd_ref[...])

    if hn_ref is not None:
        @pl.when(f == pl.num_programs(1) - 1)
        def _():
            hn_ref[...] = _rms(xo_ref[...], g2_ref[...]).astype(BF16)


def _ffn1_kernel(n_prompt_tiles, xp_ref, xs_ref, g1_ref, wg_ref, wu_ref, wd_ref, g2_ref,
                 xo_ref, hn_ref, xn_ref):
    i = pl.program_id(0)

    def load_x():
        return jnp.where(i < n_prompt_tiles, xp_ref[...], xs_ref[...])

    _ffn_body(load_x, g1_ref, wg_ref, wu_ref, wd_ref, g2_ref, xo_ref, hn_ref, xn_ref)


def _ffn2_kernel(x_ref, g1_ref, wg_ref, wu_ref, wd_ref, xo_ref, xn_ref):
    _ffn_body(lambda: x_ref[...], g1_ref, wg_ref, wu_ref, wd_ref, None, xo_ref, None, xn_ref)


def _ffn_weight_specs():
    def start(f):
        return pl.multiple_of(jnp.minimum(f * FF_TILE, D_FF - FF_TILE), LANES)

    cols = (pl.Element(D_MODEL), pl.Element(FF_TILE))
    rows = (pl.Element(FF_TILE), pl.Element(D_MODEL))
    return [
        pl.BlockSpec(cols, lambda i, f: (0, start(f))),
        pl.BlockSpec(cols, lambda i, f: (0, pl.multiple_of(D_FF + start(f), LANES))),
        pl.BlockSpec(rows, lambda i, f: (start(f), 0)),
    ], FF_STEPS


def _ffn1(xp, xs, g1, wgu, wd, g2):
    n_p, n_s = xp.shape[0] // ROW_TILE, xs.shape[0] // ROW_TILE
    assert n_s == 1 and xs.shape[0] == ROW_TILE
    m = xp.shape[0] + xs.shape[0]
    wspecs, nf = _ffn_weight_specs()
    vec = pl.BlockSpec((1, D_MODEL), lambda i, f: (0, 0))
    row = pl.BlockSpec((ROW_TILE, D_MODEL), lambda i, f: (i, 0))
    return pl.pallas_call(
        functools.partial(_ffn1_kernel, n_p),
        grid=(n_p + n_s, nf),
        in_specs=[pl.BlockSpec((ROW_TILE, D_MODEL), lambda i, f: (jnp.minimum(i, n_p - 1), 0)),
                  pl.BlockSpec((ROW_TILE, D_MODEL), lambda i, f: (0, 0)),
                  vec, *wspecs, vec],
        out_specs=[row, row],
        out_shape=[jax.ShapeDtypeStruct((m, D_MODEL), F32), jax.ShapeDtypeStruct((m, D_MODEL), BF16)],
        scratch_shapes=[pltpu.VMEM((ROW_TILE, D_MODEL), BF16)],
        compiler_params=_params("parallel", "arbitrary"),
        name="ffn1",
    )(xp, xs, g1, wgu, wgu, wd, g2)


def _ffn2(x, g1, wgu, wd):
    m = x.shape[0]
    wspecs, nf = _ffn_weight_specs()
    vec = pl.BlockSpec((1, D_MODEL), lambda i, f: (0, 0))
    row = pl.BlockSpec((ROW_TILE, D_MODEL), lambda i, f: (i, 0))
    return pl.pallas_call(
        _ffn2_kernel,
        grid=(m // ROW_TILE, nf),
        in_specs=[row, vec, *wspecs],
        out_specs=row,
        out_shape=jax.ShapeDtypeStruct((m, D_MODEL), F32),
        scratch_shapes=[pltpu.VMEM((ROW_TILE, D_MODEL), BF16)],
        compiler_params=_params("parallel", "arbitrary"),
        name="ffn2",
    )(x, g1, wgu, wgu, wd)


def _glu_kernel(h_ref, wa_ref, wb_ref, o_ref):
    h = h_ref[...]
    o_ref[...] = _dot(h, wa_ref[...]) * jax.nn.sigmoid(_dot(h, wb_ref[...]))


def _glu(hn, w_in):
    m, tn = hn.shape[0], IN_COL_TILE
    nj = D_CONV // tn
    return pl.pallas_call(
        _glu_kernel,
        grid=(m // IN_TILE, nj),
        in_specs=[pl.BlockSpec((IN_TILE, D_MODEL), lambda i, j: (i, 0)),
                  pl.BlockSpec((D_MODEL, tn), lambda i, j: (0, j)),
                  pl.BlockSpec((D_MODEL, tn), lambda i, j: (0, j + nj))],
        out_specs=pl.BlockSpec((IN_TILE, tn), lambda i, j: (i, j)),
        out_shape=jax.ShapeDtypeStruct((m, D_CONV), F32),
        compiler_params=_params("parallel", "arbitrary"),
        name="glu",
    )(hn, w_in, w_in)


def _gates_kernel(h_ref, w_ref, o_ref):
    o_ref[...] = jax.nn.sigmoid(_dot(h_ref[...], w_ref[...])).astype(BF16)


def _gates(hn, w_in):
    m, tn = hn.shape[0], IN_COL_TILE
    col0 = (2 * D_CONV + Q_W + 2 * KV_W) // tn
    return pl.pallas_call(
        _gates_kernel,
        grid=(m // IN_TILE, 2 * D_MODEL // tn),
        in_specs=[pl.BlockSpec((IN_TILE, D_MODEL), lambda i, j: (i, 0)),
                  pl.BlockSpec((D_MODEL, tn), lambda i, j: (0, col0 + j))],
        out_specs=pl.BlockSpec((IN_TILE, tn), lambda i, j: (i, j)),
        out_shape=jax.ShapeDtypeStruct((m, 2 * D_MODEL), BF16),
        compiler_params=_params("parallel", "arbitrary"),
        name="gates",
    )(hn, w_in)


def _rotate(x, cos, sin_lo, sin_hi):
    cols = []
    for c in range(x.shape[1] // LANES):
        xb = x[:, c * LANES:(c + 1) * LANES]
        cols.append(xb * cos
                    + pltpu.roll(xb, LANES - ROT_DIM // 2, 1) * sin_lo
                    + pltpu.roll(xb, ROT_DIM // 2, 1) * sin_hi)
    return jnp.concatenate(cols, axis=1)


def _qkv_kernel(h_ref, wq_ref, wk_ref, wv_ref, cos_ref, slo_ref, shi_ref, q_ref, k_ref, v_ref):
    h = h_ref[...]
    cos, slo, shi = cos_ref[...], slo_ref[...], shi_ref[...]
    q = _rotate(_dot(h, wq_ref[...]), cos, slo, shi)
    q_ref[...] = (q * (HEAD_DIM ** -0.5)).astype(BF16)
    k_ref[...] = _rotate(_dot(h, wk_ref[...]), cos, slo, shi)
    v_ref[...] = _dot(h, wv_ref[...])


def _qkv(hn, w_in, cos, slo, shi):
    m = hn.shape[0]
    c0 = 2 * D_CONV
    tab = pl.BlockSpec((IN_TILE, LANES), lambda i: (i, 0))
    return pl.pallas_call(
        _qkv_kernel,
        grid=(m // IN_TILE,),
        in_specs=[pl.BlockSpec((IN_TILE, D_MODEL), lambda i: (i, 0)),
                  pl.BlockSpec((D_MODEL, Q_W), lambda i: (0, c0 // Q_W)),
                  pl.BlockSpec((D_MODEL, KV_W), lambda i: (0, (c0 + Q_W) // KV_W)),
                  pl.BlockSpec((D_MODEL, KV_W), lambda i: (0, (c0 + Q_W) // KV_W + 1)),
                  tab, tab, tab],
        out_specs=[pl.BlockSpec((IN_TILE, Q_W), lambda i: (i, 0)),
                   pl.BlockSpec((IN_TILE, KV_W), lambda i: (i, 0)),
                   pl.BlockSpec((IN_TILE, KV_W), lambda i: (i, 0))],
        out_shape=[jax.ShapeDtypeStruct((m, Q_W), BF16),
                   jax.ShapeDtypeStruct((m, KV_W), F32),
                   jax.ShapeDtypeStruct((m, KV_W), F32)],
        compiler_params=_params("parallel"),
        name="qkv",
    )(hn, w_in, w_in, w_in, cos, slo, shi)


def _rope_tables(batch, seq, dec_batch, dec_seq):
    inv = ROPE_THETA ** (-jnp.arange(0, ROT_DIM, 2, dtype=F32) / ROT_DIM)
    pos = jnp.concatenate([jnp.tile(jnp.arange(seq, dtype=jnp.int32), batch),
                           jnp.tile(PAST_LEN + jnp.arange(dec_seq, dtype=jnp.int32), dec_batch)])
    ang = pos.astype(F32)[:, None] * inv[None, :]
    cos, sin = jnp.cos(ang), jnp.sin(ang)
    one = jnp.ones((pos.shape[0], HEAD_DIM - ROT_DIM), F32)
    zero = jnp.zeros_like(one)
    zh = jnp.zeros_like(sin)
    c = jnp.concatenate([cos, cos, one], 1)
    lo = jnp.concatenate([-sin, zh, zero], 1)
    hi = jnp.concatenate([zh, sin, zero], 1)
    return [jnp.tile(t, (1, LANES // HEAD_DIM)) for t in (c, lo, hi)]


def _conv_tile(s_ref, row0, c_ref, w_ref, b_ref, lg_ref, lb_ref, rows):
    ext = rows + CONV_HALO - SUBLANES
    for s in range(1, SUBLANES):
        c_ref[s - 1] = s_ref[row0 + s:row0 + s + ext, :]
    base = CONV_HALO - (CONV_WIDTH - 1)
    cols = []
    for cb in range(D_CONV // LANES):
        lanes = slice(cb * LANES, (cb + 1) * LANES)
        acc = jnp.broadcast_to(b_ref[:, lanes], (rows, LANES))
        for j in range(CONV_WIDTH):
            a, s = divmod(base + j, SUBLANES)
            r0 = a * SUBLANES
            if s == 0:
                tap = s_ref[row0 + r0:row0 + r0 + rows, lanes]
            else:
                tap = c_ref[s - 1, r0:r0 + rows, lanes]
            acc = acc + tap * w_ref[j:j + 1, lanes]
        cols.append(acc)
    acc = jnp.concatenate(cols, axis=1)
    mu = jnp.mean(acc, axis=-1, keepdims=True)
    xc = acc - mu
    y = xc * lax.rsqrt(jnp.mean(xc * xc, axis=-1, keepdims=True) + EPS) * lg_ref[...] + lb_ref[...]
    return (y * jax.nn.sigmoid(y)).astype(BF16)


def _conv_sample_kernel(u_ref, past_ref, w_ref, b_ref, lg_ref, lb_ref, o_ref, s_ref, c_ref):
    s_ref[CONV_HALO - (CONV_WIDTH - 1):CONV_HALO, :] = past_ref[0]
    s_ref[CONV_HALO:, :] = u_ref[...]
    o_ref[...] = _conv_tile(s_ref, 0, c_ref, w_ref, b_ref, lg_ref, lb_ref, u_ref.shape[0])


def _conv_common_specs():
    return [pl.BlockSpec((CONV_WIDTH, D_CONV), lambda i: (0, 0)),
            pl.BlockSpec((1, D_CONV), lambda i: (0, 0)),
            pl.BlockSpec((1, D_CONV), lambda i: (0, 0)),
            pl.BlockSpec((1, D_CONV), lambda i: (0, 0))]


def _conv_scratch(block_rows, tile_rows):
    return [pltpu.VMEM((CONV_HALO + block_rows, D_CONV), F32),
            pltpu.VMEM((SUBLANES - 1, CONV_HALO + tile_rows - SUBLANES, D_CONV), F32)]


def _conv_sample(u, row0, state, w, b, lg, lb):
    n_seq, hist, _ = state.shape
    assert hist == CONV_WIDTH - 1
    rows = CHUNK
    blk0 = row0 // rows
    return pl.pallas_call(
        _conv_sample_kernel,
        grid=(n_seq,),
        in_specs=[pl.BlockSpec((rows, D_CONV), lambda i: (blk0 + i, 0)),
                  pl.BlockSpec((1, hist, D_CONV), lambda i: (i, 0, 0)),
                  *_conv_common_specs()],
        out_specs=pl.BlockSpec((rows, D_CONV), lambda i: (i, 0)),
        out_shape=jax.ShapeDtypeStruct((n_seq * rows, D_CONV), BF16),
        scratch_shapes=_conv_scratch(rows, rows),
        compiler_params=_params("parallel"),
        name="conv_sample",
    )(u, state, w, b, lg, lb)


def _attn_tile(q, k, v, sink_ref, valid):
    r = q.shape[0]
    lane_head = lax.broadcasted_iota(jnp.int32, (1, KV_W), 1) // HEAD_DIM
    sel_row = lax.broadcasted_iota(jnp.int32, (KV_W, KV_W), 0)
    sel_col = lax.broadcasted_iota(jnp.int32, (KV_W, KV_W), 1) % HEAD_DIM
    head_mask = [jnp.where(lane_head == g, 1.0, 0.0).astype(BF16) for g in range(GROUP)]
    outs = []
    for kv in range(N_KV):
        sel = jnp.where(sel_row == sel_col + kv * HEAD_DIM, 1.0, 0.0).astype(BF16)
        krep = _dot(k, sel).astype(BF16)
        vrep = _dot(v, sel).astype(BF16)
        qh = q[:, kv * KV_W:(kv + 1) * KV_W]
        lhs = jnp.concatenate([qh * head_mask[g] for g in range(GROUP)], axis=0)
        s = lax.dot_general(lhs, krep, (((1,), (1,)), ((), ())), preferred_element_type=F32)
        ps = []
        for g in range(GROUP):
            sg = s[g * r:(g + 1) * r]
            if valid is not None:
                sg = jnp.where(valid, sg, NEG)
            sink = sink_ref[kv * GROUP + g]
            mx = jnp.maximum(jnp.max(sg, axis=-1, keepdims=True), sink)
            e = jnp.exp(sg - mx)
            den = jnp.sum(e, axis=-1, keepdims=True) + jnp.exp(sink - mx)
            ps.append((e / den).astype(BF16))
        o4 = _dot(jnp.concatenate(ps, axis=0), vrep)
        oh = o4[(GROUP - 1) * r:]
        for g in range(GROUP - 2, -1, -1):
            oh = jnp.where(lane_head == g, o4[g * r:(g + 1) * r], oh)
        outs.append(oh)
    return jnp.concatenate(outs, axis=1).astype(BF16)


def _attn_sample_kernel(sink_ref, q_ref, kp_ref, kc_ref, vp_ref, vc_ref, o_ref):
    k = jnp.concatenate([kp_ref[0], kc_ref[...]], axis=0).astype(BF16)
    v = jnp.concatenate([vp_ref[0], vc_ref[...]], axis=0).astype(BF16)
    o_ref[...] = _attn_tile(q_ref[...], k, v, sink_ref, None)


def _attn_sample(q, k, v, sinks, row0, cache_k, cache_v):
    n_seq, w_rows, _ = cache_k.shape
    assert w_rows == WINDOW
    blk0 = row0 // CHUNK
    cache = pl.BlockSpec((1, w_rows, KV_W), lambda i: (i, 0, 0))
    cur = pl.BlockSpec((CHUNK, KV_W), lambda i: (blk0 + i, 0))
    return pl.pallas_call(
        _attn_sample_kernel,
        grid=(n_seq,),
        in_specs=[pl.BlockSpec(memory_space=pltpu.SMEM),
                  pl.BlockSpec((CHUNK, Q_W), lambda i: (blk0 + i, 0)), cache, cur, cache, cur],
        out_specs=pl.BlockSpec((CHUNK, Q_W), lambda i: (i, 0)),
        out_shape=jax.ShapeDtypeStruct((n_seq * CHUNK, Q_W), BF16),
        compiler_params=_params("parallel"),
        name="attn_sample",
    )(sinks, q, cache_k, k, cache_v, v)


def _merge_out(yc, o, gc, ga, x, wc_ref, wa_ref, wo_ref):
    m = gc.astype(F32) * _dot(yc, wc_ref[...]) + ga.astype(F32) * _dot(o, wa_ref[...])
    return x + _dot(m.astype(BF16), wo_ref[...])


def _mix_kernel(per_seq, sink_ref, u_ref, uh_ref, q_ref, kp_ref, kc_ref, vp_ref, vc_ref,
                cw_ref, cb_ref, lg_ref, lb_ref, gc_ref, ga_ref, x_ref, wc_ref, wa_ref, wo_ref,
                out_ref, s_ref, c_ref):
    first = pl.program_id(0) % per_seq == 0
    s_ref[0:CONV_HALO, :] = jnp.where(first, 0.0, uh_ref[...])
    s_ref[CONV_HALO:, :] = u_ref[...]

    k = jnp.concatenate([kp_ref[...], kc_ref[...]], axis=0).astype(BF16)
    v = jnp.concatenate([vp_ref[...], vc_ref[...]], axis=0).astype(BF16)
    nk = WINDOW + MIX_TILE
    col = lax.broadcasted_iota(jnp.int32, (MIX_TILE, nk), 1)
    d = col // CHUNK - lax.broadcasted_iota(jnp.int32, (MIX_TILE, nk), 0) // CHUNK
    band = (d >= 0) & (d <= WINDOW // CHUNK)

    for r0 in range(0, MIX_ROWS, MIX_TILE):
        rows = slice(r0, r0 + MIX_TILE)
        yc = _conv_tile(s_ref, r0, c_ref, cw_ref, cb_ref, lg_ref, lb_ref, MIX_TILE)
        valid = band & ((col >= WINDOW) | jnp.logical_not(first)) if r0 == 0 else band
        o = _attn_tile(q_ref[rows, :], k[r0:r0 + nk], v[r0:r0 + nk], sink_ref, valid)
        out_ref[rows, :] = _merge_out(yc, o, gc_ref[rows, :], ga_ref[rows, :], x_ref[rows, :],
                                      wc_ref, wa_ref, wo_ref)


def _mix_prompt(u, q, k, v, gates, x, sinks, cw, cb, lg, lb, wc, wa, wo, n_rows, seq):
    rows = MIX_ROWS
    halo_per_block = rows // CONV_HALO
    pre_per_block = rows // WINDOW
    kv_prev = pl.BlockSpec((WINDOW, KV_W), lambda i: (jnp.maximum(i * pre_per_block - 1, 0), 0))
    kv_cur = pl.BlockSpec((rows, KV_W), lambda i: (i, 0))
    return pl.pallas_call(
        functools.partial(_mix_kernel, seq // rows),
        grid=(n_rows // rows,),
        in_specs=[pl.BlockSpec(memory_space=pltpu.SMEM),
                  pl.BlockSpec((rows, D_CONV), lambda i: (i, 0)),
                  pl.BlockSpec((CONV_HALO, D_CONV), lambda i: (jnp.maximum(i * halo_per_block - 1, 0), 0)),
                  pl.BlockSpec((rows, Q_W), lambda i: (i, 0)),
                  kv_prev, kv_cur, kv_prev, kv_cur,
                  *_conv_common_specs(),
                  pl.BlockSpec((rows, D_MODEL), lambda i: (i, 0)),
                  pl.BlockSpec((rows, D_MODEL), lambda i: (i, 1)),
                  pl.BlockSpec((rows, D_MODEL), lambda i: (i, 0)),
                  _resident((D_CONV, D_MODEL)), _resident((Q_W, D_MODEL)), _resident((D_MODEL, D_MODEL))],
        out_specs=pl.BlockSpec((rows, D_MODEL), lambda i: (i, 0)),
        out_shape=jax.ShapeDtypeStruct((x.shape[0], D_MODEL), F32),
        scratch_shapes=_conv_scratch(rows, MIX_TILE),
        compiler_params=_params("parallel"),
        name="mix_prompt",
    )(sinks, u, u, q, k, k, v, v, cw, cb, lg, lb, gates, gates, x, wc, wa, wo)


def _post_sample_kernel(yc_ref, o_ref, gc_ref, ga_ref, x_ref, wc_ref, wa_ref, wo_ref, _, out_ref):
    out_ref[...] = _merge_out(yc_ref[...], o_ref[...], gc_ref[...], ga_ref[...], x_ref[...],
                              wc_ref, wa_ref, wo_ref)


def _post_sample(yc, o, gates, x, wc, wa, wo, x_out, row0):
    tm = POST_TILE
    blk0 = row0 // tm
    return pl.pallas_call(
        _post_sample_kernel,
        grid=(yc.shape[0] // tm,),
        in_specs=[pl.BlockSpec((tm, D_CONV), lambda i: (i, 0)),
                  pl.BlockSpec((tm, Q_W), lambda i: (i, 0)),
                  pl.BlockSpec((tm, D_MODEL), lambda i: (blk0 + i, 0)),
                  pl.BlockSpec((tm, D_MODEL), lambda i: (blk0 + i, 1)),
                  pl.BlockSpec((tm, D_MODEL), lambda i: (blk0 + i, 0)),
                  _resident((D_CONV, D_MODEL)), _resident((Q_W, D_MODEL)), _resident((D_MODEL, D_MODEL)),
                  pl.BlockSpec(memory_space=pl.ANY)],
        out_specs=pl.BlockSpec((tm, D_MODEL), lambda i: (blk0 + i, 0)),
        out_shape=jax.ShapeDtypeStruct(x_out.shape, F32),
        input_output_aliases={8: 0},
        compiler_params=_params("parallel"),
        name="post_sample",
    )(yc, o, gates, gates, x, wc, wa, wo, x_out)


def _ple_kernel(n_prompt_tiles, x_ref, pp_ref, ps_ref, gn_ref, wg_ref, wp_ref, fn_ref, yp_ref, ys_ref):
    i = pl.program_id(0)
    x = x_ref[...]
    gate = jax.nn.sigmoid(_dot(_rms(x, gn_ref[...]).astype(BF16), wg_ref[...]))
    pe = jnp.where(i < n_prompt_tiles, pp_ref[...], ps_ref[...]).astype(BF16)
    y = _rms(x + gate * _dot(pe, wp_ref[...]), fn_ref[...])

    @pl.when(i < n_prompt_tiles)
    def _():
        yp_ref[...] = y

    @pl.when(i >= n_prompt_tiles)
    def _():
        ys_ref[...] = y


def _ple(x, pp, ps, gn, wg, wp, fn):
    tm = POST_TILE
    n_p, n_s = pp.shape[0] // tm, ps.shape[0] // tm
    vec = pl.BlockSpec((1, D_MODEL), lambda i: (0, 0))

    def p_idx(i):
        return (jnp.minimum(i, n_p - 1), 0)

    def s_idx(i):
        return (jnp.maximum(i - n_p, 0), 0)

    return pl.pallas_call(
        functools.partial(_ple_kernel, n_p),
        grid=(n_p + n_s,),
        in_specs=[pl.BlockSpec((tm, D_MODEL), lambda i: (i, 0)),
                  pl.BlockSpec((tm, D_PLE), p_idx),
                  pl.BlockSpec((tm, D_PLE), s_idx),
                  vec, _resident((D_MODEL, D_MODEL)), _resident((D_PLE, D_MODEL)), vec],
        out_specs=[pl.BlockSpec((tm, D_MODEL), p_idx), pl.BlockSpec((tm, D_MODEL), s_idx)],
        out_shape=[jax.ShapeDtypeStruct((pp.shape[0], D_MODEL), F32),
                   jax.ShapeDtypeStruct((ps.shape[0], D_MODEL), F32)],
        compiler_params=_params("arbitrary"),
        name="ple_final",
    )(x, pp, ps, gn, wg, wp, fn)


def kernel(x_prompt, x_sample, p_prompt, p_sample, state_conv, cache_k, cache_v, ffn1_norm, ffn1_w_gu, ffn1_w_down, mix_norm, w_in, conv_w, conv_b, conv_ln_g, conv_ln_b, conv_w_out, attn_sinks, attn_w_out, w_out, ffn2_norm, ffn2_w_gu, ffn2_w_down, ple_norm, ple_w_gate, ple_w_proj, final_norm):
    assert x_prompt.shape[-1] == D_MODEL and ffn1_norm.shape[0] == 1 and w_in.shape[-1] == IN_COLS
    batch, seq, _ = x_prompt.shape
    dec_batch, dec_seq, _ = x_sample.shape
    assert dec_seq == CHUNK and seq % ROW_TILE == 0
    n_p, n_s = batch * seq, dec_batch * dec_seq

    def vec(a):
        return a.reshape(1, -1)

    wgu1, wd1 = ffn1_w_gu[0].astype(BF16), ffn1_w_down[0].astype(BF16)
    wgu2, wd2 = ffn2_w_gu[0].astype(BF16), ffn2_w_down[0].astype(BF16)
    w_in_b = w_in[0].astype(BF16)
    w_co = conv_w_out[0].astype(BF16)
    w_ao = attn_w_out[0].astype(BF16)
    w_o = w_out[0].astype(BF16)
    w_pg = ple_w_gate[0].astype(BF16)
    w_pp = ple_w_proj[0].astype(BF16)

    xp = x_prompt.reshape(n_p, D_MODEL)
    xs = x_sample.reshape(n_s, D_MODEL)

    x1, hn = _ffn1(xp, xs, vec(ffn1_norm), wgu1, wd1, vec(mix_norm))

    u = _glu(hn, w_in_b)
    cos, slo, shi = _rope_tables(batch, seq, dec_batch, dec_seq)
    q, k, v = _qkv(hn, w_in_b, cos, slo, shi)
    gates = _gates(hn, w_in_b)

    cw, cb, lg, lb = conv_w[0], vec(conv_b), vec(conv_ln_g), vec(conv_ln_b)
    sinks = attn_sinks[0]
    ck = cache_k[0].reshape(dec_batch, -1, KV_W)
    cv = cache_v[0].reshape(dec_batch, -1, KV_W)
    yc_s = _conv_sample(u, n_p, state_conv[0], cw, cb, lg, lb)
    o_s = _attn_sample(q, k, v, sinks, n_p, ck, cv)
    x2 = _mix_prompt(u, q, k, v, gates, x1, sinks, cw, cb, lg, lb, w_co, w_ao, w_o, n_p, seq)
    x2 = _post_sample(yc_s, o_s, gates, x1, w_co, w_ao, w_o, x2, n_p)
    x3 = _ffn2(x2, vec(ffn2_norm), wgu2, wd2)

    yp, ys = _ple(x3, p_prompt[0].reshape(n_p, D_PLE), p_sample[0].reshape(n_s, D_PLE),
                  vec(ple_norm), w_pg, w_pp, vec(final_norm))

    w_rows = ck.shape[1]

    def prompt_tail(a, rows):
        return jnp.stack([a[(b + 1) * seq - rows:(b + 1) * seq] for b in range(batch)])

    k_p = prompt_tail(k, WINDOW).reshape(batch, WINDOW, N_KV, HEAD_DIM)
    v_p = prompt_tail(v, WINDOW).reshape(batch, WINDOW, N_KV, HEAD_DIM)
    c_p = prompt_tail(u, CONV_WIDTH - 1)
    k_s = jnp.concatenate([cache_k[0], k[n_p:].reshape(dec_batch, dec_seq, N_KV, HEAD_DIM)], 1)[:, -w_rows:]
    v_s = jnp.concatenate([cache_v[0], v[n_p:].reshape(dec_batch, dec_seq, N_KV, HEAD_DIM)], 1)[:, -w_rows:]
    c_s = jnp.concatenate([state_conv[0], u[n_p:].reshape(dec_batch, dec_seq, D_CONV)], 1)[:, -(CONV_WIDTH - 1):]
    return (yp.reshape(batch, seq, D_MODEL), ys.reshape(dec_batch, dec_seq, D_MODEL),
            k_p[None], v_p[None], c_p[None], k_s[None], v_s[None], c_s[None])
```

```python
import functools

import jax
import jax.numpy as jnp
from jax import lax
from jax.experimental import pallas as pl
from jax.experimental.pallas import tpu as pltpu

D_MODEL = 2048
D_PLE = 256
D_FF = 5504
D_CONV = 1024
CONV_WIDTH = 31
HEAD_DIM = 64
N_HEADS = 16
N_KV = 4
GROUP = N_HEADS // N_KV
ROT_DIM = 16
ROPE_THETA = 500000.0
CHUNK = 64
WINDOW = 128
PAST_LEN = 1024
EPS = 1e-6
NEG = -1e30
LOG2E = 1.4426950408889634
Q_W = N_HEADS * HEAD_DIM
KV_W = N_KV * HEAD_DIM
IN_COLS = 2 * D_CONV + Q_W + 2 * KV_W + 2 * D_MODEL

LANES = 128
SUBLANES = 8
FF1_TILE = 512
FF2_TILE = 1024
ROW_TILE = 512
IN_TILE = 1088
IN_COL_TILE = 1024
POST_TILE = 256
PLE_TILE = 512
CONV_HALO = 32
MIX_ROWS = 256
MIX_TILE = WINDOW
VMEM_LIMIT = 56 * 1024 * 1024

F32 = jnp.float32
BF16 = jnp.bfloat16


def _params(*sem):
    return pltpu.CompilerParams(dimension_semantics=sem, vmem_limit_bytes=VMEM_LIMIT)


def _rms(x, g):
    return x * lax.rsqrt(jnp.mean(x * x, axis=-1, keepdims=True) + EPS) * g


def _dot(a, b):
    return jnp.dot(a, b, preferred_element_type=F32)


def _resident(shape):
    return pl.BlockSpec(shape, lambda *_: (0,) * len(shape), pipeline_mode=pl.Buffered(1))


def _ffn_body(load_x, g1_ref, wg_ref, wu_ref, wd_ref, g2_ref, xo_ref, hn_ref, xn_ref):
    f = pl.program_id(1)

    @pl.when(f == 0)
    def _():
        x = load_x()
        xn_ref[...] = _rms(x, g1_ref[...]).astype(BF16)
        xo_ref[...] = x

    tile = wg_ref.shape[1]
    steps = pl.num_programs(1)

    def accumulate(cols):
        xn = xn_ref[...]
        g = _dot(xn, wg_ref[:, cols])
        u = _dot(xn, wu_ref[:, cols])
        h = (g * jax.nn.sigmoid(g)) * (u * 0.5)
        xo_ref[...] += _dot(h.astype(BF16), wd_ref[cols, :])

    pl.when(f < steps - 1)(lambda: accumulate(slice(0, tile)))
    pl.when(f == steps - 1)(lambda: accumulate(slice(_ff_steps(tile) * tile - D_FF, tile)))

    if hn_ref is not None:
        @pl.when(f == pl.num_programs(1) - 1)
        def _():
            hn_ref[...] = _rms(xo_ref[...], g2_ref[...]).astype(BF16)


def _ffn1_kernel(n_prompt_tiles, xp_ref, xs_ref, g1_ref, wg_ref, wu_ref, wd_ref, g2_ref,
                 xo_ref, hn_ref, xn_ref):
    i = pl.program_id(0)

    def load_x():
        return jnp.where(i < n_prompt_tiles, xp_ref[...], xs_ref[...])

    _ffn_body(load_x, g1_ref, wg_ref, wu_ref, wd_ref, g2_ref, xo_ref, hn_ref, xn_ref)


def _ffn2_kernel(x_ref, g1_ref, wg_ref, wu_ref, wd_ref, xo_ref, xn_ref):
    _ffn_body(lambda: x_ref[...], g1_ref, wg_ref, wu_ref, wd_ref, None, xo_ref, None, xn_ref)


def _ff_steps(tile):
    return -(-D_FF // tile)


def _ffn_weight_specs(tile):
    def start(f):
        return pl.multiple_of(jnp.minimum(f * tile, D_FF - tile), LANES)

    cols = (pl.Element(D_MODEL), pl.Element(tile))
    rows = (pl.Element(tile), pl.Element(D_MODEL))
    return [
        pl.BlockSpec(cols, lambda i, f: (0, start(f))),
        pl.BlockSpec(cols, lambda i, f: (0, pl.multiple_of(D_FF + start(f), LANES))),
        pl.BlockSpec(rows, lambda i, f: (start(f), 0)),
    ], _ff_steps(tile)


def _ffn1(xp, xs, g1, wgu, wd, g2):
    n_p, n_s = xp.shape[0] // ROW_TILE, xs.shape[0] // ROW_TILE
    assert n_s == 1 and xs.shape[0] == ROW_TILE
    m = xp.shape[0] + xs.shape[0]
    wspecs, nf = _ffn_weight_specs(FF1_TILE)
    vec = pl.BlockSpec((1, D_MODEL), lambda i, f: (0, 0))
    row = pl.BlockSpec((ROW_TILE, D_MODEL), lambda i, f: (i, 0))
    return pl.pallas_call(
        functools.partial(_ffn1_kernel, n_p),
        grid=(n_p + n_s, nf),
        in_specs=[pl.BlockSpec((ROW_TILE, D_MODEL), lambda i, f: (jnp.minimum(i, n_p - 1), 0)),
                  pl.BlockSpec((ROW_TILE, D_MODEL), lambda i, f: (0, 0)),
                  vec, *wspecs, vec],
        out_specs=[row, row],
        out_shape=[jax.ShapeDtypeStruct((m, D_MODEL), F32), jax.ShapeDtypeStruct((m, D_MODEL), BF16)],
        scratch_shapes=[pltpu.VMEM((ROW_TILE, D_MODEL), BF16)],
        compiler_params=_params("parallel", "arbitrary"),
        name="ffn1",
    )(xp, xs, g1, wgu, wgu, wd, g2)


def _ffn2(x, g1, wgu, wd):
    m = x.shape[0]
    wspecs, nf = _ffn_weight_specs(FF2_TILE)
    vec = pl.BlockSpec((1, D_MODEL), lambda i, f: (0, 0))
    row = pl.BlockSpec((ROW_TILE, D_MODEL), lambda i, f: (i, 0))
    return pl.pallas_call(
        _ffn2_kernel,
        grid=(m // ROW_TILE, nf),
        in_specs=[row, vec, *wspecs],
        out_specs=row,
        out_shape=jax.ShapeDtypeStruct((m, D_MODEL), F32),
        scratch_shapes=[pltpu.VMEM((ROW_TILE, D_MODEL), BF16)],
        compiler_params=_params("parallel", "arbitrary"),
        name="ffn2",
    )(x, g1, wgu, wgu, wd)


def _glu_kernel(h_ref, wa_ref, wb_ref, o_ref):
    h = h_ref[...]
    o_ref[...] = _dot(h, wa_ref[...]) * jax.nn.sigmoid(_dot(h, wb_ref[...]))


def _glu(hn, w_in):
    m, tn = hn.shape[0], IN_COL_TILE
    nj = D_CONV // tn
    return pl.pallas_call(
        _glu_kernel,
        grid=(m // IN_TILE, nj),
        in_specs=[pl.BlockSpec((IN_TILE, D_MODEL), lambda i, j: (i, 0)),
                  pl.BlockSpec((D_MODEL, tn), lambda i, j: (0, j)),
                  pl.BlockSpec((D_MODEL, tn), lambda i, j: (0, j + nj))],
        out_specs=pl.BlockSpec((IN_TILE, tn), lambda i, j: (i, j)),
        out_shape=jax.ShapeDtypeStruct((m, D_CONV), F32),
        compiler_params=_params("parallel", "arbitrary"),
        name="glu",
    )(hn, w_in, w_in)


def _gates_kernel(h_ref, w_ref, o_ref):
    o_ref[...] = jax.nn.sigmoid(_dot(h_ref[...], w_ref[...])).astype(BF16)


def _gates(hn, w_in):
    m, tn = hn.shape[0], IN_COL_TILE
    col0 = 2 * D_CONV + Q_W + 2 * KV_W
    return pl.pallas_call(
        _gates_kernel,
        grid=(m // IN_TILE, 2 * D_MODEL // tn),
        in_specs=[pl.BlockSpec((IN_TILE, D_MODEL), lambda i, j: (i, 0)),
                  pl.BlockSpec((pl.Element(D_MODEL), pl.Element(tn)),
                               lambda i, j: (0, pl.multiple_of(col0 + j * tn, LANES)))],
        out_specs=pl.BlockSpec((IN_TILE, tn), lambda i, j: (i, j)),
        out_shape=jax.ShapeDtypeStruct((m, 2 * D_MODEL), BF16),
        compiler_params=_params("parallel", "arbitrary"),
        name="gates",
    )(hn, w_in)


def _rotate(x, cos, sin_lo, sin_hi):
    cols = []
    for c in range(x.shape[1] // LANES):
        xb = x[:, c * LANES:(c + 1) * LANES]
        cols.append(xb * cos
                    + pltpu.roll(xb, LANES - ROT_DIM // 2, 1) * sin_lo
                    + pltpu.roll(xb, ROT_DIM // 2, 1) * sin_hi)
    return jnp.concatenate(cols, axis=1)


def _qkv_kernel(h_ref, wq_ref, wk_ref, wv_ref, cos_ref, slo_ref, shi_ref, q_ref, k_ref, v_ref):
    h = h_ref[...]
    cos, slo, shi = cos_ref[...], slo_ref[...], shi_ref[...]
    q = _rotate(_dot(h, wq_ref[...]), cos, slo, shi)
    q_ref[...] = (q * (LOG2E * HEAD_DIM ** -0.5)).astype(BF16)
    k_ref[...] = _rotate(_dot(h, wk_ref[...]), cos, slo, shi)
    v_ref[...] = _dot(h, wv_ref[...])


def _qkv(hn, w_in, cos, slo, shi):
    m = hn.shape[0]
    c0 = 2 * D_CONV
    tab = pl.BlockSpec((IN_TILE, LANES), lambda i: (i, 0))
    return pl.pallas_call(
        _qkv_kernel,
        grid=(m // IN_TILE,),
        in_specs=[pl.BlockSpec((IN_TILE, D_MODEL), lambda i: (i, 0)),
                  pl.BlockSpec((D_MODEL, Q_W), lambda i: (0, c0 // Q_W)),
                  pl.BlockSpec((D_MODEL, KV_W), lambda i: (0, (c0 + Q_W) // KV_W)),
                  pl.BlockSpec((D_MODEL, KV_W), lambda i: (0, (c0 + Q_W) // KV_W + 1)),
                  tab, tab, tab],
        out_specs=[pl.BlockSpec((IN_TILE, Q_W), lambda i: (i, 0)),
                   pl.BlockSpec((IN_TILE, KV_W), lambda i: (i, 0)),
                   pl.BlockSpec((IN_TILE, KV_W), lambda i: (i, 0))],
        out_shape=[jax.ShapeDtypeStruct((m, Q_W), BF16),
                   jax.ShapeDtypeStruct((m, KV_W), F32),
                   jax.ShapeDtypeStruct((m, KV_W), F32)],
        compiler_params=_params("parallel"),
        name="qkv",
    )(hn, w_in, w_in, w_in, cos, slo, shi)


def _rope_tables(batch, seq, dec_batch, dec_seq):
    inv = ROPE_THETA ** (-jnp.arange(0, ROT_DIM, 2, dtype=F32) / ROT_DIM)
    pos = jnp.concatenate([jnp.tile(jnp.arange(seq, dtype=jnp.int32), batch),
                           jnp.tile(PAST_LEN + jnp.arange(dec_seq, dtype=jnp.int32), dec_batch)])
    ang = pos.astype(F32)[:, None] * inv[None, :]
    cos, sin = jnp.cos(ang), jnp.sin(ang)
    one = jnp.ones((pos.shape[0], HEAD_DIM - ROT_DIM), F32)
    zero = jnp.zeros_like(one)
    zh = jnp.zeros_like(sin)
    c = jnp.concatenate([cos, cos, one], 1)
    lo = jnp.concatenate([-sin, zh, zero], 1)
    hi = jnp.concatenate([zh, sin, zero], 1)
    return [jnp.tile(t, (1, LANES // HEAD_DIM)) for t in (c, lo, hi)]


def _conv_tile(s_ref, row0, c_ref, w_ref, b_ref, lg_ref, lb_ref, rows):
    ext = rows + CONV_HALO - SUBLANES
    for s in range(1, SUBLANES):
        c_ref[s - 1] = s_ref[row0 + s:row0 + s + ext, :]
    base = CONV_HALO - (CONV_WIDTH - 1)
    cols = []
    for cb in range(D_CONV // LANES):
        lanes = slice(cb * LANES, (cb + 1) * LANES)
        acc = jnp.broadcast_to(b_ref[:, lanes], (rows, LANES))
        for j in range(CONV_WIDTH):
            a, s = divmod(base + j, SUBLANES)
            r0 = a * SUBLANES
            if s == 0:
                tap = s_ref[row0 + r0:row0 + r0 + rows, lanes]
            else:
                tap = c_ref[s - 1, r0:r0 + rows, lanes]
            acc = acc + tap * w_ref[j:j + 1, lanes]
        cols.append(acc)
    acc = jnp.concatenate(cols, axis=1)
    mu = jnp.mean(acc, axis=-1, keepdims=True)
    xc = acc - mu
    y = xc * lax.rsqrt(jnp.mean(xc * xc, axis=-1, keepdims=True) + EPS) * lg_ref[...] + lb_ref[...]
    return (y * jax.nn.sigmoid(y)).astype(BF16)


def _conv_sample_kernel(u_ref, past_ref, w_ref, b_ref, lg_ref, lb_ref, o_ref, s_ref, c_ref):
    s_ref[CONV_HALO - (CONV_WIDTH - 1):CONV_HALO, :] = past_ref[0]
    s_ref[CONV_HALO:, :] = u_ref[...]
    o_ref[...] = _conv_tile(s_ref, 0, c_ref, w_ref, b_ref, lg_ref, lb_ref, u_ref.shape[0])


def _conv_common_specs():
    return [pl.BlockSpec((CONV_WIDTH, D_CONV), lambda i: (0, 0)),
            pl.BlockSpec((1, D_CONV), lambda i: (0, 0)),
            pl.BlockSpec((1, D_CONV), lambda i: (0, 0)),
            pl.BlockSpec((1, D_CONV), lambda i: (0, 0))]


def _conv_scratch(block_rows, tile_rows):
    return [pltpu.VMEM((CONV_HALO + block_rows, D_CONV), F32),
            pltpu.VMEM((SUBLANES - 1, CONV_HALO + tile_rows - SUBLANES, D_CONV), F32)]


def _conv_sample(u, row0, state, w, b, lg, lb):
    n_seq, hist, _ = state.shape
    assert hist == CONV_WIDTH - 1
    rows = CHUNK
    blk0 = row0 // rows
    return pl.pallas_call(
        _conv_sample_kernel,
        grid=(n_seq,),
        in_specs=[pl.BlockSpec((rows, D_CONV), lambda i: (blk0 + i, 0)),
                  pl.BlockSpec((1, hist, D_CONV), lambda i: (i, 0, 0)),
                  *_conv_common_specs()],
        out_specs=pl.BlockSpec((rows, D_CONV), lambda i: (i, 0)),
        out_shape=jax.ShapeDtypeStruct((n_seq * rows, D_CONV), BF16),
        scratch_shapes=_conv_scratch(rows, rows),
        compiler_params=_params("parallel"),
        name="conv_sample",
    )(u, state, w, b, lg, lb)


def _attn_tile(q, k, v, sink_ref, valid):
    r = q.shape[0]
    lane_head = lax.broadcasted_iota(jnp.int32, (1, KV_W), 1) // HEAD_DIM
    sel_row = lax.broadcasted_iota(jnp.int32, (KV_W, Q_W), 0)
    sel_col = lax.broadcasted_iota(jnp.int32, (KV_W, Q_W), 1)
    sel = jnp.where(sel_row == sel_col // KV_W * HEAD_DIM + sel_col % HEAD_DIM, 1.0, 0.0).astype(BF16)
    krep = _dot(k, sel).astype(BF16)
    vrep = _dot(v, sel).astype(BF16)
    head_mask = [jnp.where(lane_head == g, 1.0, 0.0).astype(BF16) for g in range(GROUP)]
    scores = []
    for kv in range(N_KV):
        cols = slice(kv * KV_W, (kv + 1) * KV_W)
        lhs = jnp.concatenate([q[:, cols] * head_mask[g] for g in range(GROUP)], axis=0)
        scores.append(lax.dot_general(lhs, krep[:, cols], (((1,), (1,)), ((), ())),
                                      preferred_element_type=F32))
    weights, dens = [], []
    for kv in range(N_KV):
        es = []
        for g in range(GROUP):
            sg = scores[kv][g * r:(g + 1) * r]
            if valid is not None:
                sg = jnp.where(valid, sg, NEG)
            sink = sink_ref[kv * GROUP + g] * LOG2E
            mx = jnp.maximum(jnp.max(sg, axis=-1, keepdims=True), sink)
            e = jnp.exp2(sg - mx)
            dens.append(jnp.sum(e, axis=-1, keepdims=True) + jnp.exp2(sink - mx))
            es.append(e.astype(BF16))
        weights.append(jnp.concatenate(es, axis=0))
    outs = []
    for kv in range(N_KV):
        o4 = _dot(weights[kv], vrep[:, kv * KV_W:(kv + 1) * KV_W])
        og = [o4[g * r:(g + 1) * r] / dens[kv * GROUP + g] for g in range(GROUP)]
        oh = og[GROUP - 1]
        for g in range(GROUP - 2, -1, -1):
            oh = jnp.where(lane_head == g, og[g], oh)
        outs.append(oh)
    return jnp.concatenate(outs, axis=1).astype(BF16)


def _attn_sample_kernel(sink_ref, q_ref, kp_ref, kc_ref, vp_ref, vc_ref, o_ref):
    k = jnp.concatenate([kp_ref[0], kc_ref[...]], axis=0).astype(BF16)
    v = jnp.concatenate([vp_ref[0], vc_ref[...]], axis=0).astype(BF16)
    o_ref[...] = _attn_tile(q_ref[...], k, v, sink_ref, None)


def _attn_sample(q, k, v, sinks, row0, cache_k, cache_v):
    n_seq, w_rows, _ = cache_k.shape
    assert w_rows == WINDOW
    blk0 = row0 // CHUNK
    cache = pl.BlockSpec((1, w_rows, KV_W), lambda i: (i, 0, 0))
    cur = pl.BlockSpec((CHUNK, KV_W), lambda i: (blk0 + i, 0))
    return pl.pallas_call(
        _attn_sample_kernel,
        grid=(n_seq,),
        in_specs=[pl.BlockSpec(memory_space=pltpu.SMEM),
                  pl.BlockSpec((CHUNK, Q_W), lambda i: (blk0 + i, 0)), cache, cur, cache, cur],
        out_specs=pl.BlockSpec((CHUNK, Q_W), lambda i: (i, 0)),
        out_shape=jax.ShapeDtypeStruct((n_seq * CHUNK, Q_W), BF16),
        compiler_params=_params("parallel"),
        name="attn_sample",
    )(sinks, q, cache_k, k, cache_v, v)


def _merge_out(yc, o, gc, ga, x, wc_ref, wa_ref, wo_ref):
    m = gc.astype(F32) * _dot(yc, wc_ref[...]) + ga.astype(F32) * _dot(o, wa_ref[...])
    return x + _dot(m.astype(BF16), wo_ref[...])


def _mix_kernel(per_seq, sink_ref, u_ref, uh_ref, q_ref, kp_ref, kc_ref, vp_ref, vc_ref,
                cw_ref, cb_ref, lg_ref, lb_ref, gc_ref, ga_ref, x_ref, wc_ref, wa_ref, wo_ref,
                out_ref, s_ref, c_ref):
    first = pl.program_id(0) % per_seq == 0
    s_ref[0:CONV_HALO, :] = jnp.where(first, 0.0, uh_ref[...])
    s_ref[CONV_HALO:, :] = u_ref[...]

    k = jnp.concatenate([kp_ref[...], kc_ref[...]], axis=0).astype(BF16)
    v = jnp.concatenate([vp_ref[...], vc_ref[...]], axis=0).astype(BF16)
    nk = WINDOW + MIX_TILE
    col = lax.broadcasted_iota(jnp.int32, (MIX_TILE, nk), 1)
    d = col // CHUNK - lax.broadcasted_iota(jnp.int32, (MIX_TILE, nk), 0) // CHUNK
    band = (d >= 0) & (d <= WINDOW // CHUNK)

    for r0 in range(0, MIX_ROWS, MIX_TILE):
        rows = slice(r0, r0 + MIX_TILE)
        yc = _conv_tile(s_ref, r0, c_ref, cw_ref, cb_ref, lg_ref, lb_ref, MIX_TILE)
        valid = band & ((col >= WINDOW) | jnp.logical_not(first)) if r0 == 0 else band
        o = _attn_tile(q_ref[rows, :], k[r0:r0 + nk], v[r0:r0 + nk], sink_ref, valid)
        out_ref[rows, :] = _merge_out(yc, o, gc_ref[rows, :], ga_ref[rows, :], x_ref[rows, :],
                                      wc_ref, wa_ref, wo_ref)


def _mix_prompt(u, q, k, v, gates, x, sinks, cw, cb, lg, lb, wc, wa, wo, n_rows, seq):
    rows = MIX_ROWS
    halo_per_block = rows // CONV_HALO
    pre_per_block = rows // WINDOW
    kv_prev = pl.BlockSpec((WINDOW, KV_W), lambda i: (jnp.maximum(i * pre_per_block - 1, 0), 0))
    kv_cur = pl.BlockSpec((rows, KV_W), lambda i: (i, 0))
    return pl.pallas_call(
        functools.partial(_mix_kernel, seq // rows),
        grid=(n_rows // rows,),
        in_specs=[pl.BlockSpec(memory_space=pltpu.SMEM),
                  pl.BlockSpec((rows, D_CONV), lambda i: (i, 0)),
                  pl.BlockSpec((CONV_HALO, D_CONV), lambda i: (jnp.maximum(i * halo_per_block - 1, 0), 0)),
                  pl.BlockSpec((rows, Q_W), lambda i: (i, 0)),
                  kv_prev, kv_cur, kv_prev, kv_cur,
                  *_conv_common_specs(),
                  pl.BlockSpec((rows, D_MODEL), lambda i: (i, 0)),
                  pl.BlockSpec((rows, D_MODEL), lambda i: (i, 1)),
                  pl.BlockSpec((rows, D_MODEL), lambda i: (i, 0)),
                  _resident((D_CONV, D_MODEL)), _resident((Q_W, D_MODEL)), _resident((D_MODEL, D_MODEL))],
        out_specs=pl.BlockSpec((rows, D_MODEL), lambda i: (i, 0)),
        out_shape=jax.ShapeDtypeStruct((x.shape[0], D_MODEL), F32),
        scratch_shapes=_conv_scratch(rows, MIX_TILE),
        compiler_params=_params("parallel"),
        name="mix_prompt",
    )(sinks, u, u, q, k, k, v, v, cw, cb, lg, lb, gates, gates, x, wc, wa, wo)


def _post_sample_kernel(yc_ref, o_ref, gc_ref, ga_ref, x_ref, wc_ref, wa_ref, wo_ref, _, out_ref):
    out_ref[...] = _merge_out(yc_ref[...], o_ref[...], gc_ref[...], ga_ref[...], x_ref[...],
                              wc_ref, wa_ref, wo_ref)


def _post_sample(yc, o, gates, x, wc, wa, wo, x_out, row0):
    tm = POST_TILE
    blk0 = row0 // tm
    return pl.pallas_call(
        _post_sample_kernel,
        grid=(yc.shape[0] // tm,),
        in_specs=[pl.BlockSpec((tm, D_CONV), lambda i: (i, 0)),
                  pl.BlockSpec((tm, Q_W), lambda i: (i, 0)),
                  pl.BlockSpec((tm, D_MODEL), lambda i: (blk0 + i, 0)),
                  pl.BlockSpec((tm, D_MODEL), lambda i: (blk0 + i, 1)),
                  pl.BlockSpec((tm, D_MODEL), lambda i: (blk0 + i, 0)),
                  _resident((D_CONV, D_MODEL)), _resident((Q_W, D_MODEL)), _resident((D_MODEL, D_MODEL)),
                  pl.BlockSpec(memory_space=pl.ANY)],
        out_specs=pl.BlockSpec((tm, D_MODEL), lambda i: (blk0 + i, 0)),
        out_shape=jax.ShapeDtypeStruct(x_out.shape, F32),
        input_output_aliases={8: 0},
        compiler_params=_params("parallel"),
        name="post_sample",
    )(yc, o, gates, gates, x, wc, wa, wo, x_out)


def _ple_kernel(n_prompt_tiles, x_ref, pp_ref, ps_ref, gn_ref, wg_ref, wp_ref, fn_ref, yp_ref, ys_ref):
    i = pl.program_id(0)
    x = x_ref[...]
    gate = jax.nn.sigmoid(_dot(_rms(x, gn_ref[...]).astype(BF16), wg_ref[...]))
    pe = jnp.where(i < n_prompt_tiles, pp_ref[...], ps_ref[...]).astype(BF16)
    y = _rms(x + gate * _dot(pe, wp_ref[...]), fn_ref[...])

    @pl.when(i < n_prompt_tiles)
    def _():
        yp_ref[...] = y

    @pl.when(i >= n_prompt_tiles)
    def _():
        ys_ref[...] = y


def _ple(x, pp, ps, gn, wg, wp, fn):
    tm = PLE_TILE
    n_p, n_s = pp.shape[0] // tm, ps.shape[0] // tm
    vec = pl.BlockSpec((1, D_MODEL), lambda i: (0, 0))

    def p_idx(i):
        return (jnp.minimum(i, n_p - 1), 0)

    def s_idx(i):
        return (jnp.maximum(i - n_p, 0), 0)

    return pl.pallas_call(
        functools.partial(_ple_kernel, n_p),
        grid=(n_p + n_s,),
        in_specs=[pl.BlockSpec((tm, D_MODEL), lambda i: (i, 0)),
                  pl.BlockSpec((tm, D_PLE), p_idx),
                  pl.BlockSpec((tm, D_PLE), s_idx),
                  vec, _resident((D_MODEL, D_MODEL)), _resident((D_PLE, D_MODEL)), vec],
        out_specs=[pl.BlockSpec((tm, D_MODEL), p_idx), pl.BlockSpec((tm, D_MODEL), s_idx)],
        out_shape=[jax.ShapeDtypeStruct((pp.shape[0], D_MODEL), F32),
                   jax.ShapeDtypeStruct((ps.shape[0], D_MODEL), F32)],
        compiler_params=_params("arbitrary"),
        name="ple_final",
    )(x, pp, ps, gn, wg, wp, fn)


def kernel(x_prompt, x_sample, p_prompt, p_sample, state_conv, cache_k, cache_v, ffn1_norm, ffn1_w_gu, ffn1_w_down, mix_norm, w_in, conv_w, conv_b, conv_ln_g, conv_ln_b, conv_w_out, attn_sinks, attn_w_out, w_out, ffn2_norm, ffn2_w_gu, ffn2_w_down, ple_norm, ple_w_gate, ple_w_proj, final_norm):
    assert x_prompt.shape[-1] == D_MODEL and ffn1_norm.shape[0] == 1 and w_in.shape[-1] == IN_COLS
    batch, seq, _ = x_prompt.shape
    dec_batch, dec_seq, _ = x_sample.shape
    assert dec_seq == CHUNK and seq % ROW_TILE == 0
    n_p, n_s = batch * seq, dec_batch * dec_seq

    def vec(a):
        return a.reshape(1, -1)

    wgu1, wd1 = ffn1_w_gu[0].astype(BF16), ffn1_w_down[0].astype(BF16)
    wgu2, wd2 = ffn2_w_gu[0].astype(BF16), ffn2_w_down[0].astype(BF16)
    w_in_b = w_in[0].astype(BF16)
    w_co = conv_w_out[0].astype(BF16)
    w_ao = attn_w_out[0].astype(BF16)
    w_o = w_out[0].astype(BF16)
    w_pg = ple_w_gate[0].astype(BF16)
    w_pp = ple_w_proj[0].astype(BF16)

    xp = x_prompt.reshape(n_p, D_MODEL)
    xs = x_sample.reshape(n_s, D_MODEL)

    x1, hn = _ffn1(xp, xs, vec(ffn1_norm), wgu1, wd1, vec(mix_norm))

    u = _glu(hn, w_in_b)
    cos, slo, shi = _rope_tables(batch, seq, dec_batch, dec_seq)
    q, k, v = _qkv(hn, w_in_b, cos, slo, shi)
    gates = _gates(hn, w_in_b)

    cw, cb, lg, lb = conv_w[0], vec(conv_b), vec(conv_ln_g), vec(conv_ln_b)
    sinks = attn_sinks[0]
    ck = cache_k[0].reshape(dec_batch, -1, KV_W)
    cv = cache_v[0].reshape(dec_batch, -1, KV_W)
    yc_s = _conv_sample(u, n_p, state_conv[0], cw, cb, lg, lb)
    o_s = _attn_sample(q, k, v, sinks, n_p, ck, cv)
    x2 = _mix_prompt(u, q, k, v, gates, x1, sinks, cw, cb, lg, lb, w_co, w_ao, w_o, n_p, seq)
    x2 = _post_sample(yc_s, o_s, gates, x1, w_co, w_ao, w_o, x2, n_p)
    x3 = _ffn2(x2, vec(ffn2_norm), wgu2, wd2)

    yp, ys = _ple(x3, p_prompt[0].reshape(n_p, D_PLE), p_sample[0].reshape(n_s, D_PLE),
                  vec(ple_norm), w_pg, w_pp, vec(final_norm))

    w_rows = ck.shape[1]

    def prompt_tail(a, rows):
        return jnp.stack([a[(b + 1) * seq - rows:(b + 1) * seq] for b in range(batch)])

    k_p = prompt_tail(k, WINDOW).reshape(batch, WINDOW, N_KV, HEAD_DIM)
    v_p = prompt_tail(v, WINDOW).reshape(batch, WINDOW, N_KV, HEAD_DIM)
    c_p = prompt_tail(u, CONV_WIDTH - 1)
    k_s = jnp.concatenate([cache_k[0], k[n_p:].reshape(dec_batch, dec_seq, N_KV, HEAD_DIM)], 1)[:, -w_rows:]
    v_s = jnp.concatenate([cache_v[0], v[n_p:].reshape(dec_batch, dec_seq, N_KV, HEAD_DIM)], 1)[:, -w_rows:]
    c_s = jnp.concatenate([state_conv[0], u[n_p:].reshape(dec_batch, dec_seq, D_CONV)], 1)[:, -(CONV_WIDTH - 1):]
    return (yp.reshape(batch, seq, D_MODEL), ys.reshape(dec_batch, dec_seq, D_MODEL),
            k_p[None], v_p[None], c_p[None], k_s[None], v_s[None], c_s[None])
```

```python
import functools

import jax
import jax.numpy as jnp
from jax import lax
from jax.experimental import pallas as pl
from jax.experimental.pallas import tpu as pltpu

D_MODEL = 2048
D_PLE = 256
D_FF = 5504
D_CONV = 1024
CONV_WIDTH = 31
HEAD_DIM = 64
N_HEADS = 16
N_KV = 4
GROUP = N_HEADS // N_KV
ROT_DIM = 16
ROPE_THETA = 500000.0
CHUNK = 64
WINDOW = 128
PAST_LEN = 1024
EPS = 1e-6
NEG = -1e30
LOG2E = 1.4426950408889634
Q_W = N_HEADS * HEAD_DIM
KV_W = N_KV * HEAD_DIM
IN_COLS = 2 * D_CONV + Q_W + 2 * KV_W + 2 * D_MODEL

LANES = 128
SUBLANES = 8
FF_TILE = 512
FF_STEPS = -(-D_FF // FF_TILE)
ROW_TILE = 512
IN_TILE = 1088
IN_COL_TILE = 1024
POST_TILE = 256
PLE_TILE = 512
CONV_HALO = 32
MIX_ROWS = 256
MIX_TILE = WINDOW
VMEM_LIMIT = 56 * 1024 * 1024

F32 = jnp.float32
BF16 = jnp.bfloat16


def _params(*sem):
    return pltpu.CompilerParams(dimension_semantics=sem, vmem_limit_bytes=VMEM_LIMIT)


def _rms(x, g):
    return x * lax.rsqrt(jnp.mean(x * x, axis=-1, keepdims=True) + EPS) * g


def _dot(a, b):
    return jnp.dot(a, b, preferred_element_type=F32)


def _resident(shape):
    return pl.BlockSpec(shape, lambda *_: (0,) * len(shape), pipeline_mode=pl.Buffered(1))


def _ff_window(i, f):
    return jnp.where(i % 2 == 0, f, FF_STEPS - 1 - f)


def _ffn_kernel(n_prompt_tiles, with_hn, xp_ref, xs_ref, g1_ref, wg_ref, wu_ref, wd_ref, g2_ref,
                xo_ref, *rest):
    hn_ref, xn_ref = rest if with_hn else (None, rest[0])
    i, f = pl.program_id(0), pl.program_id(1)

    @pl.when(f == 0)
    def _():
        x = jnp.where(i < n_prompt_tiles, xp_ref[...], xs_ref[...])
        xn_ref[...] = _rms(x, g1_ref[...]).astype(BF16)
        xo_ref[...] = x

    def accumulate(cols):
        xn = xn_ref[...]
        g = _dot(xn, wg_ref[:, cols])
        u = _dot(xn, wu_ref[:, cols])
        h = (g * jax.nn.sigmoid(g)) * (u * 0.5)
        xo_ref[...] += _dot(h.astype(BF16), wd_ref[cols, :])

    top = _ff_window(i, f) == FF_STEPS - 1
    pl.when(jnp.logical_not(top))(lambda: accumulate(slice(0, FF_TILE)))
    pl.when(top)(lambda: accumulate(slice(FF_STEPS * FF_TILE - D_FF, FF_TILE)))

    if with_hn:
        @pl.when(f == FF_STEPS - 1)
        def _():
            hn_ref[...] = _rms(xo_ref[...], g2_ref[...]).astype(BF16)


def _ffn(xp, xs, g1, wgu, wd, g2, with_hn, name):
    n_p, n_s = xp.shape[0] // ROW_TILE, xs.shape[0] // ROW_TILE
    assert n_s == 1 and xs.shape[0] == ROW_TILE
    m = xp.shape[0] + xs.shape[0]

    def start(i, f):
        return pl.multiple_of(jnp.minimum(_ff_window(i, f) * FF_TILE, D_FF - FF_TILE), LANES)

    cols = (pl.Element(D_MODEL), pl.Element(FF_TILE))
    rows = (pl.Element(FF_TILE), pl.Element(D_MODEL))
    vec = pl.BlockSpec((1, D_MODEL), lambda i, f: (0, 0))
    row = pl.BlockSpec((ROW_TILE, D_MODEL), lambda i, f: (i, 0))
    out_shape = [jax.ShapeDtypeStruct((m, D_MODEL), F32)]
    if with_hn:
        out_shape.append(jax.ShapeDtypeStruct((m, D_MODEL), BF16))
    return pl.pallas_call(
        functools.partial(_ffn_kernel, n_p, with_hn),
        grid=(n_p + n_s, FF_STEPS),
        in_specs=[pl.BlockSpec((ROW_TILE, D_MODEL), lambda i, f: (jnp.minimum(i, n_p - 1), 0)),
                  pl.BlockSpec((ROW_TILE, D_MODEL), lambda i, f: (0, 0)),
                  vec,
                  pl.BlockSpec(cols, lambda i, f: (0, start(i, f))),
                  pl.BlockSpec(cols, lambda i, f: (0, pl.multiple_of(D_FF + start(i, f), LANES))),
                  pl.BlockSpec(rows, lambda i, f: (start(i, f), 0)),
                  vec],
        out_specs=[row] * len(out_shape),
        out_shape=out_shape,
        scratch_shapes=[pltpu.VMEM((ROW_TILE, D_MODEL), BF16)],
        compiler_params=_params("parallel", "arbitrary"),
        name=name,
    )(xp, xs, g1, wgu, wgu, wd, g2)


def _glu_kernel(h_ref, wa_ref, wb_ref, o_ref):
    h = h_ref[...]
    o_ref[...] = _dot(h, wa_ref[...]) * jax.nn.sigmoid(_dot(h, wb_ref[...]))


def _glu(hn, w_in):
    m, tn = hn.shape[0], IN_COL_TILE
    nj = D_CONV // tn
    return pl.pallas_call(
        _glu_kernel,
        grid=(m // IN_TILE, nj),
        in_specs=[pl.BlockSpec((IN_TILE, D_MODEL), lambda i, j: (i, 0)),
                  pl.BlockSpec((D_MODEL, tn), lambda i, j: (0, j)),
                  pl.BlockSpec((D_MODEL, tn), lambda i, j: (0, j + nj))],
        out_specs=pl.BlockSpec((IN_TILE, tn), lambda i, j: (i, j)),
        out_shape=jax.ShapeDtypeStruct((m, D_CONV), F32),
        compiler_params=_params("parallel", "arbitrary"),
        name="glu",
    )(hn, w_in, w_in)


def _gates_kernel(h_ref, w_ref, o_ref):
    o_ref[...] = jax.nn.sigmoid(_dot(h_ref[...], w_ref[...])).astype(BF16)


def _gates(hn, w_in):
    m, tn = hn.shape[0], IN_COL_TILE
    col0 = 2 * D_CONV + Q_W + 2 * KV_W
    return pl.pallas_call(
        _gates_kernel,
        grid=(m // IN_TILE, 2 * D_MODEL // tn),
        in_specs=[pl.BlockSpec((IN_TILE, D_MODEL), lambda i, j: (i, 0)),
                  pl.BlockSpec((pl.Element(D_MODEL), pl.Element(tn)),
                               lambda i, j: (0, pl.multiple_of(col0 + j * tn, LANES)))],
        out_specs=pl.BlockSpec((IN_TILE, tn), lambda i, j: (i, j)),
        out_shape=jax.ShapeDtypeStruct((m, 2 * D_MODEL), BF16),
        compiler_params=_params("parallel", "arbitrary"),
        name="gates",
    )(hn, w_in)


def _rotate(x, cos, sin_lo, sin_hi):
    cols = []
    for c in range(x.shape[1] // LANES):
        xb = x[:, c * LANES:(c + 1) * LANES]
        cols.append(xb * cos
                    + pltpu.roll(xb, LANES - ROT_DIM // 2, 1) * sin_lo
                    + pltpu.roll(xb, ROT_DIM // 2, 1) * sin_hi)
    return jnp.concatenate(cols, axis=1)


def _qkv_kernel(h_ref, wq_ref, wk_ref, wv_ref, cos_ref, slo_ref, shi_ref, q_ref, k_ref, v_ref):
    h = h_ref[...]
    cos, slo, shi = cos_ref[...], slo_ref[...], shi_ref[...]
    q = _rotate(_dot(h, wq_ref[...]), cos, slo, shi)
    q_ref[...] = (q * (LOG2E * HEAD_DIM ** -0.5)).astype(BF16)
    k_ref[...] = _rotate(_dot(h, wk_ref[...]), cos, slo, shi)
    v_ref[...] = _dot(h, wv_ref[...])


def _qkv(hn, w_in, cos, slo, shi):
    m = hn.shape[0]
    c0 = 2 * D_CONV
    tab = pl.BlockSpec((IN_TILE, LANES), lambda i: (i, 0))
    return pl.pallas_call(
        _qkv_kernel,
        grid=(m // IN_TILE,),
        in_specs=[pl.BlockSpec((IN_TILE, D_MODEL), lambda i: (i, 0)),
                  pl.BlockSpec((D_MODEL, Q_W), lambda i: (0, c0 // Q_W)),
                  pl.BlockSpec((D_MODEL, KV_W), lambda i: (0, (c0 + Q_W) // KV_W)),
                  pl.BlockSpec((D_MODEL, KV_W), lambda i: (0, (c0 + Q_W) // KV_W + 1)),
                  tab, tab, tab],
        out_specs=[pl.BlockSpec((IN_TILE, Q_W), lambda i: (i, 0)),
                   pl.BlockSpec((IN_TILE, KV_W), lambda i: (i, 0)),
                   pl.BlockSpec((IN_TILE, KV_W), lambda i: (i, 0))],
        out_shape=[jax.ShapeDtypeStruct((m, Q_W), BF16),
                   jax.ShapeDtypeStruct((m, KV_W), F32),
                   jax.ShapeDtypeStruct((m, KV_W), F32)],
        compiler_params=_params("parallel"),
        name="qkv",
    )(hn, w_in, w_in, w_in, cos, slo, shi)


def _rope_tables(batch, seq, dec_batch, dec_seq):
    inv = ROPE_THETA ** (-jnp.arange(0, ROT_DIM, 2, dtype=F32) / ROT_DIM)
    pos = jnp.concatenate([jnp.tile(jnp.arange(seq, dtype=jnp.int32), batch),
                           jnp.tile(PAST_LEN + jnp.arange(dec_seq, dtype=jnp.int32), dec_batch)])
    ang = pos.astype(F32)[:, None] * inv[None, :]
    cos, sin = jnp.cos(ang), jnp.sin(ang)
    one = jnp.ones((pos.shape[0], HEAD_DIM - ROT_DIM), F32)
    zero = jnp.zeros_like(one)
    zh = jnp.zeros_like(sin)
    c = jnp.concatenate([cos, cos, one], 1)
    lo = jnp.concatenate([-sin, zh, zero], 1)
    hi = jnp.concatenate([zh, sin, zero], 1)
    return [jnp.tile(t, (1, LANES // HEAD_DIM)) for t in (c, lo, hi)]


def _conv_tile(s_ref, row0, c_ref, w_ref, b_ref, lg_ref, lb_ref, rows):
    ext = rows + CONV_HALO - SUBLANES
    for s in range(1, SUBLANES):
        c_ref[s - 1] = s_ref[row0 + s:row0 + s + ext, :]
    base = CONV_HALO - (CONV_WIDTH - 1)
    cols = []
    for cb in range(D_CONV // LANES):
        lanes = slice(cb * LANES, (cb + 1) * LANES)
        acc = jnp.broadcast_to(b_ref[:, lanes], (rows, LANES))
        for j in range(CONV_WIDTH):
            a, s = divmod(base + j, SUBLANES)
            r0 = a * SUBLANES
            if s == 0:
                tap = s_ref[row0 + r0:row0 + r0 + rows, lanes]
            else:
                tap = c_ref[s - 1, r0:r0 + rows, lanes]
            acc = acc + tap * w_ref[j:j + 1, lanes]
        cols.append(acc)
    acc = jnp.concatenate(cols, axis=1)
    mu = jnp.mean(acc, axis=-1, keepdims=True)
    xc = acc - mu
    y = xc * lax.rsqrt(jnp.mean(xc * xc, axis=-1, keepdims=True) + EPS) * lg_ref[...] + lb_ref[...]
    return (y * jax.nn.sigmoid(y)).astype(BF16)


def _conv_sample_kernel(u_ref, past_ref, w_ref, b_ref, lg_ref, lb_ref, o_ref, s_ref, c_ref):
    s_ref[CONV_HALO - (CONV_WIDTH - 1):CONV_HALO, :] = past_ref[0]
    s_ref[CONV_HALO:, :] = u_ref[...]
    o_ref[...] = _conv_tile(s_ref, 0, c_ref, w_ref, b_ref, lg_ref, lb_ref, u_ref.shape[0])


def _conv_common_specs():
    return [pl.BlockSpec((CONV_WIDTH, D_CONV), lambda i: (0, 0)),
            pl.BlockSpec((1, D_CONV), lambda i: (0, 0)),
            pl.BlockSpec((1, D_CONV), lambda i: (0, 0)),
            pl.BlockSpec((1, D_CONV), lambda i: (0, 0))]


def _conv_scratch(block_rows, tile_rows):
    return [pltpu.VMEM((CONV_HALO + block_rows, D_CONV), F32),
            pltpu.VMEM((SUBLANES - 1, CONV_HALO + tile_rows - SUBLANES, D_CONV), F32)]


def _conv_sample(u, row0, state, w, b, lg, lb):
    n_seq, hist, _ = state.shape
    assert hist == CONV_WIDTH - 1
    rows = CHUNK
    blk0 = row0 // rows
    return pl.pallas_call(
        _conv_sample_kernel,
        grid=(n_seq,),
        in_specs=[pl.BlockSpec((rows, D_CONV), lambda i: (blk0 + i, 0)),
                  pl.BlockSpec((1, hist, D_CONV), lambda i: (i, 0, 0)),
                  *_conv_common_specs()],
        out_specs=pl.BlockSpec((rows, D_CONV), lambda i: (i, 0)),
        out_shape=jax.ShapeDtypeStruct((n_seq * rows, D_CONV), BF16),
        scratch_shapes=_conv_scratch(rows, rows),
        compiler_params=_params("parallel"),
        name="conv_sample",
    )(u, state, w, b, lg, lb)


def _attn_tile(q, k, v, sink_ref, valid):
    r = q.shape[0]
    lane_head = lax.broadcasted_iota(jnp.int32, (1, KV_W), 1) // HEAD_DIM
    sel_row = lax.broadcasted_iota(jnp.int32, (KV_W, Q_W), 0)
    sel_col = lax.broadcasted_iota(jnp.int32, (KV_W, Q_W), 1)
    sel = jnp.where(sel_row == sel_col // KV_W * HEAD_DIM + sel_col % HEAD_DIM, 1.0, 0.0).astype(BF16)
    krep = _dot(k, sel).astype(BF16)
    vrep = _dot(v, sel).astype(BF16)
    head_mask = [jnp.where(lane_head == g, 1.0, 0.0).astype(BF16) for g in range(GROUP)]
    scores = []
    for kv in range(N_KV):
        cols = slice(kv * KV_W, (kv + 1) * KV_W)
        lhs = jnp.concatenate([q[:, cols] * head_mask[g] for g in range(GROUP)], axis=0)
        scores.append(lax.dot_general(lhs, krep[:, cols], (((1,), (1,)), ((), ())),
                                      preferred_element_type=F32))
    weights, dens = [], []
    for kv in range(N_KV):
        es = []
        for g in range(GROUP):
            sg = scores[kv][g * r:(g + 1) * r]
            if valid is not None:
                sg = jnp.where(valid, sg, NEG)
            sink = sink_ref[kv * GROUP + g] * LOG2E
            mx = jnp.maximum(jnp.max(sg, axis=-1, keepdims=True), sink)
            e = jnp.exp2(sg - mx)
            dens.append(jnp.sum(e, axis=-1, keepdims=True) + jnp.exp2(sink - mx))
            es.append(e.astype(BF16))
        weights.append(jnp.concatenate(es, axis=0))
    outs = []
    for kv in range(N_KV):
        o4 = _dot(weights[kv], vrep[:, kv * KV_W:(kv + 1) * KV_W])
        og = [o4[g * r:(g + 1) * r] / dens[kv * GROUP + g] for g in range(GROUP)]
        oh = og[GROUP - 1]
        for g in range(GROUP - 2, -1, -1):
            oh = jnp.where(lane_head == g, og[g], oh)
        outs.append(oh)
    return jnp.concatenate(outs, axis=1).astype(BF16)


def _attn_sample_kernel(sink_ref, q_ref, kp_ref, kc_ref, vp_ref, vc_ref, o_ref):
    k = jnp.concatenate([kp_ref[0], kc_ref[...]], axis=0).astype(BF16)
    v = jnp.concatenate([vp_ref[0], vc_ref[...]], axis=0).astype(BF16)
    o_ref[...] = _attn_tile(q_ref[...], k, v, sink_ref, None)


def _attn_sample(q, k, v, sinks, row0, cache_k, cache_v):
    n_seq, w_rows, _ = cache_k.shape
    assert w_rows == WINDOW
    blk0 = row0 // CHUNK
    cache = pl.BlockSpec((1, w_rows, KV_W), lambda i: (i, 0, 0))
    cur = pl.BlockSpec((CHUNK, KV_W), lambda i: (blk0 + i, 0))
    return pl.pallas_call(
        _attn_sample_kernel,
        grid=(n_seq,),
        in_specs=[pl.BlockSpec(memory_space=pltpu.SMEM),
                  pl.BlockSpec((CHUNK, Q_W), lambda i: (blk0 + i, 0)), cache, cur, cache, cur],
        out_specs=pl.BlockSpec((CHUNK, Q_W), lambda i: (i, 0)),
        out_shape=jax.ShapeDtypeStruct((n_seq * CHUNK, Q_W), BF16),
        compiler_params=_params("parallel"),
        name="attn_sample",
    )(sinks, q, cache_k, k, cache_v, v)


def _merge_out(yc, o, gc, ga, x, wc_ref, wa_ref, wo_ref):
    m = gc.astype(F32) * _dot(yc, wc_ref[...]) + ga.astype(F32) * _dot(o, wa_ref[...])
    return x + _dot(m.astype(BF16), wo_ref[...])


def _mix_kernel(per_seq, sink_ref, u_ref, uh_ref, q_ref, kp_ref, kc_ref, vp_ref, vc_ref,
                cw_ref, cb_ref, lg_ref, lb_ref, gc_ref, ga_ref, x_ref, wc_ref, wa_ref, wo_ref,
                out_ref, s_ref, c_ref):
    first = pl.program_id(0) % per_seq == 0
    s_ref[0:CONV_HALO, :] = jnp.where(first, 0.0, uh_ref[...])
    s_ref[CONV_HALO:, :] = u_ref[...]

    k = jnp.concatenate([kp_ref[...], kc_ref[...]], axis=0).astype(BF16)
    v = jnp.concatenate([vp_ref[...], vc_ref[...]], axis=0).astype(BF16)
    nk = WINDOW + MIX_TILE
    col = lax.broadcasted_iota(jnp.int32, (MIX_TILE, nk), 1)
    d = col // CHUNK - lax.broadcasted_iota(jnp.int32, (MIX_TILE, nk), 0) // CHUNK
    band = (d >= 0) & (d <= WINDOW // CHUNK)

    for r0 in range(0, MIX_ROWS, MIX_TILE):
        rows = slice(r0, r0 + MIX_TILE)
        yc = _conv_tile(s_ref, r0, c_ref, cw_ref, cb_ref, lg_ref, lb_ref, MIX_TILE)
        valid = band & ((col >= WINDOW) | jnp.logical_not(first)) if r0 == 0 else band
        o = _attn_tile(q_ref[rows, :], k[r0:r0 + nk], v[r0:r0 + nk], sink_ref, valid)
        out_ref[rows, :] = _merge_out(yc, o, gc_ref[rows, :], ga_ref[rows, :], x_ref[rows, :],
                                      wc_ref, wa_ref, wo_ref)


def _mix_prompt(u, q, k, v, gates, x, sinks, cw, cb, lg, lb, wc, wa, wo, n_rows, seq):
    rows = MIX_ROWS
    halo_per_block = rows // CONV_HALO
    pre_per_block = rows // WINDOW
    kv_prev = pl.BlockSpec((WINDOW, KV_W), lambda i: (jnp.maximum(i * pre_per_block - 1, 0), 0))
    kv_cur = pl.BlockSpec((rows, KV_W), lambda i: (i, 0))
    return pl.pallas_call(
        functools.partial(_mix_kernel, seq // rows),
        grid=(n_rows // rows,),
        in_specs=[pl.BlockSpec(memory_space=pltpu.SMEM),
                  pl.BlockSpec((rows, D_CONV), lambda i: (i, 0)),
                  pl.BlockSpec((CONV_HALO, D_CONV), lambda i: (jnp.maximum(i * halo_per_block - 1, 0), 0)),
                  pl.BlockSpec((rows, Q_W), lambda i: (i, 0)),
                  kv_prev, kv_cur, kv_prev, kv_cur,
                  *_conv_common_specs(),
                  pl.BlockSpec((rows, D_MODEL), lambda i: (i, 0)),
                  pl.BlockSpec((rows, D_MODEL), lambda i: (i, 1)),
                  pl.BlockSpec((rows, D_MODEL), lambda i: (i, 0)),
                  _resident((D_CONV, D_MODEL)), _resident((Q_W, D_MODEL)), _resident((D_MODEL, D_MODEL))],
        out_specs=pl.BlockSpec((rows, D_MODEL), lambda i: (i, 0)),
        out_shape=jax.ShapeDtypeStruct((n_rows, D_MODEL), F32),
        scratch_shapes=_conv_scratch(rows, MIX_TILE),
        compiler_params=_params("parallel"),
        name="mix_prompt",
    )(sinks, u, u, q, k, k, v, v, cw, cb, lg, lb, gates, gates, x, wc, wa, wo)


def _post_sample_kernel(yc_ref, o_ref, gc_ref, ga_ref, x_ref, wc_ref, wa_ref, wo_ref, out_ref):
    out_ref[...] = _merge_out(yc_ref[...], o_ref[...], gc_ref[...], ga_ref[...], x_ref[...],
                              wc_ref, wa_ref, wo_ref)


def _post_sample(yc, o, gates, x, wc, wa, wo, row0):
    tm = POST_TILE
    blk0 = row0 // tm
    return pl.pallas_call(
        _post_sample_kernel,
        grid=(yc.shape[0] // tm,),
        in_specs=[pl.BlockSpec((tm, D_CONV), lambda i: (i, 0)),
                  pl.BlockSpec((tm, Q_W), lambda i: (i, 0)),
                  pl.BlockSpec((tm, D_MODEL), lambda i: (blk0 + i, 0)),
                  pl.BlockSpec((tm, D_MODEL), lambda i: (blk0 + i, 1)),
                  pl.BlockSpec((tm, D_MODEL), lambda i: (blk0 + i, 0)),
                  _resident((D_CONV, D_MODEL)), _resident((Q_W, D_MODEL)), _resident((D_MODEL, D_MODEL))],
        out_specs=pl.BlockSpec((tm, D_MODEL), lambda i: (i, 0)),
        out_shape=jax.ShapeDtypeStruct((yc.shape[0], D_MODEL), F32),
        compiler_params=_params("parallel"),
        name="post_sample",
    )(yc, o, gates, gates, x, wc, wa, wo)


def _ple_kernel(n_prompt_tiles, x_ref, pp_ref, ps_ref, gn_ref, wg_ref, wp_ref, fn_ref, yp_ref, ys_ref):
    i = pl.program_id(0)
    x = x_ref[...]
    gate = jax.nn.sigmoid(_dot(_rms(x, gn_ref[...]).astype(BF16), wg_ref[...]))
    pe = jnp.where(i < n_prompt_tiles, pp_ref[...], ps_ref[...]).astype(BF16)
    y = _rms(x + gate * _dot(pe, wp_ref[...]), fn_ref[...])

    @pl.when(i < n_prompt_tiles)
    def _():
        yp_ref[...] = y

    @pl.when(i >= n_prompt_tiles)
    def _():
        ys_ref[...] = y


def _ple(x, pp, ps, gn, wg, wp, fn):
    tm = PLE_TILE
    n_p, n_s = pp.shape[0] // tm, ps.shape[0] // tm
    vec = pl.BlockSpec((1, D_MODEL), lambda i: (0, 0))

    def p_idx(i):
        return (jnp.minimum(i, n_p - 1), 0)

    def s_idx(i):
        return (jnp.maximum(i - n_p, 0), 0)

    return pl.pallas_call(
        functools.partial(_ple_kernel, n_p),
        grid=(n_p + n_s,),
        in_specs=[pl.BlockSpec((tm, D_MODEL), lambda i: (i, 0)),
                  pl.BlockSpec((tm, D_PLE), p_idx),
                  pl.BlockSpec((tm, D_PLE), s_idx),
                  vec, _resident((D_MODEL, D_MODEL)), _resident((D_PLE, D_MODEL)), vec],
        out_specs=[pl.BlockSpec((tm, D_MODEL), p_idx), pl.BlockSpec((tm, D_MODEL), s_idx)],
        out_shape=[jax.ShapeDtypeStruct((pp.shape[0], D_MODEL), F32),
                   jax.ShapeDtypeStruct((ps.shape[0], D_MODEL), F32)],
        compiler_params=_params("arbitrary"),
        name="ple_final",
    )(x, pp, ps, gn, wg, wp, fn)


def kernel(x_prompt, x_sample, p_prompt, p_sample, state_conv, cache_k, cache_v, ffn1_norm, ffn1_w_gu, ffn1_w_down, mix_norm, w_in, conv_w, conv_b, conv_ln_g, conv_ln_b, conv_w_out, attn_sinks, attn_w_out, w_out, ffn2_norm, ffn2_w_gu, ffn2_w_down, ple_norm, ple_w_gate, ple_w_proj, final_norm):
    assert x_prompt.shape[-1] == D_MODEL and ffn1_norm.shape[0] == 1 and w_in.shape[-1] == IN_COLS
    batch, seq, _ = x_prompt.shape
    dec_batch, dec_seq, _ = x_sample.shape
    assert dec_seq == CHUNK and seq % ROW_TILE == 0
    n_p, n_s = batch * seq, dec_batch * dec_seq

    def vec(a):
        return a.reshape(1, -1)

    wgu1, wd1 = ffn1_w_gu[0].astype(BF16), ffn1_w_down[0].astype(BF16)
    wgu2, wd2 = ffn2_w_gu[0].astype(BF16), ffn2_w_down[0].astype(BF16)
    w_in_b = w_in[0].astype(BF16)
    w_co = conv_w_out[0].astype(BF16)
    w_ao = attn_w_out[0].astype(BF16)
    w_o = w_out[0].astype(BF16)
    w_pg = ple_w_gate[0].astype(BF16)
    w_pp = ple_w_proj[0].astype(BF16)

    xp = x_prompt.reshape(n_p, D_MODEL)
    xs = x_sample.reshape(n_s, D_MODEL)

    x1, hn = _ffn(xp, xs, vec(ffn1_norm), wgu1, wd1, vec(mix_norm), True, "ffn1")

    u = _glu(hn, w_in_b)
    cos, slo, shi = _rope_tables(batch, seq, dec_batch, dec_seq)
    q, k, v = _qkv(hn, w_in_b, cos, slo, shi)
    gates = _gates(hn, w_in_b)

    cw, cb, lg, lb = conv_w[0], vec(conv_b), vec(conv_ln_g), vec(conv_ln_b)
    sinks = attn_sinks[0]
    ck = cache_k[0].reshape(dec_batch, -1, KV_W)
    cv = cache_v[0].reshape(dec_batch, -1, KV_W)
    yc_s = _conv_sample(u, n_p, state_conv[0], cw, cb, lg, lb)
    o_s = _attn_sample(q, k, v, sinks, n_p, ck, cv)
    x2_p = _mix_prompt(u, q, k, v, gates, x1, sinks, cw, cb, lg, lb, w_co, w_ao, w_o, n_p, seq)
    x2_s = _post_sample(yc_s, o_s, gates, x1, w_co, w_ao, w_o, n_p)
    x3, = _ffn(x2_p, x2_s, vec(ffn2_norm), wgu2, wd2, vec(ffn2_norm), False, "ffn2")

    yp, ys = _ple(x3, p_prompt[0].reshape(n_p, D_PLE), p_sample[0].reshape(n_s, D_PLE),
                  vec(ple_norm), w_pg, w_pp, vec(final_norm))

    w_rows = ck.shape[1]

    def prompt_tail(a, rows):
        return jnp.stack([a[(b + 1) * seq - rows:(b + 1) * seq] for b in range(batch)])

    k_p = prompt_tail(k, WINDOW).reshape(batch, WINDOW, N_KV, HEAD_DIM)
    v_p = prompt_tail(v, WINDOW).reshape(batch, WINDOW, N_KV, HEAD_DIM)
    c_p = prompt_tail(u, CONV_WIDTH - 1)
    k_s = jnp.concatenate([cache_k[0], k[n_p:].reshape(dec_batch, dec_seq, N_KV, HEAD_DIM)], 1)[:, -w_rows:]
    v_s = jnp.concatenate([cache_v[0], v[n_p:].reshape(dec_batch, dec_seq, N_KV, HEAD_DIM)], 1)[:, -w_rows:]
    c_s = jnp.concatenate([state_conv[0], u[n_p:].reshape(dec_batch, dec_seq, D_CONV)], 1)[:, -(CONV_WIDTH - 1):]
    return (yp.reshape(batch, seq, D_MODEL), ys.reshape(dec_batch, dec_seq, D_MODEL),
            k_p[None], v_p[None], c_p[None], k_s[None], v_s[None], c_s[None])
```

```python
import functools

import jax
import jax.numpy as jnp
from jax import lax
from jax.experimental import pallas as pl
from jax.experimental.pallas import tpu as pltpu

D_MODEL = 2048
D_PLE = 256
D_FF = 5504
D_CONV = 1024
CONV_WIDTH = 31
HEAD_DIM = 64
N_HEADS = 16
N_KV = 4
GROUP = N_HEADS // N_KV
ROT_DIM = 16
ROPE_THETA = 500000.0
CHUNK = 64
WINDOW = 128
PAST_LEN = 1024
EPS = 1e-6
NEG = -1e30
LOG2E = 1.4426950408889634
Q_W = N_HEADS * HEAD_DIM
KV_W = N_KV * HEAD_DIM
IN_COLS = 2 * D_CONV + Q_W + 2 * KV_W + 2 * D_MODEL

LANES = 128
SUBLANES = 8
FF_TILE = 512
FF_STEPS = -(-D_FF // FF_TILE)
ROW_TILE = 512
IN_TILE = 1088
IN_COL_TILE = 1024
GLU_COL_TILE = 512
POST_TILE = 256
PLE_TILE = 512
CONV_HALO = 32
MIX_ROWS = 256
MIX_TILE = WINDOW
VMEM_LIMIT = 56 * 1024 * 1024

F32 = jnp.float32
BF16 = jnp.bfloat16


def _params(*sem):
    return pltpu.CompilerParams(dimension_semantics=sem, vmem_limit_bytes=VMEM_LIMIT)


def _rms(x, g):
    return x * lax.rsqrt(jnp.mean(x * x, axis=-1, keepdims=True) + EPS) * g


def _dot(a, b):
    return jnp.dot(a, b, preferred_element_type=F32)


def _resident(shape):
    return pl.BlockSpec(shape, lambda *_: (0,) * len(shape), pipeline_mode=pl.Buffered(1))


def _side_cast_specs(side, n_steps, step_of):
    in_specs, out_shape = [], []
    for a, rows in side:
        n_blocks = a.shape[0] // rows
        assert a.shape[0] % rows == 0 and n_blocks <= n_steps
        in_specs.append(pl.BlockSpec((rows, a.shape[1]),
                                     lambda *g, n=n_blocks: (jnp.minimum(step_of(*g), n - 1), 0)))
        out_shape.append(jax.ShapeDtypeStruct(a.shape, BF16))
    return in_specs, out_shape


def _side_cast(src_refs, dst_refs):
    for src, dst in zip(src_refs, dst_refs, strict=True):
        dst[...] = src[...].astype(BF16)


def _ffn_kernel(n_prompt_tiles, with_hn, n_side, xp_ref, xs_ref, g1_ref, wg_ref, wu_ref, wd_ref, g2_ref,
                *rest):
    side_in, rest = rest[:n_side], rest[n_side:]
    xo_ref = rest[0]
    hn_ref = rest[1] if with_hn else None
    side_out = rest[1 + with_hn:1 + with_hn + n_side]
    xn_ref = rest[-1]
    i, f = pl.program_id(0), pl.program_id(1)
    _side_cast(side_in, side_out)

    @pl.when(f == 0)
    def _():
        x = jnp.where(i < n_prompt_tiles, xp_ref[...], xs_ref[...])
        xn_ref[...] = _rms(x, g1_ref[...]).astype(BF16)
        xo_ref[...] = x

    def accumulate(cols):
        xn = xn_ref[...]
        g = _dot(xn, wg_ref[:, cols])
        u = _dot(xn, wu_ref[:, cols])
        h = (g * jax.nn.sigmoid(g)) * (u * 0.5)
        xo_ref[...] += _dot(h.astype(BF16), wd_ref[cols, :])

    pl.when(f < FF_STEPS - 1)(lambda: accumulate(slice(0, FF_TILE)))
    pl.when(f == FF_STEPS - 1)(lambda: accumulate(slice(FF_STEPS * FF_TILE - D_FF, FF_TILE)))

    if with_hn:
        @pl.when(f == FF_STEPS - 1)
        def _():
            hn_ref[...] = _rms(xo_ref[...], g2_ref[...]).astype(BF16)


def _ffn(xp, xs, g1, wgu, wd, g2, with_hn, name, side=()):
    n_p, n_s = xp.shape[0] // ROW_TILE, xs.shape[0] // ROW_TILE
    assert n_s == 1 and xs.shape[0] == ROW_TILE
    m = xp.shape[0] + xs.shape[0]

    def start(f):
        return pl.multiple_of(jnp.minimum(f * FF_TILE, D_FF - FF_TILE), LANES)

    cols = (pl.Element(D_MODEL), pl.Element(FF_TILE))
    rows = (pl.Element(FF_TILE), pl.Element(D_MODEL))
    vec = pl.BlockSpec((1, D_MODEL), lambda i, f: (0, 0))
    row = pl.BlockSpec((ROW_TILE, D_MODEL), lambda i, f: (i, 0))
    out_shape = [jax.ShapeDtypeStruct((m, D_MODEL), F32)]
    if with_hn:
        out_shape.append(jax.ShapeDtypeStruct((m, D_MODEL), BF16))
    n_main = len(out_shape)
    side_specs, side_shape = _side_cast_specs(side, (n_p + n_s) * FF_STEPS, lambda i, f: i * FF_STEPS + f)
    return pl.pallas_call(
        functools.partial(_ffn_kernel, n_p, with_hn, len(side)),
        grid=(n_p + n_s, FF_STEPS),
        in_specs=[pl.BlockSpec((ROW_TILE, D_MODEL), lambda i, f: (jnp.minimum(i, n_p - 1), 0)),
                  pl.BlockSpec((ROW_TILE, D_MODEL), lambda i, f: (0, 0)),
                  vec,
                  pl.BlockSpec(cols, lambda i, f: (0, start(f))),
                  pl.BlockSpec(cols, lambda i, f: (0, pl.multiple_of(D_FF + start(f), LANES))),
                  pl.BlockSpec(rows, lambda i, f: (start(f), 0)),
                  vec, *side_specs],
        out_specs=[row] * n_main + side_specs,
        out_shape=out_shape + side_shape,
        scratch_shapes=[pltpu.VMEM((ROW_TILE, D_MODEL), BF16)],
        compiler_params=_params("arbitrary", "arbitrary"),
        name=name,
    )(xp, xs, g1, wgu, wgu, wd, g2, *[a for a, _ in side])


def _glu_kernel(h_ref, wa_ref, wb_ref, o_ref):
    h = h_ref[...]
    o_ref[...] = _dot(h, wa_ref[...].astype(BF16)) * jax.nn.sigmoid(_dot(h, wb_ref[...].astype(BF16)))


def _glu(hn, w_in):
    m, tn = hn.shape[0], GLU_COL_TILE
    nj = D_CONV // tn
    return pl.pallas_call(
        _glu_kernel,
        grid=(m // IN_TILE, nj),
        in_specs=[pl.BlockSpec((IN_TILE, D_MODEL), lambda i, j: (i, 0)),
                  pl.BlockSpec((D_MODEL, tn), lambda i, j: (0, j)),
                  pl.BlockSpec((D_MODEL, tn), lambda i, j: (0, j + nj))],
        out_specs=pl.BlockSpec((IN_TILE, tn), lambda i, j: (i, j)),
        out_shape=jax.ShapeDtypeStruct((m, D_CONV), F32),
        compiler_params=_params("parallel", "arbitrary"),
        name="glu",
    )(hn, w_in, w_in)


def _gates_kernel(n_side, h_ref, w_ref, *rest):
    side_in, o_ref, side_out = rest[:n_side], rest[n_side], rest[n_side + 1:]
    _side_cast(side_in, side_out)
    o_ref[...] = jax.nn.sigmoid(_dot(h_ref[...], w_ref[...].astype(BF16))).astype(BF16)


def _gates(hn, w_in, side=()):
    m, tn = hn.shape[0], IN_COL_TILE
    col0 = 2 * D_CONV + Q_W + 2 * KV_W
    nj = 2 * D_MODEL // tn
    side_specs, side_shape = _side_cast_specs(side, m // IN_TILE * nj, lambda i, j: i * nj + j)
    return pl.pallas_call(
        functools.partial(_gates_kernel, len(side)),
        grid=(m // IN_TILE, nj),
        in_specs=[pl.BlockSpec((IN_TILE, D_MODEL), lambda i, j: (i, 0)),
                  pl.BlockSpec((pl.Element(D_MODEL), pl.Element(tn)),
                               lambda i, j: (0, pl.multiple_of(col0 + j * tn, LANES))),
                  *side_specs],
        out_specs=[pl.BlockSpec((IN_TILE, tn), lambda i, j: (i, j))] + side_specs,
        out_shape=[jax.ShapeDtypeStruct((m, 2 * D_MODEL), BF16)] + side_shape,
        compiler_params=_params("arbitrary", "arbitrary"),
        name="gates",
    )(hn, w_in, *[a for a, _ in side])


def _rotate(x, cos, sin_lo, sin_hi):
    cols = []
    for c in range(x.shape[1] // LANES):
        xb = x[:, c * LANES:(c + 1) * LANES]
        cols.append(xb * cos
                    + pltpu.roll(xb, LANES - ROT_DIM // 2, 1) * sin_lo
                    + pltpu.roll(xb, ROT_DIM // 2, 1) * sin_hi)
    return jnp.concatenate(cols, axis=1)


def _qkv_kernel(h_ref, wq_ref, wk_ref, wv_ref, cos_ref, slo_ref, shi_ref, q_ref, k_ref, v_ref):
    h = h_ref[...]
    cos, slo, shi = cos_ref[...], slo_ref[...], shi_ref[...]
    q = _rotate(_dot(h, wq_ref[...].astype(BF16)), cos, slo, shi)
    q_ref[...] = (q * (LOG2E * HEAD_DIM ** -0.5)).astype(BF16)
    k_ref[...] = _rotate(_dot(h, wk_ref[...].astype(BF16)), cos, slo, shi)
    v_ref[...] = _dot(h, wv_ref[...].astype(BF16))


def _qkv(hn, w_in, cos, slo, shi):
    m = hn.shape[0]
    c0 = 2 * D_CONV
    tab = pl.BlockSpec((IN_TILE, LANES), lambda i: (i, 0))
    return pl.pallas_call(
        _qkv_kernel,
        grid=(m // IN_TILE,),
        in_specs=[pl.BlockSpec((IN_TILE, D_MODEL), lambda i: (i, 0)),
                  pl.BlockSpec((D_MODEL, Q_W), lambda i: (0, c0 // Q_W), pipeline_mode=pl.Buffered(1)),
                  pl.BlockSpec((D_MODEL, KV_W), lambda i: (0, (c0 + Q_W) // KV_W), pipeline_mode=pl.Buffered(1)),
                  pl.BlockSpec((D_MODEL, KV_W), lambda i: (0, (c0 + Q_W) // KV_W + 1),
                               pipeline_mode=pl.Buffered(1)),
                  tab, tab, tab],
        out_specs=[pl.BlockSpec((IN_TILE, Q_W), lambda i: (i, 0)),
                   pl.BlockSpec((IN_TILE, KV_W), lambda i: (i, 0)),
                   pl.BlockSpec((IN_TILE, KV_W), lambda i: (i, 0))],
        out_shape=[jax.ShapeDtypeStruct((m, Q_W), BF16),
                   jax.ShapeDtypeStruct((m, KV_W), F32),
                   jax.ShapeDtypeStruct((m, KV_W), F32)],
        compiler_params=_params("parallel"),
        name="qkv",
    )(hn, w_in, w_in, w_in, cos, slo, shi)


def _rope_tables(batch, seq, dec_batch, dec_seq):
    inv = ROPE_THETA ** (-jnp.arange(0, ROT_DIM, 2, dtype=F32) / ROT_DIM)
    pos = jnp.concatenate([jnp.tile(jnp.arange(seq, dtype=jnp.int32), batch),
                           jnp.tile(PAST_LEN + jnp.arange(dec_seq, dtype=jnp.int32), dec_batch)])
    ang = pos.astype(F32)[:, None] * inv[None, :]
    cos, sin = jnp.cos(ang), jnp.sin(ang)
    one = jnp.ones((pos.shape[0], HEAD_DIM - ROT_DIM), F32)
    zero = jnp.zeros_like(one)
    zh = jnp.zeros_like(sin)
    c = jnp.concatenate([cos, cos, one], 1)
    lo = jnp.concatenate([-sin, zh, zero], 1)
    hi = jnp.concatenate([zh, sin, zero], 1)
    return [jnp.tile(t, (1, LANES // HEAD_DIM)) for t in (c, lo, hi)]


def _conv_tile(s_ref, row0, c_ref, w_ref, b_ref, lg_ref, lb_ref, rows):
    ext = rows + CONV_HALO - SUBLANES
    for s in range(1, SUBLANES):
        c_ref[s - 1] = s_ref[row0 + s:row0 + s + ext, :]
    base = CONV_HALO - (CONV_WIDTH - 1)
    cols = []
    for cb in range(D_CONV // LANES):
        lanes = slice(cb * LANES, (cb + 1) * LANES)
        acc = jnp.broadcast_to(b_ref[:, lanes], (rows, LANES))
        for j in range(CONV_WIDTH):
            a, s = divmod(base + j, SUBLANES)
            r0 = a * SUBLANES
            if s == 0:
                tap = s_ref[row0 + r0:row0 + r0 + rows, lanes]
            else:
                tap = c_ref[s - 1, r0:r0 + rows, lanes]
            acc = acc + tap * w_ref[j:j + 1, lanes]
        cols.append(acc)
    acc = jnp.concatenate(cols, axis=1)
    mu = jnp.mean(acc, axis=-1, keepdims=True)
    xc = acc - mu
    y = xc * lax.rsqrt(jnp.mean(xc * xc, axis=-1, keepdims=True) + EPS) * lg_ref[...] + lb_ref[...]
    return (y * jax.nn.sigmoid(y)).astype(BF16)


def _conv_sample_kernel(u_ref, past_ref, w_ref, b_ref, lg_ref, lb_ref, o_ref, s_ref, c_ref):
    s_ref[CONV_HALO - (CONV_WIDTH - 1):CONV_HALO, :] = past_ref[0]
    s_ref[CONV_HALO:, :] = u_ref[...]
    o_ref[...] = _conv_tile(s_ref, 0, c_ref, w_ref, b_ref, lg_ref, lb_ref, u_ref.shape[0])


def _conv_common_specs():
    return [pl.BlockSpec((CONV_WIDTH, D_CONV), lambda i: (0, 0)),
            pl.BlockSpec((1, D_CONV), lambda i: (0, 0)),
            pl.BlockSpec((1, D_CONV), lambda i: (0, 0)),
            pl.BlockSpec((1, D_CONV), lambda i: (0, 0))]


def _conv_scratch(block_rows, tile_rows):
    return [pltpu.VMEM((CONV_HALO + block_rows, D_CONV), F32),
            pltpu.VMEM((SUBLANES - 1, CONV_HALO + tile_rows - SUBLANES, D_CONV), F32)]


def _conv_sample(u, row0, state, w, b, lg, lb):
    n_seq, hist, _ = state.shape
    assert hist == CONV_WIDTH - 1
    rows = CHUNK
    blk0 = row0 // rows
    return pl.pallas_call(
        _conv_sample_kernel,
        grid=(n_seq,),
        in_specs=[pl.BlockSpec((rows, D_CONV), lambda i: (blk0 + i, 0)),
                  pl.BlockSpec((1, hist, D_CONV), lambda i: (i, 0, 0)),
                  *_conv_common_specs()],
        out_specs=pl.BlockSpec((rows, D_CONV), lambda i: (i, 0)),
        out_shape=jax.ShapeDtypeStruct((n_seq * rows, D_CONV), BF16),
        scratch_shapes=_conv_scratch(rows, rows),
        compiler_params=_params("parallel"),
        name="conv_sample",
    )(u, state, w, b, lg, lb)


def _attn_tile(q, k, v, sink_ref, valid):
    r = q.shape[0]
    lane_head = lax.broadcasted_iota(jnp.int32, (1, KV_W), 1) // HEAD_DIM
    sel_row = lax.broadcasted_iota(jnp.int32, (KV_W, Q_W), 0)
    sel_col = lax.broadcasted_iota(jnp.int32, (KV_W, Q_W), 1)
    sel = jnp.where(sel_row == sel_col // KV_W * HEAD_DIM + sel_col % HEAD_DIM, 1.0, 0.0).astype(BF16)
    krep = _dot(k, sel).astype(BF16)
    vrep = _dot(v, sel).astype(BF16)
    head_mask = [jnp.where(lane_head == g, 1.0, 0.0).astype(BF16) for g in range(GROUP)]
    scores = []
    for kv in range(N_KV):
        cols = slice(kv * KV_W, (kv + 1) * KV_W)
        lhs = jnp.concatenate([q[:, cols] * head_mask[g] for g in range(GROUP)], axis=0)
        scores.append(lax.dot_general(lhs, krep[:, cols], (((1,), (1,)), ((), ())),
                                      preferred_element_type=F32))
    weights, dens = [], []
    for kv in range(N_KV):
        es = []
        for g in range(GROUP):
            sg = scores[kv][g * r:(g + 1) * r]
            if valid is not None:
                sg = jnp.where(valid, sg, NEG)
            sink = sink_ref[kv * GROUP + g] * LOG2E
            mx = jnp.maximum(jnp.max(sg, axis=-1, keepdims=True), sink)
            e = jnp.exp2(sg - mx)
            dens.append(jnp.sum(e, axis=-1, keepdims=True) + jnp.exp2(sink - mx))
            es.append(e.astype(BF16))
        weights.append(jnp.concatenate(es, axis=0))
    outs = []
    for kv in range(N_KV):
        o4 = _dot(weights[kv], vrep[:, kv * KV_W:(kv + 1) * KV_W])
        og = [o4[g * r:(g + 1) * r] / dens[kv * GROUP + g] for g in range(GROUP)]
        oh = og[GROUP - 1]
        for g in range(GROUP - 2, -1, -1):
            oh = jnp.where(lane_head == g, og[g], oh)
        outs.append(oh)
    return jnp.concatenate(outs, axis=1).astype(BF16)


def _attn_sample_kernel(sink_ref, q_ref, kp_ref, kc_ref, vp_ref, vc_ref, o_ref):
    k = jnp.concatenate([kp_ref[0], kc_ref[...]], axis=0).astype(BF16)
    v = jnp.concatenate([vp_ref[0], vc_ref[...]], axis=0).astype(BF16)
    o_ref[...] = _attn_tile(q_ref[...], k, v, sink_ref, None)


def _attn_sample(q, k, v, sinks, row0, cache_k, cache_v):
    n_seq, w_rows, _ = cache_k.shape
    assert w_rows == WINDOW
    blk0 = row0 // CHUNK
    cache = pl.BlockSpec((1, w_rows, KV_W), lambda i: (i, 0, 0))
    cur = pl.BlockSpec((CHUNK, KV_W), lambda i: (blk0 + i, 0))
    return pl.pallas_call(
        _attn_sample_kernel,
        grid=(n_seq,),
        in_specs=[pl.BlockSpec(memory_space=pltpu.SMEM),
                  pl.BlockSpec((CHUNK, Q_W), lambda i: (blk0 + i, 0)), cache, cur, cache, cur],
        out_specs=pl.BlockSpec((CHUNK, Q_W), lambda i: (i, 0)),
        out_shape=jax.ShapeDtypeStruct((n_seq * CHUNK, Q_W), BF16),
        compiler_params=_params("parallel"),
        name="attn_sample",
    )(sinks, q, cache_k, k, cache_v, v)


def _merge_out(yc, o, gc, ga, x, wc_ref, wa_ref, wo_ref):
    m = gc.astype(F32) * _dot(yc, wc_ref[...]) + ga.astype(F32) * _dot(o, wa_ref[...])
    return x + _dot(m.astype(BF16), wo_ref[...])


def _mix_kernel(per_seq, sink_ref, u_ref, uh_ref, q_ref, kp_ref, kc_ref, vp_ref, vc_ref,
                cw_ref, cb_ref, lg_ref, lb_ref, gc_ref, ga_ref, x_ref, wc_ref, wa_ref, wo_ref,
                out_ref, s_ref, c_ref):
    first = pl.program_id(0) % per_seq == 0
    s_ref[0:CONV_HALO, :] = jnp.where(first, 0.0, uh_ref[...])
    s_ref[CONV_HALO:, :] = u_ref[...]

    k = jnp.concatenate([kp_ref[...], kc_ref[...]], axis=0).astype(BF16)
    v = jnp.concatenate([vp_ref[...], vc_ref[...]], axis=0).astype(BF16)
    nk = WINDOW + MIX_TILE
    col = lax.broadcasted_iota(jnp.int32, (MIX_TILE, nk), 1)
    d = col // CHUNK - lax.broadcasted_iota(jnp.int32, (MIX_TILE, nk), 0) // CHUNK
    band = (d >= 0) & (d <= WINDOW // CHUNK)

    for r0 in range(0, MIX_ROWS, MIX_TILE):
        rows = slice(r0, r0 + MIX_TILE)
        yc = _conv_tile(s_ref, r0, c_ref, cw_ref, cb_ref, lg_ref, lb_ref, MIX_TILE)
        valid = band & ((col >= WINDOW) | jnp.logical_not(first)) if r0 == 0 else band
        o = _attn_tile(q_ref[rows, :], k[r0:r0 + nk], v[r0:r0 + nk], sink_ref, valid)
        out_ref[rows, :] = _merge_out(yc, o, gc_ref[rows, :], ga_ref[rows, :], x_ref[rows, :],
                                      wc_ref, wa_ref, wo_ref)


def _mix_prompt(u, q, k, v, gates, x, sinks, cw, cb, lg, lb, wc, wa, wo, n_rows, seq):
    rows = MIX_ROWS
    halo_per_block = rows // CONV_HALO
    pre_per_block = rows // WINDOW
    kv_prev = pl.BlockSpec((WINDOW, KV_W), lambda i: (jnp.maximum(i * pre_per_block - 1, 0), 0))
    kv_cur = pl.BlockSpec((rows, KV_W), lambda i: (i, 0))
    return pl.pallas_call(
        functools.partial(_mix_kernel, seq // rows),
        grid=(n_rows // rows,),
        in_specs=[pl.BlockSpec(memory_space=pltpu.SMEM),
                  pl.BlockSpec((rows, D_CONV), lambda i: (i, 0)),
                  pl.BlockSpec((CONV_HALO, D_CONV), lambda i: (jnp.maximum(i * halo_per_block - 1, 0), 0)),
                  pl.BlockSpec((rows, Q_W), lambda i: (i, 0)),
                  kv_prev, kv_cur, kv_prev, kv_cur,
                  *_conv_common_specs(),
                  pl.BlockSpec((rows, D_MODEL), lambda i: (i, 0)),
                  pl.BlockSpec((rows, D_MODEL), lambda i: (i, 1)),
                  pl.BlockSpec((rows, D_MODEL), lambda i: (i, 0)),
                  _resident((D_CONV, D_MODEL)), _resident((Q_W, D_MODEL)), _resident((D_MODEL, D_MODEL))],
        out_specs=pl.BlockSpec((rows, D_MODEL), lambda i: (i, 0)),
        out_shape=jax.ShapeDtypeStruct((n_rows, D_MODEL), F32),
        scratch_shapes=_conv_scratch(rows, MIX_TILE),
        compiler_params=_params("parallel"),
        name="mix_prompt",
    )(sinks, u, u, q, k, k, v, v, cw, cb, lg, lb, gates, gates, x, wc, wa, wo)


def _post_sample_kernel(yc_ref, o_ref, gc_ref, ga_ref, x_ref, wc_ref, wa_ref, wo_ref, out_ref):
    out_ref[...] = _merge_out(yc_ref[...], o_ref[...], gc_ref[...], ga_ref[...], x_ref[...],
                              wc_ref, wa_ref, wo_ref)


def _post_sample(yc, o, gates, x, wc, wa, wo, row0):
    tm = POST_TILE
    blk0 = row0 // tm
    return pl.pallas_call(
        _post_sample_kernel,
        grid=(yc.shape[0] // tm,),
        in_specs=[pl.BlockSpec((tm, D_CONV), lambda i: (i, 0)),
                  pl.BlockSpec((tm, Q_W), lambda i: (i, 0)),
                  pl.BlockSpec((tm, D_MODEL), lambda i: (blk0 + i, 0)),
                  pl.BlockSpec((tm, D_MODEL), lambda i: (blk0 + i, 1)),
                  pl.BlockSpec((tm, D_MODEL), lambda i: (blk0 + i, 0)),
                  _resident((D_CONV, D_MODEL)), _resident((Q_W, D_MODEL)), _resident((D_MODEL, D_MODEL))],
        out_specs=pl.BlockSpec((tm, D_MODEL), lambda i: (i, 0)),
        out_shape=jax.ShapeDtypeStruct((yc.shape[0], D_MODEL), F32),
        compiler_params=_params("parallel"),
        name="post_sample",
    )(yc, o, gates, gates, x, wc, wa, wo)


def _ple_kernel(n_prompt_tiles, x_ref, pp_ref, ps_ref, gn_ref, wg_ref, wp_ref, fn_ref, yp_ref, ys_ref):
    i = pl.program_id(0)
    x = x_ref[...]
    gate = jax.nn.sigmoid(_dot(_rms(x, gn_ref[...]).astype(BF16), wg_ref[...]))
    pe = jnp.where(i < n_prompt_tiles, pp_ref[...], ps_ref[...]).astype(BF16)
    y = _rms(x + gate * _dot(pe, wp_ref[...]), fn_ref[...])

    @pl.when(i < n_prompt_tiles)
    def _():
        yp_ref[...] = y

    @pl.when(i >= n_prompt_tiles)
    def _():
        ys_ref[...] = y


def _ple(x, pp, ps, gn, wg, wp, fn):
    tm = PLE_TILE
    n_p, n_s = pp.shape[0] // tm, ps.shape[0] // tm
    vec = pl.BlockSpec((1, D_MODEL), lambda i: (0, 0))

    def p_idx(i):
        return (jnp.minimum(i, n_p - 1), 0)

    def s_idx(i):
        return (jnp.maximum(i - n_p, 0), 0)

    return pl.pallas_call(
        functools.partial(_ple_kernel, n_p),
        grid=(n_p + n_s,),
        in_specs=[pl.BlockSpec((tm, D_MODEL), lambda i: (i, 0)),
                  pl.BlockSpec((tm, D_PLE), p_idx),
                  pl.BlockSpec((tm, D_PLE), s_idx),
                  vec, _resident((D_MODEL, D_MODEL)), _resident((D_PLE, D_MODEL)), vec],
        out_specs=[pl.BlockSpec((tm, D_MODEL), p_idx), pl.BlockSpec((tm, D_MODEL), s_idx)],
        out_shape=[jax.ShapeDtypeStruct((pp.shape[0], D_MODEL), F32),
                   jax.ShapeDtypeStruct((ps.shape[0], D_MODEL), F32)],
        compiler_params=_params("arbitrary"),
        name="ple_final",
    )(x, pp, ps, gn, wg, wp, fn)


def kernel(x_prompt, x_sample, p_prompt, p_sample, state_conv, cache_k, cache_v, ffn1_norm, ffn1_w_gu, ffn1_w_down, mix_norm, w_in, conv_w, conv_b, conv_ln_g, conv_ln_b, conv_w_out, attn_sinks, attn_w_out, w_out, ffn2_norm, ffn2_w_gu, ffn2_w_down, ple_norm, ple_w_gate, ple_w_proj, final_norm):
    assert x_prompt.shape[-1] == D_MODEL and ffn1_norm.shape[0] == 1 and w_in.shape[-1] == IN_COLS
    batch, seq, _ = x_prompt.shape
    dec_batch, dec_seq, _ = x_sample.shape
    assert dec_seq == CHUNK and seq % ROW_TILE == 0
    n_p, n_s = batch * seq, dec_batch * dec_seq

    def vec(a):
        return a.reshape(1, -1)

    wgu1, wd1 = ffn1_w_gu[0].astype(BF16), ffn1_w_down[0].astype(BF16)

    xp = x_prompt.reshape(n_p, D_MODEL)
    xs = x_sample.reshape(n_s, D_MODEL)

    x1, hn, wgu2, wd2 = _ffn(xp, xs, vec(ffn1_norm), wgu1, wd1, vec(mix_norm), True, "ffn1",
                             side=[(ffn2_w_gu[0], 16), (ffn2_w_down[0], 32)])

    u = _glu(hn, w_in[0])
    cos, slo, shi = _rope_tables(batch, seq, dec_batch, dec_seq)
    q, k, v = _qkv(hn, w_in[0], cos, slo, shi)
    gates, w_co, w_ao, w_o, w_pg, w_pp = _gates(
        hn, w_in[0], side=[(conv_w_out[0], 32), (attn_w_out[0], 32), (w_out[0], 64), (ple_w_gate[0], 64),
                           (ple_w_proj[0], 16)])

    cw, cb, lg, lb = conv_w[0], vec(conv_b), vec(conv_ln_g), vec(conv_ln_b)
    sinks = attn_sinks[0]
    ck = cache_k[0].reshape(dec_batch, -1, KV_W)
    cv = cache_v[0].reshape(dec_batch, -1, KV_W)
    yc_s = _conv_sample(u, n_p, state_conv[0], cw, cb, lg, lb)
    o_s = _attn_sample(q, k, v, sinks, n_p, ck, cv)
    x2_p = _mix_prompt(u, q, k, v, gates, x1, sinks, cw, cb, lg, lb, w_co, w_ao, w_o, n_p, seq)
    x2_s = _post_sample(yc_s, o_s, gates, x1, w_co, w_ao, w_o, n_p)
    x3, = _ffn(x2_p, x2_s, vec(ffn2_norm), wgu2, wd2, vec(ffn2_norm), False, "ffn2")

    yp, ys = _ple(x3, p_prompt[0].reshape(n_p, D_PLE), p_sample[0].reshape(n_s, D_PLE),
                  vec(ple_norm), w_pg, w_pp, vec(final_norm))

    w_rows = ck.shape[1]

    def prompt_tail(a, rows):
        return jnp.stack([a[(b + 1) * seq - rows:(b + 1) * seq] for b in range(batch)])

    k_p = prompt_tail(k, WINDOW).reshape(batch, WINDOW, N_KV, HEAD_DIM)
    v_p = prompt_tail(v, WINDOW).reshape(batch, WINDOW, N_KV, HEAD_DIM)
    c_p = prompt_tail(u, CONV_WIDTH - 1)
    k_s = jnp.concatenate([cache_k[0], k[n_p:].reshape(dec_batch, dec_seq, N_KV, HEAD_DIM)], 1)[:, -w_rows:]
    v_s = jnp.concatenate([cache_v[0], v[n_p:].reshape(dec_batch, dec_seq, N_KV, HEAD_DIM)], 1)[:, -w_rows:]
    c_s = jnp.concatenate([state_conv[0], u[n_p:].reshape(dec_batch, dec_seq, D_CONV)], 1)[:, -(CONV_WIDTH - 1):]
    return (yp.reshape(batch, seq, D_MODEL), ys.reshape(dec_batch, dec_seq, D_MODEL),
            k_p[None], v_p[None], c_p[None], k_s[None], v_s[None], c_s[None])
```

```python
import functools

import jax
import jax.numpy as jnp
from jax import lax
from jax.experimental import pallas as pl
from jax.experimental.pallas import tpu as pltpu

D_MODEL = 2048
D_PLE = 256
D_FF = 5504
D_CONV = 1024
CONV_WIDTH = 31
HEAD_DIM = 64
N_HEADS = 16
N_KV = 4
GROUP = N_HEADS // N_KV
ROT_DIM = 16
ROPE_THETA = 500000.0
CHUNK = 64
WINDOW = 128
PAST_LEN = 1024
EPS = 1e-6
NEG = -1e30
LOG2E = 1.4426950408889634
Q_W = N_HEADS * HEAD_DIM
KV_W = N_KV * HEAD_DIM
IN_COLS = 2 * D_CONV + Q_W + 2 * KV_W + 2 * D_MODEL

LANES = 128
SUBLANES = 8
BF16_ROWS = 16
FF_TILE = 512
FF_STEPS = -(-D_FF // FF_TILE)
CAST_TILE = 256
WGU_CAST_ROWS = 16
WD_CAST_ROWS = 32
ROW_TILE = 512
IN_TILE = 1088
IN_COL_TILE = 1024
GLU_COL_TILE = 512
POST_TILE = 256
PLE_TILE = 512
CONV_HALO = 32
MIX_ROWS = 256
MIX_TILE = WINDOW
VMEM_LIMIT = 56 * 1024 * 1024

F32 = jnp.float32
BF16 = jnp.bfloat16


def _params(*sem):
    return pltpu.CompilerParams(dimension_semantics=sem, vmem_limit_bytes=VMEM_LIMIT)


def _rms(x, g):
    return x * lax.rsqrt(jnp.mean(x * x, axis=-1, keepdims=True) + EPS) * g


def _dot(a, b):
    return jnp.dot(a, b, preferred_element_type=F32)


def _resident(shape):
    return pl.BlockSpec(shape, lambda *_: (0,) * len(shape), pipeline_mode=pl.Buffered(1))


def _side_cast_specs(side, n_steps, step_of):
    in_specs, out_shape = [], []
    for a, rows in side:
        n_blocks = a.shape[0] // rows
        assert a.shape[0] % rows == 0 and n_blocks <= n_steps
        in_specs.append(pl.BlockSpec((rows, a.shape[1]),
                                     lambda *g, n=n_blocks: (jnp.minimum(step_of(*g), n - 1), 0)))
        out_shape.append(jax.ShapeDtypeStruct(a.shape, BF16))
    return in_specs, out_shape


def _side_cast(src_refs, dst_refs):
    for src, dst in zip(src_refs, dst_refs, strict=True):
        dst[...] = src[...].astype(BF16)


def _swiglu_accumulate(xn_ref, xo_ref, wg, wu, wd):
    xn = xn_ref[...]
    g = _dot(xn, wg)
    u = _dot(xn, wu)
    h = (g * jax.nn.sigmoid(g)) * (u * 0.5)
    xo_ref[...] += _dot(h.astype(BF16), wd)


def _ff_steps(tile):
    return -(-D_FF // tile)


def _ff_start(f, tile):
    return pl.multiple_of(jnp.minimum(f * tile, D_FF - tile), LANES)


def _ffn_first_kernel(with_hn, x_ref, g1_ref, wg_ref, wu_ref, wd_ref, g2_ref, xo_ref, *rest):
    hn_ref = rest[0] if with_hn else None
    wgb_ref, wub_ref, wdb_ref, xn_ref = rest[with_hn:]
    f = pl.program_id(0)
    steps = _ff_steps(CAST_TILE)
    wg, wu, wd = wg_ref[...].astype(BF16), wu_ref[...].astype(BF16), wd_ref[...].astype(BF16)

    @pl.when(f == 0)
    def _():
        x = x_ref[...]
        xn_ref[...] = _rms(x, g1_ref[...]).astype(BF16)
        xo_ref[...] = x

    @pl.when(f < steps - 1)
    def _():
        wgb_ref[...] = wg
        wub_ref[...] = wu
        wdb_ref[...] = wd
        _swiglu_accumulate(xn_ref, xo_ref, wg, wu, wd)

    @pl.when(f == steps - 1)
    def _():
        old = steps * CAST_TILE - D_FF
        wg_new, wu_new, wd_new = wg[:, old:], wu[:, old:], wd[old:, :]
        wgb_ref[...] = jnp.concatenate([wg_new, jnp.zeros((D_MODEL, old), BF16)], axis=1)
        wub_ref[...] = jnp.concatenate([wu_new, jnp.zeros((D_MODEL, old), BF16)], axis=1)
        wdb_ref[...] = jnp.concatenate([wd_new, jnp.zeros((old, D_MODEL), BF16)], axis=0)
        _swiglu_accumulate(xn_ref, xo_ref, wg_new, wu_new, wd_new)

    if with_hn:
        @pl.when(f == steps - 1)
        def _():
            hn_ref[...] = _rms(xo_ref[...], g2_ref[...]).astype(BF16)


def _ffn_first(x, g1, w_gu, w_down, g2, with_hn, name):
    assert x.shape[0] == ROW_TILE
    tile = CAST_TILE
    cols = (pl.Element(D_MODEL), pl.Element(tile))
    rows = (pl.Element(tile), pl.Element(D_MODEL))
    vec = pl.BlockSpec((1, D_MODEL), lambda f: (0, 0))
    row = pl.BlockSpec((ROW_TILE, D_MODEL), lambda f: (0, 0))
    gate_win = pl.BlockSpec(cols, lambda f: (0, _ff_start(f, tile)))
    up_win = pl.BlockSpec(cols, lambda f: (0, pl.multiple_of(D_FF + _ff_start(f, tile), LANES)))
    down_win = pl.BlockSpec(rows, lambda f: (_ff_start(f, tile), 0))
    out_shape = [jax.ShapeDtypeStruct((ROW_TILE, D_MODEL), F32)]
    if with_hn:
        out_shape.append(jax.ShapeDtypeStruct((ROW_TILE, D_MODEL), BF16))
    steps = _ff_steps(tile)
    weights_shape = ([jax.ShapeDtypeStruct((D_MODEL, steps * tile), BF16)] * 2
                     + [jax.ShapeDtypeStruct((steps * tile, D_MODEL), BF16)])
    col_blk = pl.BlockSpec((D_MODEL, tile), lambda f: (0, f))
    row_blk = pl.BlockSpec((tile, D_MODEL), lambda f: (f, 0))
    return pl.pallas_call(
        functools.partial(_ffn_first_kernel, with_hn),
        grid=(steps,),
        in_specs=[row, vec, gate_win, up_win, down_win, vec],
        out_specs=[row] * len(out_shape) + [col_blk, col_blk, row_blk],
        out_shape=out_shape + weights_shape,
        scratch_shapes=[pltpu.VMEM((ROW_TILE, D_MODEL), BF16)],
        compiler_params=_params("arbitrary"),
        name=name,
    )(x, g1, w_gu, w_gu, w_down, g2)


def _ffn_kernel(n_prompt_tiles, with_hn, tail_done, n_side, xp_ref, xs_ref, *rest):
    if tail_done:
        hns_ref, rest = rest[0], rest[1:]
    g1_ref, wg_ref, wu_ref, wd_ref, g2_ref = rest[:5]
    side_in, outs = rest[5:5 + n_side], rest[5 + n_side:]
    xo_ref = outs[0]
    hn_ref = outs[1] if with_hn else None
    side_out = outs[1 + with_hn:1 + with_hn + n_side]
    xn_ref = outs[-1]
    i, f = pl.program_id(0), pl.program_id(1)
    _side_cast(side_in, side_out)

    def on(cond):
        return jnp.logical_and(cond, i < n_prompt_tiles) if tail_done else cond

    @pl.when(on(f == 0))
    def _():
        x = xp_ref[...] if tail_done else jnp.where(i < n_prompt_tiles, xp_ref[...], xs_ref[...])
        xn_ref[...] = _rms(x, g1_ref[...]).astype(BF16)
        xo_ref[...] = x

    if tail_done:
        @pl.when(jnp.logical_and(f == 0, i >= n_prompt_tiles))
        def _():
            xo_ref[...] = xs_ref[...]
            hn_ref[...] = hns_ref[...]

    old = FF_STEPS * FF_TILE - D_FF
    pl.when(on(f < FF_STEPS - 1))(
        lambda: _swiglu_accumulate(xn_ref, xo_ref, wg_ref[...], wu_ref[...], wd_ref[...]))
    pl.when(on(f == FF_STEPS - 1))(
        lambda: _swiglu_accumulate(xn_ref, xo_ref, wg_ref[:, old:], wu_ref[:, old:], wd_ref[old:, :]))

    if with_hn:
        @pl.when(on(f == FF_STEPS - 1))
        def _():
            hn_ref[...] = _rms(xo_ref[...], g2_ref[...]).astype(BF16)


def _ffn(xp, xs, hn_s, g1, wg, wu, wd, g2, with_hn, name, side=()):
    n_p, n_s = xp.shape[0] // ROW_TILE, xs.shape[0] // ROW_TILE
    assert n_s == 1 and xs.shape[0] == ROW_TILE
    m = xp.shape[0] + xs.shape[0]
    tail_done = hn_s is not None
    assert with_hn or not tail_done

    def start(i, f):
        return _ff_start(jnp.where(i < n_p, f, FF_STEPS - 1) if tail_done else f, FF_TILE)

    cols = (pl.Element(D_MODEL), pl.Element(FF_TILE))
    rows = (pl.Element(FF_TILE), pl.Element(D_MODEL))
    vec = pl.BlockSpec((1, D_MODEL), lambda i, f: (0, 0))
    row = pl.BlockSpec((ROW_TILE, D_MODEL), lambda i, f: (i, 0))
    if tail_done:
        tail_specs, tail_args = [_resident((ROW_TILE, D_MODEL))] * 2, [xs, hn_s]
    else:
        tail_specs, tail_args = [pl.BlockSpec((ROW_TILE, D_MODEL), lambda i, f: (0, 0))], [xs]
    out_shape = [jax.ShapeDtypeStruct((m, D_MODEL), F32)]
    if with_hn:
        out_shape.append(jax.ShapeDtypeStruct((m, D_MODEL), BF16))
    side_specs, side_shape = _side_cast_specs(side, (n_p + n_s) * FF_STEPS, lambda i, f: i * FF_STEPS + f)
    return pl.pallas_call(
        functools.partial(_ffn_kernel, n_p, with_hn, tail_done, len(side)),
        grid=(n_p + n_s, FF_STEPS),
        in_specs=[pl.BlockSpec((ROW_TILE, D_MODEL), lambda i, f: (jnp.minimum(i, n_p - 1), 0)),
                  *tail_specs, vec,
                  pl.BlockSpec(cols, lambda i, f: (0, pl.multiple_of(wg[1] + start(i, f), LANES))),
                  pl.BlockSpec(cols, lambda i, f: (0, pl.multiple_of(wu[1] + start(i, f), LANES))),
                  pl.BlockSpec(rows, lambda i, f: (start(i, f), 0)),
                  vec, *side_specs],
        out_specs=[row] * len(out_shape) + side_specs,
        out_shape=out_shape + side_shape,
        scratch_shapes=[pltpu.VMEM((ROW_TILE, D_MODEL), BF16)],
        compiler_params=_params("arbitrary", "arbitrary"),
        name=name,
    )(xp, *tail_args, g1, wg[0], wu[0], wd, g2, *[a for a, _ in side])


def _glu_kernel(h_ref, wa_ref, wb_ref, o_ref):
    h = h_ref[...]
    o_ref[...] = _dot(h, wa_ref[...].astype(BF16)) * jax.nn.sigmoid(_dot(h, wb_ref[...].astype(BF16)))


def _glu(hn, w_in):
    m, tn = hn.shape[0], GLU_COL_TILE
    nj = D_CONV // tn
    return pl.pallas_call(
        _glu_kernel,
        grid=(m // IN_TILE, nj),
        in_specs=[pl.BlockSpec((IN_TILE, D_MODEL), lambda i, j: (i, 0)),
                  pl.BlockSpec((D_MODEL, tn), lambda i, j: (0, j)),
                  pl.BlockSpec((D_MODEL, tn), lambda i, j: (0, j + nj))],
        out_specs=pl.BlockSpec((IN_TILE, tn), lambda i, j: (i, j)),
        out_shape=jax.ShapeDtypeStruct((m, D_CONV), F32),
        compiler_params=_params("parallel", "arbitrary"),
        name="glu",
    )(hn, w_in, w_in)


def _gates_kernel(n_side, h_ref, w_ref, *rest):
    side_in, o_ref, side_out = rest[:n_side], rest[n_side], rest[n_side + 1:]
    _side_cast(side_in, side_out)
    o_ref[...] = jax.nn.sigmoid(_dot(h_ref[...], w_ref[...].astype(BF16))).astype(BF16)


def _gates(hn, w_in, side=()):
    m, tn = hn.shape[0], IN_COL_TILE
    col0 = 2 * D_CONV + Q_W + 2 * KV_W
    nj = 2 * D_MODEL // tn
    side_specs, side_shape = _side_cast_specs(side, m // IN_TILE * nj, lambda i, j: i * nj + j)
    return pl.pallas_call(
        functools.partial(_gates_kernel, len(side)),
        grid=(m // IN_TILE, nj),
        in_specs=[pl.BlockSpec((IN_TILE, D_MODEL), lambda i, j: (i, 0)),
                  pl.BlockSpec((pl.Element(D_MODEL), pl.Element(tn)),
                               lambda i, j: (0, pl.multiple_of(col0 + j * tn, LANES))),
                  *side_specs],
        out_specs=[pl.BlockSpec((IN_TILE, tn), lambda i, j: (i, j))] + side_specs,
        out_shape=[jax.ShapeDtypeStruct((m, 2 * D_MODEL), BF16)] + side_shape,
        compiler_params=_params("arbitrary", "arbitrary"),
        name="gates",
    )(hn, w_in, *[a for a, _ in side])


def _rotate(x, cos, sin_lo, sin_hi):
    cols = []
    for c in range(x.shape[1] // LANES):
        xb = x[:, c * LANES:(c + 1) * LANES]
        cols.append(xb * cos
                    + pltpu.roll(xb, LANES - ROT_DIM // 2, 1) * sin_lo
                    + pltpu.roll(xb, ROT_DIM // 2, 1) * sin_hi)
    return jnp.concatenate(cols, axis=1)


def _qkv_kernel(h_ref, wq_ref, wk_ref, wv_ref, cos_ref, slo_ref, shi_ref, q_ref, k_ref, v_ref):
    h = h_ref[...]
    cos, slo, shi = cos_ref[...], slo_ref[...], shi_ref[...]
    q = _rotate(_dot(h, wq_ref[...].astype(BF16)), cos, slo, shi)
    q_ref[...] = (q * (LOG2E * HEAD_DIM ** -0.5)).astype(BF16)
    k_ref[...] = _rotate(_dot(h, wk_ref[...].astype(BF16)), cos, slo, shi)
    v_ref[...] = _dot(h, wv_ref[...].astype(BF16))


def _qkv(hn, w_in, cos, slo, shi):
    m = hn.shape[0]
    c0 = 2 * D_CONV
    tab = pl.BlockSpec((IN_TILE, LANES), lambda i: (i, 0))
    return pl.pallas_call(
        _qkv_kernel,
        grid=(m // IN_TILE,),
        in_specs=[pl.BlockSpec((IN_TILE, D_MODEL), lambda i: (i, 0)),
                  pl.BlockSpec((D_MODEL, Q_W), lambda i: (0, c0 // Q_W), pipeline_mode=pl.Buffered(1)),
                  pl.BlockSpec((D_MODEL, KV_W), lambda i: (0, (c0 + Q_W) // KV_W), pipeline_mode=pl.Buffered(1)),
                  pl.BlockSpec((D_MODEL, KV_W), lambda i: (0, (c0 + Q_W) // KV_W + 1),
                               pipeline_mode=pl.Buffered(1)),
                  tab, tab, tab],
        out_specs=[pl.BlockSpec((IN_TILE, Q_W), lambda i: (i, 0)),
                   pl.BlockSpec((IN_TILE, KV_W), lambda i: (i, 0)),
                   pl.BlockSpec((IN_TILE, KV_W), lambda i: (i, 0))],
        out_shape=[jax.ShapeDtypeStruct((m, Q_W), BF16),
                   jax.ShapeDtypeStruct((m, KV_W), F32),
                   jax.ShapeDtypeStruct((m, KV_W), F32)],
        compiler_params=_params("parallel"),
        name="qkv",
    )(hn, w_in, w_in, w_in, cos, slo, shi)


def _rope_tables(batch, seq, dec_batch, dec_seq):
    inv = ROPE_THETA ** (-jnp.arange(0, ROT_DIM, 2, dtype=F32) / ROT_DIM)
    pos = jnp.concatenate([jnp.tile(jnp.arange(seq, dtype=jnp.int32), batch),
                           jnp.tile(PAST_LEN + jnp.arange(dec_seq, dtype=jnp.int32), dec_batch)])
    ang = pos.astype(F32)[:, None] * inv[None, :]
    cos, sin = jnp.cos(ang), jnp.sin(ang)
    one = jnp.ones((pos.shape[0], HEAD_DIM - ROT_DIM), F32)
    zero = jnp.zeros_like(one)
    zh = jnp.zeros_like(sin)
    c = jnp.concatenate([cos, cos, one], 1)
    lo = jnp.concatenate([-sin, zh, zero], 1)
    hi = jnp.concatenate([zh, sin, zero], 1)
    return [jnp.tile(t, (1, LANES // HEAD_DIM)) for t in (c, lo, hi)]


def _conv_tile(s_ref, row0, c_ref, w_ref, b_ref, lg_ref, lb_ref, rows):
    ext = rows + CONV_HALO - SUBLANES
    for s in range(1, SUBLANES):
        c_ref[s - 1] = s_ref[row0 + s:row0 + s + ext, :]
    base = CONV_HALO - (CONV_WIDTH - 1)
    cols = []
    for cb in range(D_CONV // LANES):
        lanes = slice(cb * LANES, (cb + 1) * LANES)
        acc = jnp.broadcast_to(b_ref[:, lanes], (rows, LANES))
        for j in range(CONV_WIDTH):
            a, s = divmod(base + j, SUBLANES)
            r0 = a * SUBLANES
            if s == 0:
                tap = s_ref[row0 + r0:row0 + r0 + rows, lanes]
            else:
                tap = c_ref[s - 1, r0:r0 + rows, lanes]
            acc = acc + tap * w_ref[j:j + 1, lanes]
        cols.append(acc)
    acc = jnp.concatenate(cols, axis=1)
    mu = jnp.mean(acc, axis=-1, keepdims=True)
    xc = acc - mu
    y = xc * lax.rsqrt(jnp.mean(xc * xc, axis=-1, keepdims=True) + EPS) * lg_ref[...] + lb_ref[...]
    return (y * jax.nn.sigmoid(y)).astype(BF16)


def _conv_sample_kernel(u_ref, past_ref, w_ref, b_ref, lg_ref, lb_ref, o_ref, s_ref, c_ref):
    s_ref[CONV_HALO - (CONV_WIDTH - 1):CONV_HALO, :] = past_ref[0]
    s_ref[CONV_HALO:, :] = u_ref[...]
    o_ref[...] = _conv_tile(s_ref, 0, c_ref, w_ref, b_ref, lg_ref, lb_ref, u_ref.shape[0])


def _conv_common_specs():
    return [pl.BlockSpec((CONV_WIDTH, D_CONV), lambda i: (0, 0)),
            pl.BlockSpec((1, D_CONV), lambda i: (0, 0)),
            pl.BlockSpec((1, D_CONV), lambda i: (0, 0)),
            pl.BlockSpec((1, D_CONV), lambda i: (0, 0))]


def _conv_scratch(block_rows, tile_rows):
    return [pltpu.VMEM((CONV_HALO + block_rows, D_CONV), F32),
            pltpu.VMEM((SUBLANES - 1, CONV_HALO + tile_rows - SUBLANES, D_CONV), F32)]


def _conv_sample(u, row0, state, w, b, lg, lb):
    n_seq, hist, _ = state.shape
    assert hist == CONV_WIDTH - 1
    rows = CHUNK
    blk0 = row0 // rows
    return pl.pallas_call(
        _conv_sample_kernel,
        grid=(n_seq,),
        in_specs=[pl.BlockSpec((rows, D_CONV), lambda i: (blk0 + i, 0)),
                  pl.BlockSpec((1, hist, D_CONV), lambda i: (i, 0, 0)),
                  *_conv_common_specs()],
        out_specs=pl.BlockSpec((rows, D_CONV), lambda i: (i, 0)),
        out_shape=jax.ShapeDtypeStruct((n_seq * rows, D_CONV), BF16),
        scratch_shapes=_conv_scratch(rows, rows),
        compiler_params=_params("parallel"),
        name="conv_sample",
    )(u, state, w, b, lg, lb)


def _attn_tile(q, k, v, sink_ref, valid):
    r = q.shape[0]
    lane_head = lax.broadcasted_iota(jnp.int32, (1, KV_W), 1) // HEAD_DIM
    sel_row = lax.broadcasted_iota(jnp.int32, (KV_W, Q_W), 0)
    sel_col = lax.broadcasted_iota(jnp.int32, (KV_W, Q_W), 1)
    sel = jnp.where(sel_row == sel_col // KV_W * HEAD_DIM + sel_col % HEAD_DIM, 1.0, 0.0).astype(BF16)
    krep = _dot(k, sel).astype(BF16)
    vrep = _dot(v, sel).astype(BF16)
    head_mask = [jnp.where(lane_head == g, 1.0, 0.0).astype(BF16) for g in range(GROUP)]
    scores = []
    for kv in range(N_KV):
        cols = slice(kv * KV_W, (kv + 1) * KV_W)
        lhs = jnp.concatenate([q[:, cols] * head_mask[g] for g in range(GROUP)], axis=0)
        scores.append(lax.dot_general(lhs, krep[:, cols], (((1,), (1,)), ((), ())),
                                      preferred_element_type=F32))
    weights, dens = [], []
    for kv in range(N_KV):
        es = []
        for g in range(GROUP):
            sg = scores[kv][g * r:(g + 1) * r]
            if valid is not None:
                sg = jnp.where(valid, sg, NEG)
            sink = sink_ref[kv * GROUP + g] * LOG2E
            mx = jnp.maximum(jnp.max(sg, axis=-1, keepdims=True), sink)
            e = jnp.exp2(sg - mx)
            dens.append(jnp.sum(e, axis=-1, keepdims=True) + jnp.exp2(sink - mx))
            es.append(e.astype(BF16))
        weights.append(jnp.concatenate(es, axis=0))
    outs = []
    for kv in range(N_KV):
        o4 = _dot(weights[kv], vrep[:, kv * KV_W:(kv + 1) * KV_W])
        og = [o4[g * r:(g + 1) * r] / dens[kv * GROUP + g] for g in range(GROUP)]
        oh = og[GROUP - 1]
        for g in range(GROUP - 2, -1, -1):
            oh = jnp.where(lane_head == g, og[g], oh)
        outs.append(oh)
    return jnp.concatenate(outs, axis=1).astype(BF16)


def _attn_sample_kernel(sink_ref, q_ref, kp_ref, kc_ref, vp_ref, vc_ref, o_ref):
    k = jnp.concatenate([kp_ref[0], kc_ref[...]], axis=0).astype(BF16)
    v = jnp.concatenate([vp_ref[0], vc_ref[...]], axis=0).astype(BF16)
    o_ref[...] = _attn_tile(q_ref[...], k, v, sink_ref, None)


def _attn_sample(q, k, v, sinks, row0, cache_k, cache_v):
    n_seq, w_rows, _ = cache_k.shape
    assert w_rows == WINDOW
    blk0 = row0 // CHUNK
    cache = pl.BlockSpec((1, w_rows, KV_W), lambda i: (i, 0, 0))
    cur = pl.BlockSpec((CHUNK, KV_W), lambda i: (blk0 + i, 0))
    return pl.pallas_call(
        _attn_sample_kernel,
        grid=(n_seq,),
        in_specs=[pl.BlockSpec(memory_space=pltpu.SMEM),
                  pl.BlockSpec((CHUNK, Q_W), lambda i: (blk0 + i, 0)), cache, cur, cache, cur],
        out_specs=pl.BlockSpec((CHUNK, Q_W), lambda i: (i, 0)),
        out_shape=jax.ShapeDtypeStruct((n_seq * CHUNK, Q_W), BF16),
        compiler_params=_params("parallel"),
        name="attn_sample",
    )(sinks, q, cache_k, k, cache_v, v)


def _merge_out(yc, o, gc, ga, x, wc_ref, wa_ref, wo_ref):
    m = gc.astype(F32) * _dot(yc, wc_ref[...]) + ga.astype(F32) * _dot(o, wa_ref[...])
    return x + _dot(m.astype(BF16), wo_ref[...])


def _mix_kernel(per_seq, sink_ref, u_ref, uh_ref, q_ref, kp_ref, kc_ref, vp_ref, vc_ref,
                cw_ref, cb_ref, lg_ref, lb_ref, gc_ref, ga_ref, x_ref, wc_ref, wa_ref, wo_ref,
                out_ref, s_ref, c_ref):
    first = pl.program_id(0) % per_seq == 0
    s_ref[0:CONV_HALO, :] = jnp.where(first, 0.0, uh_ref[...])
    s_ref[CONV_HALO:, :] = u_ref[...]

    k = jnp.concatenate([kp_ref[...], kc_ref[...]], axis=0).astype(BF16)
    v = jnp.concatenate([vp_ref[...], vc_ref[...]], axis=0).astype(BF16)
    nk = WINDOW + MIX_TILE
    col = lax.broadcasted_iota(jnp.int32, (MIX_TILE, nk), 1)
    d = col // CHUNK - lax.broadcasted_iota(jnp.int32, (MIX_TILE, nk), 0) // CHUNK
    band = (d >= 0) & (d <= WINDOW // CHUNK)

    for r0 in range(0, MIX_ROWS, MIX_TILE):
        rows = slice(r0, r0 + MIX_TILE)
        yc = _conv_tile(s_ref, r0, c_ref, cw_ref, cb_ref, lg_ref, lb_ref, MIX_TILE)
        valid = band & ((col >= WINDOW) | jnp.logical_not(first)) if r0 == 0 else band
        o = _attn_tile(q_ref[rows, :], k[r0:r0 + nk], v[r0:r0 + nk], sink_ref, valid)
        out_ref[rows, :] = _merge_out(yc, o, gc_ref[rows, :], ga_ref[rows, :], x_ref[rows, :],
                                      wc_ref, wa_ref, wo_ref)


def _mix_prompt(u, q, k, v, gates, x, sinks, cw, cb, lg, lb, wc, wa, wo, n_rows, seq):
    rows = MIX_ROWS
    halo_per_block = rows // CONV_HALO
    pre_per_block = rows // WINDOW
    kv_prev = pl.BlockSpec((WINDOW, KV_W), lambda i: (jnp.maximum(i * pre_per_block - 1, 0), 0))
    kv_cur = pl.BlockSpec((rows, KV_W), lambda i: (i, 0))
    return pl.pallas_call(
        functools.partial(_mix_kernel, seq // rows),
        grid=(n_rows // rows,),
        in_specs=[pl.BlockSpec(memory_space=pltpu.SMEM),
                  pl.BlockSpec((rows, D_CONV), lambda i: (i, 0)),
                  pl.BlockSpec((CONV_HALO, D_CONV), lambda i: (jnp.maximum(i * halo_per_block - 1, 0), 0)),
                  pl.BlockSpec((rows, Q_W), lambda i: (i, 0)),
                  kv_prev, kv_cur, kv_prev, kv_cur,
                  *_conv_common_specs(),
                  pl.BlockSpec((rows, D_MODEL), lambda i: (i, 0)),
                  pl.BlockSpec((rows, D_MODEL), lambda i: (i, 1)),
                  pl.BlockSpec((rows, D_MODEL), lambda i: (i, 0)),
                  _resident((D_CONV, D_MODEL)), _resident((Q_W, D_MODEL)), _resident((D_MODEL, D_MODEL))],
        out_specs=pl.BlockSpec((rows, D_MODEL), lambda i: (i, 0)),
        out_shape=jax.ShapeDtypeStruct((n_rows, D_MODEL), F32),
        scratch_shapes=_conv_scratch(rows, MIX_TILE),
        compiler_params=_params("parallel"),
        name="mix_prompt",
    )(sinks, u, u, q, k, k, v, v, cw, cb, lg, lb, gates, gates, x, wc, wa, wo)


def _post_sample_kernel(yc_ref, o_ref, gc_ref, ga_ref, x_ref, wc_ref, wa_ref, wo_ref, out_ref):
    out_ref[...] = _merge_out(yc_ref[...], o_ref[...], gc_ref[...], ga_ref[...], x_ref[...],
                              wc_ref, wa_ref, wo_ref)


def _post_sample(yc, o, gates, x, wc, wa, wo, row0):
    tm = POST_TILE
    blk0 = row0 // tm
    return pl.pallas_call(
        _post_sample_kernel,
        grid=(yc.shape[0] // tm,),
        in_specs=[pl.BlockSpec((tm, D_CONV), lambda i: (i, 0)),
                  pl.BlockSpec((tm, Q_W), lambda i: (i, 0)),
                  pl.BlockSpec((tm, D_MODEL), lambda i: (blk0 + i, 0)),
                  pl.BlockSpec((tm, D_MODEL), lambda i: (blk0 + i, 1)),
                  pl.BlockSpec((tm, D_MODEL), lambda i: (blk0 + i, 0)),
                  _resident((D_CONV, D_MODEL)), _resident((Q_W, D_MODEL)), _resident((D_MODEL, D_MODEL))],
        out_specs=pl.BlockSpec((tm, D_MODEL), lambda i: (i, 0)),
        out_shape=jax.ShapeDtypeStruct((yc.shape[0], D_MODEL), F32),
        compiler_params=_params("parallel"),
        name="post_sample",
    )(yc, o, gates, gates, x, wc, wa, wo)


def _ple_kernel(n_prompt_tiles, x_ref, pp_ref, ps_ref, gn_ref, wg_ref, wp_ref, fn_ref, yp_ref, ys_ref):
    i = pl.program_id(0)
    x = x_ref[...]
    gate = jax.nn.sigmoid(_dot(_rms(x, gn_ref[...]).astype(BF16), wg_ref[...]))
    pe = jnp.where(i < n_prompt_tiles, pp_ref[...], ps_ref[...]).astype(BF16)
    y = _rms(x + gate * _dot(pe, wp_ref[...]), fn_ref[...])

    @pl.when(i < n_prompt_tiles)
    def _():
        yp_ref[...] = y

    @pl.when(i >= n_prompt_tiles)
    def _():
        ys_ref[...] = y


def _ple(x, pp, ps, gn, wg, wp, fn):
    tm = PLE_TILE
    n_p, n_s = pp.shape[0] // tm, ps.shape[0] // tm
    vec = pl.BlockSpec((1, D_MODEL), lambda i: (0, 0))

    def p_idx(i):
        return (jnp.minimum(i, n_p - 1), 0)

    def s_idx(i):
        return (jnp.maximum(i - n_p, 0), 0)

    return pl.pallas_call(
        functools.partial(_ple_kernel, n_p),
        grid=(n_p + n_s,),
        in_specs=[pl.BlockSpec((tm, D_MODEL), lambda i: (i, 0)),
                  pl.BlockSpec((tm, D_PLE), p_idx),
                  pl.BlockSpec((tm, D_PLE), s_idx),
                  vec, _resident((D_MODEL, D_MODEL)), _resident((D_PLE, D_MODEL)), vec],
        out_specs=[pl.BlockSpec((tm, D_MODEL), p_idx), pl.BlockSpec((tm, D_MODEL), s_idx)],
        out_shape=[jax.ShapeDtypeStruct((pp.shape[0], D_MODEL), F32),
                   jax.ShapeDtypeStruct((ps.shape[0], D_MODEL), F32)],
        compiler_params=_params("arbitrary"),
        name="ple_final",
    )(x, pp, ps, gn, wg, wp, fn)


def kernel(x_prompt, x_sample, p_prompt, p_sample, state_conv, cache_k, cache_v, ffn1_norm, ffn1_w_gu, ffn1_w_down, mix_norm, w_in, conv_w, conv_b, conv_ln_g, conv_ln_b, conv_w_out, attn_sinks, attn_w_out, w_out, ffn2_norm, ffn2_w_gu, ffn2_w_down, ple_norm, ple_w_gate, ple_w_proj, final_norm):
    assert x_prompt.shape[-1] == D_MODEL and ffn1_norm.shape[0] == 1 and w_in.shape[-1] == IN_COLS
    batch, seq, _ = x_prompt.shape
    dec_batch, dec_seq, _ = x_sample.shape
    assert dec_seq == CHUNK and seq % ROW_TILE == 0
    n_p, n_s = batch * seq, dec_batch * dec_seq

    def vec(a):
        return a.reshape(1, -1)

    xp = x_prompt.reshape(n_p, D_MODEL)
    xs = x_sample.reshape(n_s, D_MODEL)

    x1_s, hn_s, wg1, wu1, wd1 = _ffn_first(xs, vec(ffn1_norm), ffn1_w_gu[0], ffn1_w_down[0], vec(mix_norm),
                                           True, "ffn1_first")
    x1, hn, wgu2, wd2 = _ffn(xp, x1_s, hn_s, vec(ffn1_norm), (wg1, 0), (wu1, 0), wd1, vec(mix_norm), True,
                             "ffn1", side=[(ffn2_w_gu[0], WGU_CAST_ROWS), (ffn2_w_down[0], WD_CAST_ROWS)])

    u = _glu(hn, w_in[0])
    cos, slo, shi = _rope_tables(batch, seq, dec_batch, dec_seq)
    q, k, v = _qkv(hn, w_in[0], cos, slo, shi)
    gate_steps = (n_p + n_s) // IN_TILE * (2 * D_MODEL // IN_COL_TILE)
    gates, w_co, w_ao, w_o, w_pg, w_pp = _gates(
        hn, w_in[0], side=[(w, max(w.shape[0] // gate_steps, BF16_ROWS))
                           for w in (conv_w_out[0], attn_w_out[0], w_out[0], ple_w_gate[0], ple_w_proj[0])])

    cw, cb, lg, lb = conv_w[0], vec(conv_b), vec(conv_ln_g), vec(conv_ln_b)
    sinks = attn_sinks[0]
    ck = cache_k[0].reshape(dec_batch, -1, KV_W)
    cv = cache_v[0].reshape(dec_batch, -1, KV_W)
    yc_s = _conv_sample(u, n_p, state_conv[0], cw, cb, lg, lb)
    o_s = _attn_sample(q, k, v, sinks, n_p, ck, cv)
    x2_p = _mix_prompt(u, q, k, v, gates, x1, sinks, cw, cb, lg, lb, w_co, w_ao, w_o, n_p, seq)
    x2_s = _post_sample(yc_s, o_s, gates, x1, w_co, w_ao, w_o, n_p)
    x3, = _ffn(x2_p, x2_s, None, vec(ffn2_norm), (wgu2, 0), (wgu2, D_FF), wd2, vec(ffn2_norm), False, "ffn2")

    yp, ys = _ple(x3, p_prompt[0].reshape(n_p, D_PLE), p_sample[0].reshape(n_s, D_PLE),
                  vec(ple_norm), w_pg, w_pp, vec(final_norm))

    w_rows = ck.shape[1]

    def prompt_tail(a, rows):
        return jnp.stack([a[(b + 1) * seq - rows:(b + 1) * seq] for b in range(batch)])

    k_p = prompt_tail(k, WINDOW).reshape(batch, WINDOW, N_KV, HEAD_DIM)
    v_p = prompt_tail(v, WINDOW).reshape(batch, WINDOW, N_KV, HEAD_DIM)
    c_p = prompt_tail(u, CONV_WIDTH - 1)
    k_s = jnp.concatenate([cache_k[0], k[n_p:].reshape(dec_batch, dec_seq, N_KV, HEAD_DIM)], 1)[:, -w_rows:]
    v_s = jnp.concatenate([cache_v[0], v[n_p:].reshape(dec_batch, dec_seq, N_KV, HEAD_DIM)], 1)[:, -w_rows:]
    c_s = jnp.concatenate([state_conv[0], u[n_p:].reshape(dec_batch, dec_seq, D_CONV)], 1)[:, -(CONV_WIDTH - 1):]
    return (yp.reshape(batch, seq, D_MODEL), ys.reshape(dec_batch, dec_seq, D_MODEL),
            k_p[None], v_p[None], c_p[None], k_s[None], v_s[None], c_s[None])
```

```python
import functools

import jax
import jax.numpy as jnp
from jax import lax
from jax.experimental import pallas as pl
from jax.experimental.pallas import tpu as pltpu

D_MODEL = 2048
D_PLE = 256
D_FF = 5504
D_CONV = 1024
CONV_WIDTH = 31
HEAD_DIM = 64
N_HEADS = 16
N_KV = 4
GROUP = N_HEADS // N_KV
ROT_DIM = 16
ROPE_THETA = 500000.0
CHUNK = 64
WINDOW = 128
PAST_LEN = 1024
EPS = 1e-6
NEG = -1e30
LOG2E = 1.4426950408889634
Q_W = N_HEADS * HEAD_DIM
KV_W = N_KV * HEAD_DIM
IN_COLS = 2 * D_CONV + Q_W + 2 * KV_W + 2 * D_MODEL

LANES = 128
SUBLANES = 8
BF16_ROWS = 16
FF_TILE = 512
FF_STEPS = -(-D_FF // FF_TILE)
CAST_TILE = 256
WGU_CAST_ROWS = 16
WD_CAST_ROWS = 32
ROW_TILE = 512
IN_TILE = 1088
IN_COL_TILE = 1024
GLU_COL_TILE = 1024
POST_TILE = 256
PLE_TILE = 512
CONV_HALO = 32
MIX_ROWS = 256
MIX_TILE = WINDOW
VMEM_LIMIT = 56 * 1024 * 1024

F32 = jnp.float32
BF16 = jnp.bfloat16


def _params(*sem):
    return pltpu.CompilerParams(dimension_semantics=sem, vmem_limit_bytes=VMEM_LIMIT)


def _rms(x, g):
    return x * lax.rsqrt(jnp.mean(x * x, axis=-1, keepdims=True) + EPS) * g


def _dot(a, b):
    return jnp.dot(a, b, preferred_element_type=F32)


def _resident(shape):
    return pl.BlockSpec(shape, lambda *_: (0,) * len(shape), pipeline_mode=pl.Buffered(1))


def _side_cast_specs(side, n_steps, step_of):
    in_specs, out_shape = [], []
    for a, rows in side:
        n_blocks = a.shape[0] // rows
        assert a.shape[0] % rows == 0 and n_blocks <= n_steps
        in_specs.append(pl.BlockSpec((rows, a.shape[1]),
                                     lambda *g, n=n_blocks: (jnp.minimum(step_of(*g), n - 1), 0)))
        out_shape.append(jax.ShapeDtypeStruct(a.shape, BF16))
    return in_specs, out_shape


def _side_cast(src_refs, dst_refs):
    for src, dst in zip(src_refs, dst_refs, strict=True):
        dst[...] = src[...].astype(BF16)


def _swiglu_accumulate(xn_ref, xo_ref, wg, wu, wd):
    xn = xn_ref[...]
    g = _dot(xn, wg)
    u = _dot(xn, wu)
    h = (g * jax.nn.sigmoid(g)) * (u * 0.5)
    xo_ref[...] += _dot(h.astype(BF16), wd)


def _ff_steps(tile):
    return -(-D_FF // tile)


def _ff_start(f, tile):
    return pl.multiple_of(jnp.minimum(f * tile, D_FF - tile), LANES)


def _ffn_first_kernel(with_hn, x_ref, g1_ref, wg_ref, wu_ref, wd_ref, g2_ref, xo_ref, *rest):
    hn_ref = rest[0] if with_hn else None
    wgb_ref, wub_ref, wdb_ref, xn_ref = rest[with_hn:]
    f = pl.program_id(0)
    steps = _ff_steps(CAST_TILE)
    wg, wu, wd = wg_ref[...].astype(BF16), wu_ref[...].astype(BF16), wd_ref[...].astype(BF16)

    @pl.when(f == 0)
    def _():
        x = x_ref[...]
        xn_ref[...] = _rms(x, g1_ref[...]).astype(BF16)
        xo_ref[...] = x

    @pl.when(f < steps - 1)
    def _():
        wgb_ref[...] = wg
        wub_ref[...] = wu
        wdb_ref[...] = wd
        _swiglu_accumulate(xn_ref, xo_ref, wg, wu, wd)

    @pl.when(f == steps - 1)
    def _():
        old = steps * CAST_TILE - D_FF
        wg_new, wu_new, wd_new = wg[:, old:], wu[:, old:], wd[old:, :]
        wgb_ref[...] = jnp.concatenate([wg_new, jnp.zeros((D_MODEL, old), BF16)], axis=1)
        wub_ref[...] = jnp.concatenate([wu_new, jnp.zeros((D_MODEL, old), BF16)], axis=1)
        wdb_ref[...] = jnp.concatenate([wd_new, jnp.zeros((old, D_MODEL), BF16)], axis=0)
        _swiglu_accumulate(xn_ref, xo_ref, wg_new, wu_new, wd_new)

    if with_hn:
        @pl.when(f == steps - 1)
        def _():
            hn_ref[...] = _rms(xo_ref[...], g2_ref[...]).astype(BF16)


def _ffn_first(x, g1, w_gu, w_down, g2, with_hn, name):
    assert x.shape[0] == ROW_TILE
    tile = CAST_TILE
    cols = (pl.Element(D_MODEL), pl.Element(tile))
    rows = (pl.Element(tile), pl.Element(D_MODEL))
    vec = pl.BlockSpec((1, D_MODEL), lambda f: (0, 0))
    row = pl.BlockSpec((ROW_TILE, D_MODEL), lambda f: (0, 0))
    gate_win = pl.BlockSpec(cols, lambda f: (0, _ff_start(f, tile)))
    up_win = pl.BlockSpec(cols, lambda f: (0, pl.multiple_of(D_FF + _ff_start(f, tile), LANES)))
    down_win = pl.BlockSpec(rows, lambda f: (_ff_start(f, tile), 0))
    out_shape = [jax.ShapeDtypeStruct((ROW_TILE, D_MODEL), F32)]
    if with_hn:
        out_shape.append(jax.ShapeDtypeStruct((ROW_TILE, D_MODEL), BF16))
    steps = _ff_steps(tile)
    weights_shape = ([jax.ShapeDtypeStruct((D_MODEL, steps * tile), BF16)] * 2
                     + [jax.ShapeDtypeStruct((steps * tile, D_MODEL), BF16)])
    col_blk = pl.BlockSpec((D_MODEL, tile), lambda f: (0, f))
    row_blk = pl.BlockSpec((tile, D_MODEL), lambda f: (f, 0))
    return pl.pallas_call(
        functools.partial(_ffn_first_kernel, with_hn),
        grid=(steps,),
        in_specs=[row, vec, gate_win, up_win, down_win, vec],
        out_specs=[row] * len(out_shape) + [col_blk, col_blk, row_blk],
        out_shape=out_shape + weights_shape,
        scratch_shapes=[pltpu.VMEM((ROW_TILE, D_MODEL), BF16)],
        compiler_params=_params("arbitrary"),
        name=name,
    )(x, g1, w_gu, w_gu, w_down, g2)


def _ffn_kernel(n_prompt_tiles, with_hn, tail_done, n_side, xp_ref, xs_ref, *rest):
    if tail_done:
        hns_ref, rest = rest[0], rest[1:]
    g1_ref, wg_ref, wu_ref, wd_ref, g2_ref = rest[:5]
    side_in, outs = rest[5:5 + n_side], rest[5 + n_side:]
    xo_ref = outs[0]
    hn_ref = outs[1] if with_hn else None
    side_out = outs[1 + with_hn:1 + with_hn + n_side]
    xn_ref = outs[-1]
    i, f = pl.program_id(0), pl.program_id(1)
    _side_cast(side_in, side_out)

    def on(cond):
        return jnp.logical_and(cond, i < n_prompt_tiles) if tail_done else cond

    @pl.when(on(f == 0))
    def _():
        x = xp_ref[...] if tail_done else jnp.where(i < n_prompt_tiles, xp_ref[...], xs_ref[...])
        xn_ref[...] = _rms(x, g1_ref[...]).astype(BF16)
        xo_ref[...] = x

    if tail_done:
        @pl.when(jnp.logical_and(f == 0, i >= n_prompt_tiles))
        def _():
            xo_ref[...] = xs_ref[...]
            hn_ref[...] = hns_ref[...]

    old = FF_STEPS * FF_TILE - D_FF
    pl.when(on(f < FF_STEPS - 1))(
        lambda: _swiglu_accumulate(xn_ref, xo_ref, wg_ref[...], wu_ref[...], wd_ref[...]))
    pl.when(on(f == FF_STEPS - 1))(
        lambda: _swiglu_accumulate(xn_ref, xo_ref, wg_ref[:, old:], wu_ref[:, old:], wd_ref[old:, :]))

    if with_hn:
        @pl.when(on(f == FF_STEPS - 1))
        def _():
            hn_ref[...] = _rms(xo_ref[...], g2_ref[...]).astype(BF16)


def _ffn(xp, xs, hn_s, g1, wg, wu, wd, g2, with_hn, name, side=()):
    n_p, n_s = xp.shape[0] // ROW_TILE, xs.shape[0] // ROW_TILE
    assert n_s == 1 and xs.shape[0] == ROW_TILE
    m = xp.shape[0] + xs.shape[0]
    tail_done = hn_s is not None
    assert with_hn or not tail_done

    def start(i, f):
        return _ff_start(jnp.where(i < n_p, f, FF_STEPS - 1) if tail_done else f, FF_TILE)

    cols = (pl.Element(D_MODEL), pl.Element(FF_TILE))
    rows = (pl.Element(FF_TILE), pl.Element(D_MODEL))
    vec = pl.BlockSpec((1, D_MODEL), lambda i, f: (0, 0))
    row = pl.BlockSpec((ROW_TILE, D_MODEL), lambda i, f: (i, 0))
    if tail_done:
        tail_specs, tail_args = [_resident((ROW_TILE, D_MODEL))] * 2, [xs, hn_s]
    else:
        tail_specs, tail_args = [pl.BlockSpec((ROW_TILE, D_MODEL), lambda i, f: (0, 0))], [xs]
    out_shape = [jax.ShapeDtypeStruct((m, D_MODEL), F32)]
    if with_hn:
        out_shape.append(jax.ShapeDtypeStruct((m, D_MODEL), BF16))
    side_specs, side_shape = _side_cast_specs(side, (n_p + n_s) * FF_STEPS, lambda i, f: i * FF_STEPS + f)
    return pl.pallas_call(
        functools.partial(_ffn_kernel, n_p, with_hn, tail_done, len(side)),
        grid=(n_p + n_s, FF_STEPS),
        in_specs=[pl.BlockSpec((ROW_TILE, D_MODEL), lambda i, f: (jnp.minimum(i, n_p - 1), 0)),
                  *tail_specs, vec,
                  pl.BlockSpec(cols, lambda i, f: (0, pl.multiple_of(wg[1] + start(i, f), LANES))),
                  pl.BlockSpec(cols, lambda i, f: (0, pl.multiple_of(wu[1] + start(i, f), LANES))),
                  pl.BlockSpec(rows, lambda i, f: (start(i, f), 0)),
                  vec, *side_specs],
        out_specs=[row] * len(out_shape) + side_specs,
        out_shape=out_shape + side_shape,
        scratch_shapes=[pltpu.VMEM((ROW_TILE, D_MODEL), BF16)],
        compiler_params=_params("arbitrary", "arbitrary"),
        name=name,
    )(xp, *tail_args, g1, wg[0], wu[0], wd, g2, *[a for a, _ in side])


def _glu_kernel(h_ref, wa_ref, wb_ref, o_ref):
    h = h_ref[...]
    o_ref[...] = _dot(h, wa_ref[...]) * jax.nn.sigmoid(_dot(h, wb_ref[...]))


def _glu(hn, w_in):
    m, tn = hn.shape[0], GLU_COL_TILE
    nj = D_CONV // tn
    return pl.pallas_call(
        _glu_kernel,
        grid=(m // IN_TILE, nj),
        in_specs=[pl.BlockSpec((IN_TILE, D_MODEL), lambda i, j: (i, 0)),
                  pl.BlockSpec((D_MODEL, tn), lambda i, j: (0, j)),
                  pl.BlockSpec((D_MODEL, tn), lambda i, j: (0, j + nj))],
        out_specs=pl.BlockSpec((IN_TILE, tn), lambda i, j: (i, j)),
        out_shape=jax.ShapeDtypeStruct((m, D_CONV), F32),
        compiler_params=_params("parallel", "arbitrary"),
        name="glu",
    )(hn, w_in, w_in)


def _gates_kernel(n_side, h_ref, w_ref, *rest):
    side_in, o_ref, side_out = rest[:n_side], rest[n_side], rest[n_side + 1:]
    _side_cast(side_in, side_out)
    o_ref[...] = jax.nn.sigmoid(_dot(h_ref[...], w_ref[...])).astype(BF16)


def _gates(hn, w_in, side=()):
    m, tn = hn.shape[0], IN_COL_TILE
    col0 = 2 * D_CONV + Q_W + 2 * KV_W
    nj = 2 * D_MODEL // tn
    side_specs, side_shape = _side_cast_specs(side, m // IN_TILE * nj, lambda i, j: i * nj + j)
    return pl.pallas_call(
        functools.partial(_gates_kernel, len(side)),
        grid=(m // IN_TILE, nj),
        in_specs=[pl.BlockSpec((IN_TILE, D_MODEL), lambda i, j: (i, 0)),
                  pl.BlockSpec((pl.Element(D_MODEL), pl.Element(tn)),
                               lambda i, j: (0, pl.multiple_of(col0 + j * tn, LANES))),
                  *side_specs],
        out_specs=[pl.BlockSpec((IN_TILE, tn), lambda i, j: (i, j))] + side_specs,
        out_shape=[jax.ShapeDtypeStruct((m, 2 * D_MODEL), BF16)] + side_shape,
        compiler_params=_params("arbitrary", "arbitrary"),
        name="gates",
    )(hn, w_in, *[a for a, _ in side])


def _rotate(x, cos, sin_lo, sin_hi):
    cols = []
    for c in range(x.shape[1] // LANES):
        xb = x[:, c * LANES:(c + 1) * LANES]
        cols.append(xb * cos
                    + pltpu.roll(xb, LANES - ROT_DIM // 2, 1) * sin_lo
                    + pltpu.roll(xb, ROT_DIM // 2, 1) * sin_hi)
    return jnp.concatenate(cols, axis=1)


def _qkv_kernel(h_ref, wq_ref, wk_ref, wv_ref, cos_ref, slo_ref, shi_ref, q_ref, k_ref, v_ref):
    h = h_ref[...]
    cos, slo, shi = cos_ref[...], slo_ref[...], shi_ref[...]
    q = _rotate(_dot(h, wq_ref[...]), cos, slo, shi)
    q_ref[...] = (q * (LOG2E * HEAD_DIM ** -0.5)).astype(BF16)
    k_ref[...] = _rotate(_dot(h, wk_ref[...]), cos, slo, shi)
    v_ref[...] = _dot(h, wv_ref[...])


def _qkv(hn, w_in, cos, slo, shi):
    m = hn.shape[0]
    c0 = 2 * D_CONV
    tab = pl.BlockSpec((IN_TILE, LANES), lambda i: (i, 0))
    return pl.pallas_call(
        _qkv_kernel,
        grid=(m // IN_TILE,),
        in_specs=[pl.BlockSpec((IN_TILE, D_MODEL), lambda i: (i, 0)),
                  pl.BlockSpec((D_MODEL, Q_W), lambda i: (0, c0 // Q_W), pipeline_mode=pl.Buffered(1)),
                  pl.BlockSpec((D_MODEL, KV_W), lambda i: (0, (c0 + Q_W) // KV_W), pipeline_mode=pl.Buffered(1)),
                  pl.BlockSpec((D_MODEL, KV_W), lambda i: (0, (c0 + Q_W) // KV_W + 1),
                               pipeline_mode=pl.Buffered(1)),
                  tab, tab, tab],
        out_specs=[pl.BlockSpec((IN_TILE, Q_W), lambda i: (i, 0)),
                   pl.BlockSpec((IN_TILE, KV_W), lambda i: (i, 0)),
                   pl.BlockSpec((IN_TILE, KV_W), lambda i: (i, 0))],
        out_shape=[jax.ShapeDtypeStruct((m, Q_W), BF16),
                   jax.ShapeDtypeStruct((m, KV_W), F32),
                   jax.ShapeDtypeStruct((m, KV_W), F32)],
        compiler_params=_params("parallel"),
        name="qkv",
    )(hn, w_in, w_in, w_in, cos, slo, shi)


def _rope_tables(batch, seq, dec_batch, dec_seq):
    inv = ROPE_THETA ** (-jnp.arange(0, ROT_DIM, 2, dtype=F32) / ROT_DIM)
    pos = jnp.concatenate([jnp.tile(jnp.arange(seq, dtype=jnp.int32), batch),
                           jnp.tile(PAST_LEN + jnp.arange(dec_seq, dtype=jnp.int32), dec_batch)])
    ang = pos.astype(F32)[:, None] * inv[None, :]
    cos, sin = jnp.cos(ang), jnp.sin(ang)
    one = jnp.ones((pos.shape[0], HEAD_DIM - ROT_DIM), F32)
    zero = jnp.zeros_like(one)
    zh = jnp.zeros_like(sin)
    c = jnp.concatenate([cos, cos, one], 1)
    lo = jnp.concatenate([-sin, zh, zero], 1)
    hi = jnp.concatenate([zh, sin, zero], 1)
    return [jnp.tile(t, (1, LANES // HEAD_DIM)) for t in (c, lo, hi)]


def _conv_tile(s_ref, row0, c_ref, w_ref, b_ref, lg_ref, lb_ref, rows):
    ext = rows + CONV_HALO - SUBLANES
    for s in range(1, SUBLANES):
        c_ref[s - 1] = s_ref[row0 + s:row0 + s + ext, :]
    base = CONV_HALO - (CONV_WIDTH - 1)
    cols = []
    for cb in range(D_CONV // LANES):
        lanes = slice(cb * LANES, (cb + 1) * LANES)
        acc = jnp.broadcast_to(b_ref[:, lanes], (rows, LANES))
        for j in range(CONV_WIDTH):
            a, s = divmod(base + j, SUBLANES)
            r0 = a * SUBLANES
            if s == 0:
                tap = s_ref[row0 + r0:row0 + r0 + rows, lanes]
            else:
                tap = c_ref[s - 1, r0:r0 + rows, lanes]
            acc = acc + tap * w_ref[j:j + 1, lanes]
        cols.append(acc)
    acc = jnp.concatenate(cols, axis=1)
    mu = jnp.mean(acc, axis=-1, keepdims=True)
    xc = acc - mu
    y = xc * lax.rsqrt(jnp.mean(xc * xc, axis=-1, keepdims=True) + EPS) * lg_ref[...] + lb_ref[...]
    return (y * jax.nn.sigmoid(y)).astype(BF16)


def _conv_sample_kernel(u_ref, past_ref, w_ref, b_ref, lg_ref, lb_ref, o_ref, s_ref, c_ref):
    s_ref[CONV_HALO - (CONV_WIDTH - 1):CONV_HALO, :] = past_ref[0]
    s_ref[CONV_HALO:, :] = u_ref[...]
    o_ref[...] = _conv_tile(s_ref, 0, c_ref, w_ref, b_ref, lg_ref, lb_ref, u_ref.shape[0])


def _conv_common_specs():
    return [pl.BlockSpec((CONV_WIDTH, D_CONV), lambda i: (0, 0)),
            pl.BlockSpec((1, D_CONV), lambda i: (0, 0)),
            pl.BlockSpec((1, D_CONV), lambda i: (0, 0)),
            pl.BlockSpec((1, D_CONV), lambda i: (0, 0))]


def _conv_scratch(block_rows, tile_rows):
    return [pltpu.VMEM((CONV_HALO + block_rows, D_CONV), F32),
            pltpu.VMEM((SUBLANES - 1, CONV_HALO + tile_rows - SUBLANES, D_CONV), F32)]


def _conv_sample(u, row0, state, w, b, lg, lb):
    n_seq, hist, _ = state.shape
    assert hist == CONV_WIDTH - 1
    rows = CHUNK
    blk0 = row0 // rows
    return pl.pallas_call(
        _conv_sample_kernel,
        grid=(n_seq,),
        in_specs=[pl.BlockSpec((rows, D_CONV), lambda i: (blk0 + i, 0)),
                  pl.BlockSpec((1, hist, D_CONV), lambda i: (i, 0, 0)),
                  *_conv_common_specs()],
        out_specs=pl.BlockSpec((rows, D_CONV), lambda i: (i, 0)),
        out_shape=jax.ShapeDtypeStruct((n_seq * rows, D_CONV), BF16),
        scratch_shapes=_conv_scratch(rows, rows),
        compiler_params=_params("parallel"),
        name="conv_sample",
    )(u, state, w, b, lg, lb)


def _attn_tile(q, k, v, sink_ref, valid):
    r = q.shape[0]
    lane_head = lax.broadcasted_iota(jnp.int32, (1, KV_W), 1) // HEAD_DIM
    sel_row = lax.broadcasted_iota(jnp.int32, (KV_W, Q_W), 0)
    sel_col = lax.broadcasted_iota(jnp.int32, (KV_W, Q_W), 1)
    sel = jnp.where(sel_row == sel_col // KV_W * HEAD_DIM + sel_col % HEAD_DIM, 1.0, 0.0).astype(BF16)
    krep = _dot(k, sel).astype(BF16)
    vrep = _dot(v, sel).astype(BF16)
    head_mask = [jnp.where(lane_head == g, 1.0, 0.0).astype(BF16) for g in range(GROUP)]
    scores = []
    for kv in range(N_KV):
        cols = slice(kv * KV_W, (kv + 1) * KV_W)
        lhs = jnp.concatenate([q[:, cols] * head_mask[g] for g in range(GROUP)], axis=0)
        scores.append(lax.dot_general(lhs, krep[:, cols], (((1,), (1,)), ((), ())),
                                      preferred_element_type=F32))
    weights, dens = [], []
    for kv in range(N_KV):
        es = []
        for g in range(GROUP):
            sg = scores[kv][g * r:(g + 1) * r]
            if valid is not None:
                sg = jnp.where(valid, sg, NEG)
            sink = sink_ref[kv * GROUP + g] * LOG2E
            mx = jnp.maximum(jnp.max(sg, axis=-1, keepdims=True), sink)
            e = jnp.exp2(sg - mx)
            dens.append(jnp.sum(e, axis=-1, keepdims=True) + jnp.exp2(sink - mx))
            es.append(e.astype(BF16))
        weights.append(jnp.concatenate(es, axis=0))
    outs = []
    for kv in range(N_KV):
        o4 = _dot(weights[kv], vrep[:, kv * KV_W:(kv + 1) * KV_W])
        og = [o4[g * r:(g + 1) * r] / dens[kv * GROUP + g] for g in range(GROUP)]
        oh = og[GROUP - 1]
        for g in range(GROUP - 2, -1, -1):
            oh = jnp.where(lane_head == g, og[g], oh)
        outs.append(oh)
    return jnp.concatenate(outs, axis=1).astype(BF16)


def _attn_sample_kernel(sink_ref, q_ref, kp_ref, kc_ref, vp_ref, vc_ref, o_ref):
    k = jnp.concatenate([kp_ref[0], kc_ref[...]], axis=0).astype(BF16)
    v = jnp.concatenate([vp_ref[0], vc_ref[...]], axis=0).astype(BF16)
    o_ref[...] = _attn_tile(q_ref[...], k, v, sink_ref, None)


def _attn_sample(q, k, v, sinks, row0, cache_k, cache_v):
    n_seq, w_rows, _ = cache_k.shape
    assert w_rows == WINDOW
    blk0 = row0 // CHUNK
    cache = pl.BlockSpec((1, w_rows, KV_W), lambda i: (i, 0, 0))
    cur = pl.BlockSpec((CHUNK, KV_W), lambda i: (blk0 + i, 0))
    return pl.pallas_call(
        _attn_sample_kernel,
        grid=(n_seq,),
        in_specs=[pl.BlockSpec(memory_space=pltpu.SMEM),
                  pl.BlockSpec((CHUNK, Q_W), lambda i: (blk0 + i, 0)), cache, cur, cache, cur],
        out_specs=pl.BlockSpec((CHUNK, Q_W), lambda i: (i, 0)),
        out_shape=jax.ShapeDtypeStruct((n_seq * CHUNK, Q_W), BF16),
        compiler_params=_params("parallel"),
        name="attn_sample",
    )(sinks, q, cache_k, k, cache_v, v)


def _merge_out(yc, o, gc, ga, x, wc_ref, wa_ref, wo_ref):
    m = gc.astype(F32) * _dot(yc, wc_ref[...]) + ga.astype(F32) * _dot(o, wa_ref[...])
    return x + _dot(m.astype(BF16), wo_ref[...])


def _mix_kernel(per_seq, sink_ref, u_ref, uh_ref, q_ref, kp_ref, kc_ref, vp_ref, vc_ref,
                cw_ref, cb_ref, lg_ref, lb_ref, gc_ref, ga_ref, x_ref, wc_ref, wa_ref, wo_ref,
                out_ref, s_ref, c_ref):
    first = pl.program_id(0) % per_seq == 0
    s_ref[0:CONV_HALO, :] = jnp.where(first, 0.0, uh_ref[...])
    s_ref[CONV_HALO:, :] = u_ref[...]

    k = jnp.concatenate([kp_ref[...], kc_ref[...]], axis=0).astype(BF16)
    v = jnp.concatenate([vp_ref[...], vc_ref[...]], axis=0).astype(BF16)
    nk = WINDOW + MIX_TILE
    col = lax.broadcasted_iota(jnp.int32, (MIX_TILE, nk), 1)
    d = col // CHUNK - lax.broadcasted_iota(jnp.int32, (MIX_TILE, nk), 0) // CHUNK
    band = (d >= 0) & (d <= WINDOW // CHUNK)

    for r0 in range(0, MIX_ROWS, MIX_TILE):
        rows = slice(r0, r0 + MIX_TILE)
        yc = _conv_tile(s_ref, r0, c_ref, cw_ref, cb_ref, lg_ref, lb_ref, MIX_TILE)
        valid = band & ((col >= WINDOW) | jnp.logical_not(first)) if r0 == 0 else band
        o = _attn_tile(q_ref[rows, :], k[r0:r0 + nk], v[r0:r0 + nk], sink_ref, valid)
        out_ref[rows, :] = _merge_out(yc, o, gc_ref[rows, :], ga_ref[rows, :], x_ref[rows, :],
                                      wc_ref, wa_ref, wo_ref)


def _mix_prompt(u, q, k, v, gates, x, sinks, cw, cb, lg, lb, wc, wa, wo, n_rows, seq):
    rows = MIX_ROWS
    halo_per_block = rows // CONV_HALO
    pre_per_block = rows // WINDOW
    kv_prev = pl.BlockSpec((WINDOW, KV_W), lambda i: (jnp.maximum(i * pre_per_block - 1, 0), 0))
    kv_cur = pl.BlockSpec((rows, KV_W), lambda i: (i, 0))
    return pl.pallas_call(
        functools.partial(_mix_kernel, seq // rows),
        grid=(n_rows // rows,),
        in_specs=[pl.BlockSpec(memory_space=pltpu.SMEM),
                  pl.BlockSpec((rows, D_CONV), lambda i: (i, 0)),
                  pl.BlockSpec((CONV_HALO, D_CONV), lambda i: (jnp.maximum(i * halo_per_block - 1, 0), 0)),
                  pl.BlockSpec((rows, Q_W), lambda i: (i, 0)),
                  kv_prev, kv_cur, kv_prev, kv_cur,
                  *_conv_common_specs(),
                  pl.BlockSpec((rows, D_MODEL), lambda i: (i, 0)),
                  pl.BlockSpec((rows, D_MODEL), lambda i: (i, 1)),
                  pl.BlockSpec((rows, D_MODEL), lambda i: (i, 0)),
                  _resident((D_CONV, D_MODEL)), _resident((Q_W, D_MODEL)), _resident((D_MODEL, D_MODEL))],
        out_specs=pl.BlockSpec((rows, D_MODEL), lambda i: (i, 0)),
        out_shape=jax.ShapeDtypeStruct((n_rows, D_MODEL), F32),
        scratch_shapes=_conv_scratch(rows, MIX_TILE),
        compiler_params=_params("parallel"),
        name="mix_prompt",
    )(sinks, u, u, q, k, k, v, v, cw, cb, lg, lb, gates, gates, x, wc, wa, wo)


def _post_sample_kernel(yc_ref, o_ref, gc_ref, ga_ref, x_ref, wc_ref, wa_ref, wo_ref, out_ref):
    out_ref[...] = _merge_out(yc_ref[...], o_ref[...], gc_ref[...], ga_ref[...], x_ref[...],
                              wc_ref, wa_ref, wo_ref)


def _post_sample(yc, o, gates, x, wc, wa, wo, row0):
    tm = POST_TILE
    blk0 = row0 // tm
    return pl.pallas_call(
        _post_sample_kernel,
        grid=(yc.shape[0] // tm,),
        in_specs=[pl.BlockSpec((tm, D_CONV), lambda i: (i, 0)),
                  pl.BlockSpec((tm, Q_W), lambda i: (i, 0)),
                  pl.BlockSpec((tm, D_MODEL), lambda i: (blk0 + i, 0)),
                  pl.BlockSpec((tm, D_MODEL), lambda i: (blk0 + i, 1)),
                  pl.BlockSpec((tm, D_MODEL), lambda i: (blk0 + i, 0)),
                  _resident((D_CONV, D_MODEL)), _resident((Q_W, D_MODEL)), _resident((D_MODEL, D_MODEL))],
        out_specs=pl.BlockSpec((tm, D_MODEL), lambda i: (i, 0)),
        out_shape=jax.ShapeDtypeStruct((yc.shape[0], D_MODEL), F32),
        compiler_params=_params("parallel"),
        name="post_sample",
    )(yc, o, gates, gates, x, wc, wa, wo)


def _ple_kernel(n_prompt_tiles, x_ref, pp_ref, ps_ref, gn_ref, wg_ref, wp_ref, fn_ref, yp_ref, ys_ref):
    i = pl.program_id(0)
    x = x_ref[...]
    gate = jax.nn.sigmoid(_dot(_rms(x, gn_ref[...]).astype(BF16), wg_ref[...]))
    pe = jnp.where(i < n_prompt_tiles, pp_ref[...], ps_ref[...]).astype(BF16)
    y = _rms(x + gate * _dot(pe, wp_ref[...]), fn_ref[...])

    @pl.when(i < n_prompt_tiles)
    def _():
        yp_ref[...] = y

    @pl.when(i >= n_prompt_tiles)
    def _():
        ys_ref[...] = y


def _ple(x, pp, ps, gn, wg, wp, fn):
    tm = PLE_TILE
    n_p, n_s = pp.shape[0] // tm, ps.shape[0] // tm
    vec = pl.BlockSpec((1, D_MODEL), lambda i: (0, 0))

    def p_idx(i):
        return (jnp.minimum(i, n_p - 1), 0)

    def s_idx(i):
        return (jnp.maximum(i - n_p, 0), 0)

    return pl.pallas_call(
        functools.partial(_ple_kernel, n_p),
        grid=(n_p + n_s,),
        in_specs=[pl.BlockSpec((tm, D_MODEL), lambda i: (i, 0)),
                  pl.BlockSpec((tm, D_PLE), p_idx),
                  pl.BlockSpec((tm, D_PLE), s_idx),
                  vec, _resident((D_MODEL, D_MODEL)), _resident((D_PLE, D_MODEL)), vec],
        out_specs=[pl.BlockSpec((tm, D_MODEL), p_idx), pl.BlockSpec((tm, D_MODEL), s_idx)],
        out_shape=[jax.ShapeDtypeStruct((pp.shape[0], D_MODEL), F32),
                   jax.ShapeDtypeStruct((ps.shape[0], D_MODEL), F32)],
        compiler_params=_params("arbitrary"),
        name="ple_final",
    )(x, pp, ps, gn, wg, wp, fn)


def kernel(x_prompt, x_sample, p_prompt, p_sample, state_conv, cache_k, cache_v, ffn1_norm, ffn1_w_gu, ffn1_w_down, mix_norm, w_in, conv_w, conv_b, conv_ln_g, conv_ln_b, conv_w_out, attn_sinks, attn_w_out, w_out, ffn2_norm, ffn2_w_gu, ffn2_w_down, ple_norm, ple_w_gate, ple_w_proj, final_norm):
    assert x_prompt.shape[-1] == D_MODEL and ffn1_norm.shape[0] == 1 and w_in.shape[-1] == IN_COLS
    batch, seq, _ = x_prompt.shape
    dec_batch, dec_seq, _ = x_sample.shape
    assert dec_seq == CHUNK and seq % ROW_TILE == 0
    n_p, n_s = batch * seq, dec_batch * dec_seq

    def vec(a):
        return a.reshape(1, -1)

    xp = x_prompt.reshape(n_p, D_MODEL)
    xs = x_sample.reshape(n_s, D_MODEL)

    x1_s, hn_s, wg1, wu1, wd1 = _ffn_first(xs, vec(ffn1_norm), ffn1_w_gu[0], ffn1_w_down[0], vec(mix_norm),
                                           True, "ffn1_first")
    x1, hn, wgu2, wd2, w_in_b = _ffn(
        xp, x1_s, hn_s, vec(ffn1_norm), (wg1, 0), (wu1, 0), wd1, vec(mix_norm), True, "ffn1",
        side=[(ffn2_w_gu[0], WGU_CAST_ROWS), (ffn2_w_down[0], WD_CAST_ROWS), (w_in[0], WGU_CAST_ROWS)])

    u = _glu(hn, w_in_b)
    cos, slo, shi = _rope_tables(batch, seq, dec_batch, dec_seq)
    q, k, v = _qkv(hn, w_in_b, cos, slo, shi)
    gate_steps = (n_p + n_s) // IN_TILE * (2 * D_MODEL // IN_COL_TILE)
    gates, w_co, w_ao, w_o, w_pg, w_pp = _gates(
        hn, w_in_b, side=[(w, max(w.shape[0] // gate_steps, BF16_ROWS))
                           for w in (conv_w_out[0], attn_w_out[0], w_out[0], ple_w_gate[0], ple_w_proj[0])])

    cw, cb, lg, lb = conv_w[0], vec(conv_b), vec(conv_ln_g), vec(conv_ln_b)
    sinks = attn_sinks[0]
    ck = cache_k[0].reshape(dec_batch, -1, KV_W)
    cv = cache_v[0].reshape(dec_batch, -1, KV_W)
    yc_s = _conv_sample(u, n_p, state_conv[0], cw, cb, lg, lb)
    o_s = _attn_sample(q, k, v, sinks, n_p, ck, cv)
    x2_p = _mix_prompt(u, q, k, v, gates, x1, sinks, cw, cb, lg, lb, w_co, w_ao, w_o, n_p, seq)
    x2_s = _post_sample(yc_s, o_s, gates, x1, w_co, w_ao, w_o, n_p)
    x3, = _ffn(x2_p, x2_s, None, vec(ffn2_norm), (wgu2, 0), (wgu2, D_FF), wd2, vec(ffn2_norm), False, "ffn2")

    yp, ys = _ple(x3, p_prompt[0].reshape(n_p, D_PLE), p_sample[0].reshape(n_s, D_PLE),
                  vec(ple_norm), w_pg, w_pp, vec(final_norm))

    w_rows = ck.shape[1]

    def prompt_tail(a, rows):
        return jnp.stack([a[(b + 1) * seq - rows:(b + 1) * seq] for b in range(batch)])

    k_p = prompt_tail(k, WINDOW).reshape(batch, WINDOW, N_KV, HEAD_DIM)
    v_p = prompt_tail(v, WINDOW).reshape(batch, WINDOW, N_KV, HEAD_DIM)
    c_p = prompt_tail(u, CONV_WIDTH - 1)
    k_s = jnp.concatenate([cache_k[0], k[n_p:].reshape(dec_batch, dec_seq, N_KV, HEAD_DIM)], 1)[:, -w_rows:]
    v_s = jnp.concatenate([cache_v[0], v[n_p:].reshape(dec_batch, dec_seq, N_KV, HEAD_DIM)], 1)[:, -w_rows:]
    c_s = jnp.concatenate([state_conv[0], u[n_p:].reshape(dec_batch, dec_seq, D_CONV)], 1)[:, -(CONV_WIDTH - 1):]
    return (yp.reshape(batch, seq, D_MODEL), ys.reshape(dec_batch, dec_seq, D_MODEL),
            k_p[None], v_p[None], c_p[None], k_s[None], v_s[None], c_s[None])
```

```python
import functools

import jax
import jax.numpy as jnp
from jax import lax
from jax.experimental import pallas as pl
from jax.experimental.pallas import tpu as pltpu

D_MODEL = 2048
D_PLE = 256
D_FF = 5504
D_CONV = 1024
CONV_WIDTH = 31
HEAD_DIM = 64
N_HEADS = 16
N_KV = 4
GROUP = N_HEADS // N_KV
ROT_DIM = 16
ROPE_THETA = 500000.0
CHUNK = 64
WINDOW = 128
PAST_LEN = 1024
EPS = 1e-6
NEG = -1e30
LOG2E = 1.4426950408889634
Q_W = N_HEADS * HEAD_DIM
KV_W = N_KV * HEAD_DIM
IN_COLS = 2 * D_CONV + Q_W + 2 * KV_W + 2 * D_MODEL

LANES = 128
SUBLANES = 8
BF16_ROWS = 16
FF_TILE = 512
FF_STEPS = -(-D_FF // FF_TILE)
CAST_TILE = 256
WGU_CAST_ROWS = 16
WD_CAST_ROWS = 32
ROW_TILE = 512
IN_TILE = 1088
IN_COL_TILE = 1024
GLU_COL_TILE = 1024
POST_TILE = 256
PLE_TILE = 512
CONV_HALO = 32
MIX_ROWS = 256
MIX_TILE = WINDOW
VMEM_LIMIT = 56 * 1024 * 1024

F32 = jnp.float32
BF16 = jnp.bfloat16


def _params(*sem):
    return pltpu.CompilerParams(dimension_semantics=sem, vmem_limit_bytes=VMEM_LIMIT)


def _rms(x, g):
    return x * lax.rsqrt(jnp.mean(x * x, axis=-1, keepdims=True) + EPS) * g


def _dot(a, b):
    return jnp.dot(a, b, preferred_element_type=F32)


def _resident(shape):
    return pl.BlockSpec(shape, lambda *_: (0,) * len(shape), pipeline_mode=pl.Buffered(1))


def _side_cast_specs(side, n_steps, step_of):
    in_specs, out_shape = [], []
    for a, rows in side:
        n_blocks = a.shape[0] // rows
        assert a.shape[0] % rows == 0 and n_blocks <= n_steps
        in_specs.append(pl.BlockSpec((rows, a.shape[1]),
                                     lambda *g, n=n_blocks: (jnp.minimum(step_of(*g), n - 1), 0)))
        out_shape.append(jax.ShapeDtypeStruct(a.shape, BF16))
    return in_specs, out_shape


def _side_cast(src_refs, dst_refs):
    for src, dst in zip(src_refs, dst_refs, strict=True):
        dst[...] = src[...].astype(BF16)


def _swiglu_accumulate(xn_ref, xo_ref, wg, wu, wd):
    xn = xn_ref[...]
    g = _dot(xn, wg)
    u = _dot(xn, wu)
    h = (g * jax.nn.sigmoid(g)) * (u * 0.5)
    xo_ref[...] += _dot(h.astype(BF16), wd)


def _ff_steps(tile):
    return -(-D_FF // tile)


def _ff_start(f, tile):
    return pl.multiple_of(jnp.minimum(f * tile, D_FF - tile), LANES)


def _ffn_first_kernel(with_hn, x_ref, g1_ref, wg_ref, wu_ref, wd_ref, g2_ref, xo_ref, *rest):
    hn_ref = rest[0] if with_hn else None
    wgb_ref, wub_ref, wdb_ref, xn_ref = rest[with_hn:]
    f = pl.program_id(0)
    steps = _ff_steps(CAST_TILE)
    wg, wu, wd = wg_ref[...].astype(BF16), wu_ref[...].astype(BF16), wd_ref[...].astype(BF16)

    @pl.when(f == 0)
    def _():
        x = x_ref[...]
        xn_ref[...] = _rms(x, g1_ref[...]).astype(BF16)
        xo_ref[...] = x

    @pl.when(f < steps - 1)
    def _():
        wgb_ref[...] = wg
        wub_ref[...] = wu
        wdb_ref[...] = wd
        _swiglu_accumulate(xn_ref, xo_ref, wg, wu, wd)

    @pl.when(f == steps - 1)
    def _():
        old = steps * CAST_TILE - D_FF
        wg_new, wu_new, wd_new = wg[:, old:], wu[:, old:], wd[old:, :]
        wgb_ref[...] = jnp.concatenate([wg_new, jnp.zeros((D_MODEL, old), BF16)], axis=1)
        wub_ref[...] = jnp.concatenate([wu_new, jnp.zeros((D_MODEL, old), BF16)], axis=1)
        wdb_ref[...] = jnp.concatenate([wd_new, jnp.zeros((old, D_MODEL), BF16)], axis=0)
        _swiglu_accumulate(xn_ref, xo_ref, wg_new, wu_new, wd_new)

    if with_hn:
        @pl.when(f == steps - 1)
        def _():
            hn_ref[...] = _rms(xo_ref[...], g2_ref[...]).astype(BF16)


def _ffn_first(x, g1, w_gu, w_down, g2, with_hn, name):
    assert x.shape[0] == ROW_TILE
    tile = CAST_TILE
    cols = (pl.Element(D_MODEL), pl.Element(tile))
    rows = (pl.Element(tile), pl.Element(D_MODEL))
    vec = pl.BlockSpec((1, D_MODEL), lambda f: (0, 0))
    row = pl.BlockSpec((ROW_TILE, D_MODEL), lambda f: (0, 0))
    gate_win = pl.BlockSpec(cols, lambda f: (0, _ff_start(f, tile)))
    up_win = pl.BlockSpec(cols, lambda f: (0, pl.multiple_of(D_FF + _ff_start(f, tile), LANES)))
    down_win = pl.BlockSpec(rows, lambda f: (_ff_start(f, tile), 0))
    out_shape = [jax.ShapeDtypeStruct((ROW_TILE, D_MODEL), F32)]
    if with_hn:
        out_shape.append(jax.ShapeDtypeStruct((ROW_TILE, D_MODEL), BF16))
    steps = _ff_steps(tile)
    weights_shape = ([jax.ShapeDtypeStruct((D_MODEL, steps * tile), BF16)] * 2
                     + [jax.ShapeDtypeStruct((steps * tile, D_MODEL), BF16)])
    col_blk = pl.BlockSpec((D_MODEL, tile), lambda f: (0, f))
    row_blk = pl.BlockSpec((tile, D_MODEL), lambda f: (f, 0))
    return pl.pallas_call(
        functools.partial(_ffn_first_kernel, with_hn),
        grid=(steps,),
        in_specs=[row, vec, gate_win, up_win, down_win, vec],
        out_specs=[row] * len(out_shape) + [col_blk, col_blk, row_blk],
        out_shape=out_shape + weights_shape,
        scratch_shapes=[pltpu.VMEM((ROW_TILE, D_MODEL), BF16)],
        compiler_params=_params("arbitrary"),
        name=name,
    )(x, g1, w_gu, w_gu, w_down, g2)


def _ffn_kernel(n_prompt_tiles, with_hn, tail_done, n_side, xp_ref, xs_ref, *rest):
    if tail_done:
        hns_ref, rest = rest[0], rest[1:]
    g1_ref, wg_ref, wu_ref, wd_ref, g2_ref = rest[:5]
    side_in, outs = rest[5:5 + n_side], rest[5 + n_side:]
    xo_ref = outs[0]
    hn_ref = outs[1] if with_hn else None
    side_out = outs[1 + with_hn:1 + with_hn + n_side]
    xn_ref = outs[-1]
    i, f = pl.program_id(0), pl.program_id(1)
    _side_cast(side_in, side_out)

    def on(cond):
        return jnp.logical_and(cond, i < n_prompt_tiles) if tail_done else cond

    @pl.when(on(f == 0))
    def _():
        x = xp_ref[...] if tail_done else jnp.where(i < n_prompt_tiles, xp_ref[...], xs_ref[...])
        xn_ref[...] = _rms(x, g1_ref[...]).astype(BF16)
        xo_ref[...] = x

    if tail_done:
        @pl.when(jnp.logical_and(f == 0, i >= n_prompt_tiles))
        def _():
            xo_ref[...] = xs_ref[...]
            hn_ref[...] = hns_ref[...]

    old = FF_STEPS * FF_TILE - D_FF
    pl.when(on(f < FF_STEPS - 1))(
        lambda: _swiglu_accumulate(xn_ref, xo_ref, wg_ref[...], wu_ref[...], wd_ref[...]))
    pl.when(on(f == FF_STEPS - 1))(
        lambda: _swiglu_accumulate(xn_ref, xo_ref, wg_ref[:, old:], wu_ref[:, old:], wd_ref[old:, :]))

    if with_hn:
        @pl.when(on(f == FF_STEPS - 1))
        def _():
            hn_ref[...] = _rms(xo_ref[...], g2_ref[...]).astype(BF16)


def _ffn(xp, xs, hn_s, g1, wg, wu, wd, g2, with_hn, name, side=()):
    n_p, n_s = xp.shape[0] // ROW_TILE, xs.shape[0] // ROW_TILE
    assert n_s == 1 and xs.shape[0] == ROW_TILE
    m = xp.shape[0] + xs.shape[0]
    tail_done = hn_s is not None
    assert with_hn or not tail_done

    def start(i, f):
        return _ff_start(jnp.where(i < n_p, f, FF_STEPS - 1) if tail_done else f, FF_TILE)

    cols = (pl.Element(D_MODEL), pl.Element(FF_TILE))
    rows = (pl.Element(FF_TILE), pl.Element(D_MODEL))
    vec = pl.BlockSpec((1, D_MODEL), lambda i, f: (0, 0))
    row = pl.BlockSpec((ROW_TILE, D_MODEL), lambda i, f: (i, 0))
    if tail_done:
        tail_specs, tail_args = [_resident((ROW_TILE, D_MODEL))] * 2, [xs, hn_s]
    else:
        tail_specs, tail_args = [pl.BlockSpec((ROW_TILE, D_MODEL), lambda i, f: (0, 0))], [xs]
    out_shape = [jax.ShapeDtypeStruct((m, D_MODEL), F32)]
    if with_hn:
        out_shape.append(jax.ShapeDtypeStruct((m, D_MODEL), BF16))
    side_specs, side_shape = _side_cast_specs(side, (n_p + n_s) * FF_STEPS, lambda i, f: i * FF_STEPS + f)
    return pl.pallas_call(
        functools.partial(_ffn_kernel, n_p, with_hn, tail_done, len(side)),
        grid=(n_p + n_s, FF_STEPS),
        in_specs=[pl.BlockSpec((ROW_TILE, D_MODEL), lambda i, f: (jnp.minimum(i, n_p - 1), 0)),
                  *tail_specs, vec,
                  pl.BlockSpec(cols, lambda i, f: (0, pl.multiple_of(wg[1] + start(i, f), LANES))),
                  pl.BlockSpec(cols, lambda i, f: (0, pl.multiple_of(wu[1] + start(i, f), LANES))),
                  pl.BlockSpec(rows, lambda i, f: (start(i, f), 0)),
                  vec, *side_specs],
        out_specs=[row] * len(out_shape) + side_specs,
        out_shape=out_shape + side_shape,
        scratch_shapes=[pltpu.VMEM((ROW_TILE, D_MODEL), BF16)],
        compiler_params=_params("arbitrary", "arbitrary"),
        name=name,
    )(xp, *tail_args, g1, wg[0], wu[0], wd, g2, *[a for a, _ in side])


def _glu_kernel(h_ref, wa_ref, wb_ref, o_ref):
    h = h_ref[...]
    o_ref[...] = _dot(h, wa_ref[...]) * jax.nn.sigmoid(_dot(h, wb_ref[...]))


def _glu(hn, w_in):
    m, tn = hn.shape[0], GLU_COL_TILE
    nj = D_CONV // tn
    return pl.pallas_call(
        _glu_kernel,
        grid=(m // IN_TILE, nj),
        in_specs=[pl.BlockSpec((IN_TILE, D_MODEL), lambda i, j: (i, 0)),
                  pl.BlockSpec((D_MODEL, tn), lambda i, j: (0, j)),
                  pl.BlockSpec((D_MODEL, tn), lambda i, j: (0, j + nj))],
        out_specs=pl.BlockSpec((IN_TILE, tn), lambda i, j: (i, j)),
        out_shape=jax.ShapeDtypeStruct((m, D_CONV), F32),
        compiler_params=_params("parallel", "arbitrary"),
        name="glu",
    )(hn, w_in, w_in)


def _gates_kernel(n_side, h_ref, w_ref, *rest):
    side_in, o_ref, side_out = rest[:n_side], rest[n_side], rest[n_side + 1:]
    _side_cast(side_in, side_out)
    o_ref[...] = jax.nn.sigmoid(_dot(h_ref[...], w_ref[...])).astype(BF16)


def _gates(hn, w_in, side=()):
    m, tn = hn.shape[0], IN_COL_TILE
    col0 = 2 * D_CONV + Q_W + 2 * KV_W
    nj = 2 * D_MODEL // tn
    side_specs, side_shape = _side_cast_specs(side, m // IN_TILE * nj, lambda i, j: i * nj + j)
    return pl.pallas_call(
        functools.partial(_gates_kernel, len(side)),
        grid=(m // IN_TILE, nj),
        in_specs=[pl.BlockSpec((IN_TILE, D_MODEL), lambda i, j: (i, 0)),
                  pl.BlockSpec((pl.Element(D_MODEL), pl.Element(tn)),
                               lambda i, j: (0, pl.multiple_of(col0 + j * tn, LANES))),
                  *side_specs],
        out_specs=[pl.BlockSpec((IN_TILE, tn), lambda i, j: (i, j))] + side_specs,
        out_shape=[jax.ShapeDtypeStruct((m, 2 * D_MODEL), BF16)] + side_shape,
        compiler_params=_params("arbitrary", "arbitrary"),
        name="gates",
    )(hn, w_in, *[a for a, _ in side])


def _rotate(x, cos, sin_lo, sin_hi):
    cols = []
    for c in range(x.shape[1] // LANES):
        xb = x[:, c * LANES:(c + 1) * LANES]
        cols.append(xb * cos
                    + pltpu.roll(xb, LANES - ROT_DIM // 2, 1) * sin_lo
                    + pltpu.roll(xb, ROT_DIM // 2, 1) * sin_hi)
    return jnp.concatenate(cols, axis=1)


def _qkv_kernel(h_ref, wq_ref, wk_ref, wv_ref, cos_ref, slo_ref, shi_ref, q_ref, k_ref, v_ref):
    h = h_ref[...]
    cos, slo, shi = cos_ref[...], slo_ref[...], shi_ref[...]
    q = _rotate(_dot(h, wq_ref[...]), cos, slo, shi)
    q_ref[...] = (q * (LOG2E * HEAD_DIM ** -0.5)).astype(BF16)
    k_ref[...] = _rotate(_dot(h, wk_ref[...]), cos, slo, shi)
    v_ref[...] = _dot(h, wv_ref[...])


def _qkv(hn, w_in, cos, slo, shi):
    m = hn.shape[0]
    c0 = 2 * D_CONV
    tab = pl.BlockSpec((IN_TILE, LANES), lambda i: (i, 0))
    return pl.pallas_call(
        _qkv_kernel,
        grid=(m // IN_TILE,),
        in_specs=[pl.BlockSpec((IN_TILE, D_MODEL), lambda i: (i, 0)),
                  pl.BlockSpec((D_MODEL, Q_W), lambda i: (0, c0 // Q_W), pipeline_mode=pl.Buffered(1)),
                  pl.BlockSpec((D_MODEL, KV_W), lambda i: (0, (c0 + Q_W) // KV_W), pipeline_mode=pl.Buffered(1)),
                  pl.BlockSpec((D_MODEL, KV_W), lambda i: (0, (c0 + Q_W) // KV_W + 1),
                               pipeline_mode=pl.Buffered(1)),
                  tab, tab, tab],
        out_specs=[pl.BlockSpec((IN_TILE, Q_W), lambda i: (i, 0)),
                   pl.BlockSpec((IN_TILE, KV_W), lambda i: (i, 0)),
                   pl.BlockSpec((IN_TILE, KV_W), lambda i: (i, 0))],
        out_shape=[jax.ShapeDtypeStruct((m, Q_W), BF16),
                   jax.ShapeDtypeStruct((m, KV_W), F32),
                   jax.ShapeDtypeStruct((m, KV_W), F32)],
        compiler_params=_params("parallel"),
        name="qkv",
    )(hn, w_in, w_in, w_in, cos, slo, shi)


def _rope_tables(batch, seq, dec_batch, dec_seq):
    inv = ROPE_THETA ** (-jnp.arange(0, ROT_DIM, 2, dtype=F32) / ROT_DIM)
    pos = jnp.concatenate([jnp.tile(jnp.arange(seq, dtype=jnp.int32), batch),
                           jnp.tile(PAST_LEN + jnp.arange(dec_seq, dtype=jnp.int32), dec_batch)])
    ang = pos.astype(F32)[:, None] * inv[None, :]
    cos, sin = jnp.cos(ang), jnp.sin(ang)
    one = jnp.ones((pos.shape[0], HEAD_DIM - ROT_DIM), F32)
    zero = jnp.zeros_like(one)
    zh = jnp.zeros_like(sin)
    c = jnp.concatenate([cos, cos, one], 1)
    lo = jnp.concatenate([-sin, zh, zero], 1)
    hi = jnp.concatenate([zh, sin, zero], 1)
    return [jnp.tile(t, (1, LANES // HEAD_DIM)) for t in (c, lo, hi)]


def _conv_stage(s_ref, history, u_ref):
    rows0 = CONV_HALO - history.shape[0]
    for cb in range(D_CONV // LANES):
        lanes = slice(cb * LANES, (cb + 1) * LANES)
        s_ref[cb, rows0:CONV_HALO, :] = history[:, lanes]
        s_ref[cb, CONV_HALO:, :] = u_ref[:, lanes]


def _conv_tile(s_ref, row0, c_ref, w_ref, b_ref, lg_ref, lb_ref, rows):
    base = CONV_HALO - (CONV_WIDTH - 1)
    n = rows // SUBLANES
    for cb in range(D_CONV // LANES):
        lanes = slice(cb * LANES, (cb + 1) * LANES)
        accs = [jnp.broadcast_to(b_ref[:, lanes], (n, LANES)) for _ in range(SUBLANES)]
        for j in range(CONV_WIDTH):
            w = jnp.broadcast_to(w_ref[j:j + 1, lanes], (n, LANES))
            for p in range(SUBLANES):
                accs[p] = accs[p] + s_ref[cb, pl.ds(row0 + base + j + p, n, stride=SUBLANES), :] * w
        for p in range(SUBLANES):
            c_ref[cb, pl.ds(p, n, stride=SUBLANES), :] = accs[p]
    acc = jnp.concatenate([c_ref[cb] for cb in range(D_CONV // LANES)], axis=1)
    mu = jnp.mean(acc, axis=-1, keepdims=True)
    xc = acc - mu
    y = xc * lax.rsqrt(jnp.mean(xc * xc, axis=-1, keepdims=True) + EPS) * lg_ref[...] + lb_ref[...]
    return (y * jax.nn.sigmoid(y)).astype(BF16)


def _conv_sample_kernel(u_ref, past_ref, w_ref, b_ref, lg_ref, lb_ref, o_ref, s_ref, c_ref):
    _conv_stage(s_ref, past_ref[0], u_ref)
    o_ref[...] = _conv_tile(s_ref, 0, c_ref, w_ref, b_ref, lg_ref, lb_ref, u_ref.shape[0])


def _conv_common_specs():
    return [pl.BlockSpec((CONV_WIDTH, D_CONV), lambda i: (0, 0)),
            pl.BlockSpec((1, D_CONV), lambda i: (0, 0)),
            pl.BlockSpec((1, D_CONV), lambda i: (0, 0)),
            pl.BlockSpec((1, D_CONV), lambda i: (0, 0))]


def _conv_scratch(block_rows, tile_rows):
    return [pltpu.VMEM((D_CONV // LANES, CONV_HALO + block_rows, LANES), F32),
            pltpu.VMEM((D_CONV // LANES, tile_rows, LANES), F32)]


def _conv_sample(u, row0, state, w, b, lg, lb):
    n_seq, hist, _ = state.shape
    assert hist == CONV_WIDTH - 1
    rows = CHUNK
    blk0 = row0 // rows
    return pl.pallas_call(
        _conv_sample_kernel,
        grid=(n_seq,),
        in_specs=[pl.BlockSpec((rows, D_CONV), lambda i: (blk0 + i, 0)),
                  pl.BlockSpec((1, hist, D_CONV), lambda i: (i, 0, 0)),
                  *_conv_common_specs()],
        out_specs=pl.BlockSpec((rows, D_CONV), lambda i: (i, 0)),
        out_shape=jax.ShapeDtypeStruct((n_seq * rows, D_CONV), BF16),
        scratch_shapes=_conv_scratch(rows, rows),
        compiler_params=_params("parallel"),
        name="conv_sample",
    )(u, state, w, b, lg, lb)


def _attn_tile(q, k, v, sink_ref, valid):
    r = q.shape[0]
    lane_head = lax.broadcasted_iota(jnp.int32, (1, KV_W), 1) // HEAD_DIM
    sel_row = lax.broadcasted_iota(jnp.int32, (KV_W, Q_W), 0)
    sel_col = lax.broadcasted_iota(jnp.int32, (KV_W, Q_W), 1)
    sel = jnp.where(sel_row == sel_col // KV_W * HEAD_DIM + sel_col % HEAD_DIM, 1.0, 0.0).astype(BF16)
    krep = _dot(k, sel).astype(BF16)
    vrep = _dot(v, sel).astype(BF16)
    head_mask = [jnp.where(lane_head == g, 1.0, 0.0).astype(BF16) for g in range(GROUP)]
    scores = []
    for kv in range(N_KV):
        cols = slice(kv * KV_W, (kv + 1) * KV_W)
        lhs = jnp.concatenate([q[:, cols] * head_mask[g] for g in range(GROUP)], axis=0)
        scores.append(lax.dot_general(lhs, krep[:, cols], (((1,), (1,)), ((), ())),
                                      preferred_element_type=F32))
    weights, dens = [], []
    for kv in range(N_KV):
        es = []
        for g in range(GROUP):
            sg = scores[kv][g * r:(g + 1) * r]
            if valid is not None:
                sg = jnp.where(valid, sg, NEG)
            sink = sink_ref[kv * GROUP + g] * LOG2E
            mx = jnp.maximum(jnp.max(sg, axis=-1, keepdims=True), sink)
            e = jnp.exp2(sg - mx)
            dens.append(jnp.sum(e, axis=-1, keepdims=True) + jnp.exp2(sink - mx))
            es.append(e.astype(BF16))
        weights.append(jnp.concatenate(es, axis=0))
    outs = []
    for kv in range(N_KV):
        o4 = _dot(weights[kv], vrep[:, kv * KV_W:(kv + 1) * KV_W])
        og = [o4[g * r:(g + 1) * r] / dens[kv * GROUP + g] for g in range(GROUP)]
        oh = og[GROUP - 1]
        for g in range(GROUP - 2, -1, -1):
            oh = jnp.where(lane_head == g, og[g], oh)
        outs.append(oh)
    return jnp.concatenate(outs, axis=1).astype(BF16)


def _attn_sample_kernel(sink_ref, q_ref, kp_ref, kc_ref, vp_ref, vc_ref, o_ref):
    k = jnp.concatenate([kp_ref[0], kc_ref[...]], axis=0).astype(BF16)
    v = jnp.concatenate([vp_ref[0], vc_ref[...]], axis=0).astype(BF16)
    o_ref[...] = _attn_tile(q_ref[...], k, v, sink_ref, None)


def _attn_sample(q, k, v, sinks, row0, cache_k, cache_v):
    n_seq, w_rows, _ = cache_k.shape
    assert w_rows == WINDOW
    blk0 = row0 // CHUNK
    cache = pl.BlockSpec((1, w_rows, KV_W), lambda i: (i, 0, 0))
    cur = pl.BlockSpec((CHUNK, KV_W), lambda i: (blk0 + i, 0))
    return pl.pallas_call(
        _attn_sample_kernel,
        grid=(n_seq,),
        in_specs=[pl.BlockSpec(memory_space=pltpu.SMEM),
                  pl.BlockSpec((CHUNK, Q_W), lambda i: (blk0 + i, 0)), cache, cur, cache, cur],
        out_specs=pl.BlockSpec((CHUNK, Q_W), lambda i: (i, 0)),
        out_shape=jax.ShapeDtypeStruct((n_seq * CHUNK, Q_W), BF16),
        compiler_params=_params("parallel"),
        name="attn_sample",
    )(sinks, q, cache_k, k, cache_v, v)


def _merge_out(yc, o, gc, ga, x, wc_ref, wa_ref, wo_ref):
    m = gc.astype(F32) * _dot(yc, wc_ref[...]) + ga.astype(F32) * _dot(o, wa_ref[...])
    return x + _dot(m.astype(BF16), wo_ref[...])


def _mix_kernel(per_seq, sink_ref, u_ref, uh_ref, q_ref, kp_ref, kc_ref, vp_ref, vc_ref,
                cw_ref, cb_ref, lg_ref, lb_ref, gc_ref, ga_ref, x_ref, wc_ref, wa_ref, wo_ref,
                out_ref, s_ref, c_ref):
    first = pl.program_id(0) % per_seq == 0
    _conv_stage(s_ref, jnp.where(first, 0.0, uh_ref[...]), u_ref)

    k = jnp.concatenate([kp_ref[...], kc_ref[...]], axis=0).astype(BF16)
    v = jnp.concatenate([vp_ref[...], vc_ref[...]], axis=0).astype(BF16)
    nk = WINDOW + MIX_TILE
    col = lax.broadcasted_iota(jnp.int32, (MIX_TILE, nk), 1)
    d = col // CHUNK - lax.broadcasted_iota(jnp.int32, (MIX_TILE, nk), 0) // CHUNK
    band = (d >= 0) & (d <= WINDOW // CHUNK)

    for r0 in range(0, MIX_ROWS, MIX_TILE):
        rows = slice(r0, r0 + MIX_TILE)
        yc = _conv_tile(s_ref, r0, c_ref, cw_ref, cb_ref, lg_ref, lb_ref, MIX_TILE)
        valid = band & ((col >= WINDOW) | jnp.logical_not(first)) if r0 == 0 else band
        o = _attn_tile(q_ref[rows, :], k[r0:r0 + nk], v[r0:r0 + nk], sink_ref, valid)
        out_ref[rows, :] = _merge_out(yc, o, gc_ref[rows, :], ga_ref[rows, :], x_ref[rows, :],
                                      wc_ref, wa_ref, wo_ref)


def _mix_prompt(u, q, k, v, gates, x, sinks, cw, cb, lg, lb, wc, wa, wo, n_rows, seq):
    rows = MIX_ROWS
    halo_per_block = rows // CONV_HALO
    pre_per_block = rows // WINDOW
    kv_prev = pl.BlockSpec((WINDOW, KV_W), lambda i: (jnp.maximum(i * pre_per_block - 1, 0), 0))
    kv_cur = pl.BlockSpec((rows, KV_W), lambda i: (i, 0))
    return pl.pallas_call(
        functools.partial(_mix_kernel, seq // rows),
        grid=(n_rows // rows,),
        in_specs=[pl.BlockSpec(memory_space=pltpu.SMEM),
                  pl.BlockSpec((rows, D_CONV), lambda i: (i, 0)),
                  pl.BlockSpec((CONV_HALO, D_CONV), lambda i: (jnp.maximum(i * halo_per_block - 1, 0), 0)),
                  pl.BlockSpec((rows, Q_W), lambda i: (i, 0)),
                  kv_prev, kv_cur, kv_prev, kv_cur,
                  *_conv_common_specs(),
                  pl.BlockSpec((rows, D_MODEL), lambda i: (i, 0)),
                  pl.BlockSpec((rows, D_MODEL), lambda i: (i, 1)),
                  pl.BlockSpec((rows, D_MODEL), lambda i: (i, 0)),
                  _resident((D_CONV, D_MODEL)), _resident((Q_W, D_MODEL)), _resident((D_MODEL, D_MODEL))],
        out_specs=pl.BlockSpec((rows, D_MODEL), lambda i: (i, 0)),
        out_shape=jax.ShapeDtypeStruct((n_rows, D_MODEL), F32),
        scratch_shapes=_conv_scratch(rows, MIX_TILE),
        compiler_params=_params("parallel"),
        name="mix_prompt",
    )(sinks, u, u, q, k, k, v, v, cw, cb, lg, lb, gates, gates, x, wc, wa, wo)


def _post_sample_kernel(yc_ref, o_ref, gc_ref, ga_ref, x_ref, wc_ref, wa_ref, wo_ref, out_ref):
    out_ref[...] = _merge_out(yc_ref[...], o_ref[...], gc_ref[...], ga_ref[...], x_ref[...],
                              wc_ref, wa_ref, wo_ref)


def _post_sample(yc, o, gates, x, wc, wa, wo, row0):
    tm = POST_TILE
    blk0 = row0 // tm
    return pl.pallas_call(
        _post_sample_kernel,
        grid=(yc.shape[0] // tm,),
        in_specs=[pl.BlockSpec((tm, D_CONV), lambda i: (i, 0)),
                  pl.BlockSpec((tm, Q_W), lambda i: (i, 0)),
                  pl.BlockSpec((tm, D_MODEL), lambda i: (blk0 + i, 0)),
                  pl.BlockSpec((tm, D_MODEL), lambda i: (blk0 + i, 1)),
                  pl.BlockSpec((tm, D_MODEL), lambda i: (blk0 + i, 0)),
                  _resident((D_CONV, D_MODEL)), _resident((Q_W, D_MODEL)), _resident((D_MODEL, D_MODEL))],
        out_specs=pl.BlockSpec((tm, D_MODEL), lambda i: (i, 0)),
        out_shape=jax.ShapeDtypeStruct((yc.shape[0], D_MODEL), F32),
        compiler_params=_params("parallel"),
        name="post_sample",
    )(yc, o, gates, gates, x, wc, wa, wo)


def _ple_kernel(n_prompt_tiles, x_ref, pp_ref, ps_ref, gn_ref, wg_ref, wp_ref, fn_ref, yp_ref, ys_ref):
    i = pl.program_id(0)
    x = x_ref[...]
    gate = jax.nn.sigmoid(_dot(_rms(x, gn_ref[...]).astype(BF16), wg_ref[...]))
    pe = jnp.where(i < n_prompt_tiles, pp_ref[...], ps_ref[...]).astype(BF16)
    y = _rms(x + gate * _dot(pe, wp_ref[...]), fn_ref[...])

    @pl.when(i < n_prompt_tiles)
    def _():
        yp_ref[...] = y

    @pl.when(i >= n_prompt_tiles)
    def _():
        ys_ref[...] = y


def _ple(x, pp, ps, gn, wg, wp, fn):
    tm = PLE_TILE
    n_p, n_s = pp.shape[0] // tm, ps.shape[0] // tm
    vec = pl.BlockSpec((1, D_MODEL), lambda i: (0, 0))

    def p_idx(i):
        return (jnp.minimum(i, n_p - 1), 0)

    def s_idx(i):
        return (jnp.maximum(i - n_p, 0), 0)

    return pl.pallas_call(
        functools.partial(_ple_kernel, n_p),
        grid=(n_p + n_s,),
        in_specs=[pl.BlockSpec((tm, D_MODEL), lambda i: (i, 0)),
                  pl.BlockSpec((tm, D_PLE), p_idx),
                  pl.BlockSpec((tm, D_PLE), s_idx),
                  vec, _resident((D_MODEL, D_MODEL)), _resident((D_PLE, D_MODEL)), vec],
        out_specs=[pl.BlockSpec((tm, D_MODEL), p_idx), pl.BlockSpec((tm, D_MODEL), s_idx)],
        out_shape=[jax.ShapeDtypeStruct((pp.shape[0], D_MODEL), F32),
                   jax.ShapeDtypeStruct((ps.shape[0], D_MODEL), F32)],
        compiler_params=_params("arbitrary"),
        name="ple_final",
    )(x, pp, ps, gn, wg, wp, fn)


def kernel(x_prompt, x_sample, p_prompt, p_sample, state_conv, cache_k, cache_v, ffn1_norm, ffn1_w_gu, ffn1_w_down, mix_norm, w_in, conv_w, conv_b, conv_ln_g, conv_ln_b, conv_w_out, attn_sinks, attn_w_out, w_out, ffn2_norm, ffn2_w_gu, ffn2_w_down, ple_norm, ple_w_gate, ple_w_proj, final_norm):
    assert x_prompt.shape[-1] == D_MODEL and ffn1_norm.shape[0] == 1 and w_in.shape[-1] == IN_COLS
    batch, seq, _ = x_prompt.shape
    dec_batch, dec_seq, _ = x_sample.shape
    assert dec_seq == CHUNK and seq % ROW_TILE == 0
    n_p, n_s = batch * seq, dec_batch * dec_seq

    def vec(a):
        return a.reshape(1, -1)

    xp = x_prompt.reshape(n_p, D_MODEL)
    xs = x_sample.reshape(n_s, D_MODEL)

    x1_s, hn_s, wg1, wu1, wd1 = _ffn_first(xs, vec(ffn1_norm), ffn1_w_gu[0], ffn1_w_down[0], vec(mix_norm),
                                           True, "ffn1_first")
    x1, hn, wgu2, wd2, w_in_b = _ffn(
        xp, x1_s, hn_s, vec(ffn1_norm), (wg1, 0), (wu1, 0), wd1, vec(mix_norm), True, "ffn1",
        side=[(ffn2_w_gu[0], WGU_CAST_ROWS), (ffn2_w_down[0], WD_CAST_ROWS), (w_in[0], WGU_CAST_ROWS)])

    u = _glu(hn, w_in_b)
    cos, slo, shi = _rope_tables(batch, seq, dec_batch, dec_seq)
    q, k, v = _qkv(hn, w_in_b, cos, slo, shi)
    gate_steps = (n_p + n_s) // IN_TILE * (2 * D_MODEL // IN_COL_TILE)
    gates, w_co, w_ao, w_o, w_pg, w_pp = _gates(
        hn, w_in_b, side=[(w, max(w.shape[0] // gate_steps, BF16_ROWS))
                           for w in (conv_w_out[0], attn_w_out[0], w_out[0], ple_w_gate[0], ple_w_proj[0])])

    cw, cb, lg, lb = conv_w[0], vec(conv_b), vec(conv_ln_g), vec(conv_ln_b)
    sinks = attn_sinks[0]
    ck = cache_k[0].reshape(dec_batch, -1, KV_W)
    cv = cache_v[0].reshape(dec_batch, -1, KV_W)
    yc_s = _conv_sample(u, n_p, state_conv[0], cw, cb, lg, lb)
    o_s = _attn_sample(q, k, v, sinks, n_p, ck, cv)
    x2_p = _mix_prompt(u, q, k, v, gates, x1, sinks, cw, cb, lg, lb, w_co, w_ao, w_o, n_p, seq)
    x2_s = _post_sample(yc_s, o_s, gates, x1, w_co, w_ao, w_o, n_p)
    x3, = _ffn(x2_p, x2_s, None, vec(ffn2_norm), (wgu2, 0), (wgu2, D_FF), wd2, vec(ffn2_norm), False, "ffn2")

    yp, ys = _ple(x3, p_prompt[0].reshape(n_p, D_PLE), p_sample[0].reshape(n_s, D_PLE),
                  vec(ple_norm), w_pg, w_pp, vec(final_norm))

    w_rows = ck.shape[1]

    def prompt_tail(a, rows):
        return jnp.stack([a[(b + 1) * seq - rows:(b + 1) * seq] for b in range(batch)])

    k_p = prompt_tail(k, WINDOW).reshape(batch, WINDOW, N_KV, HEAD_DIM)
    v_p = prompt_tail(v, WINDOW).reshape(batch, WINDOW, N_KV, HEAD_DIM)
    c_p = prompt_tail(u, CONV_WIDTH - 1)
    k_s = jnp.concatenate([cache_k[0], k[n_p:].reshape(dec_batch, dec_seq, N_KV, HEAD_DIM)], 1)[:, -w_rows:]
    v_s = jnp.concatenate([cache_v[0], v[n_p:].reshape(dec_batch, dec_seq, N_KV, HEAD_DIM)], 1)[:, -w_rows:]
    c_s = jnp.concatenate([state_conv[0], u[n_p:].reshape(dec_batch, dec_seq, D_CONV)], 1)[:, -(CONV_WIDTH - 1):]
    return (yp.reshape(batch, seq, D_MODEL), ys.reshape(dec_batch, dec_seq, D_MODEL),
            k_p[None], v_p[None], c_p[None], k_s[None], v_s[None], c_s[None])
```

```python
import functools

import jax
import jax.numpy as jnp
from jax import lax
from jax.experimental import pallas as pl
from jax.experimental.pallas import tpu as pltpu

D_MODEL = 2048
D_PLE = 256
D_FF = 5504
D_CONV = 1024
CONV_WIDTH = 31
HEAD_DIM = 64
N_HEADS = 16
N_KV = 4
GROUP = N_HEADS // N_KV
ROT_DIM = 16
ROPE_THETA = 500000.0
CHUNK = 64
WINDOW = 128
PAST_LEN = 1024
EPS = 1e-6
NEG = -1e30
LOG2E = 1.4426950408889634
Q_W = N_HEADS * HEAD_DIM
KV_W = N_KV * HEAD_DIM
IN_COLS = 2 * D_CONV + Q_W + 2 * KV_W + 2 * D_MODEL

LANES = 128
SUBLANES = 8
BF16_ROWS = 16
FF_TILE = 512
FF_STEPS = -(-D_FF // FF_TILE)
CAST_TILE = 256
WGU_CAST_ROWS = 16
WD_CAST_ROWS = 32
ROW_TILE = 512
IN_TILE = 1088
IN_COL_TILE = 1024
UQKV_TILE = 544
SAMPLE_ROWS = 256
PLE_TILE = 512
CONV_HALO = 32
MIX_ROWS = 256
MIX_TILE = WINDOW
VMEM_LIMIT = 56 * 1024 * 1024

F32 = jnp.float32
BF16 = jnp.bfloat16


def _params(*sem):
    return pltpu.CompilerParams(dimension_semantics=sem, vmem_limit_bytes=VMEM_LIMIT)


def _rms(x, g):
    return x * lax.rsqrt(jnp.mean(x * x, axis=-1, keepdims=True) + EPS) * g


def _dot(a, b):
    return jnp.dot(a, b, preferred_element_type=F32)


def _resident(shape):
    return pl.BlockSpec(shape, lambda *_: (0,) * len(shape), pipeline_mode=pl.Buffered(1))


def _side_cast_specs(side, n_steps, step_of):
    in_specs, out_shape = [], []
    for a, rows in side:
        n_blocks = a.shape[0] // rows
        assert a.shape[0] % rows == 0 and n_blocks <= n_steps
        in_specs.append(pl.BlockSpec((rows, a.shape[1]),
                                     lambda *g, n=n_blocks: (jnp.minimum(step_of(*g), n - 1), 0)))
        out_shape.append(jax.ShapeDtypeStruct(a.shape, BF16))
    return in_specs, out_shape


def _side_cast(src_refs, dst_refs):
    for src, dst in zip(src_refs, dst_refs, strict=True):
        dst[...] = src[...].astype(BF16)


def _swiglu_accumulate(xn_ref, xo_ref, wg, wu, wd):
    xn = xn_ref[...]
    g = _dot(xn, wg)
    u = _dot(xn, wu)
    h = (g * jax.nn.sigmoid(g)) * (u * 0.5)
    xo_ref[...] += _dot(h.astype(BF16), wd)


def _ff_steps(tile):
    return -(-D_FF // tile)


def _ff_start(f, tile):
    return pl.multiple_of(jnp.minimum(f * tile, D_FF - tile), LANES)


def _ffn_first_kernel(with_hn, x_ref, g1_ref, wg_ref, wu_ref, wd_ref, g2_ref, xo_ref, *rest):
    hn_ref = rest[0] if with_hn else None
    wgb_ref, wub_ref, wdb_ref, xn_ref = rest[with_hn:]
    f = pl.program_id(0)
    steps = _ff_steps(CAST_TILE)
    wg, wu, wd = wg_ref[...].astype(BF16), wu_ref[...].astype(BF16), wd_ref[...].astype(BF16)

    @pl.when(f == 0)
    def _():
        x = x_ref[...]
        xn_ref[...] = _rms(x, g1_ref[...]).astype(BF16)
        xo_ref[...] = x

    @pl.when(f < steps - 1)
    def _():
        wgb_ref[...] = wg
        wub_ref[...] = wu
        wdb_ref[...] = wd
        _swiglu_accumulate(xn_ref, xo_ref, wg, wu, wd)

    @pl.when(f == steps - 1)
    def _():
        old = steps * CAST_TILE - D_FF
        wg_new, wu_new, wd_new = wg[:, old:], wu[:, old:], wd[old:, :]
        wgb_ref[...] = jnp.concatenate([wg_new, jnp.zeros((D_MODEL, old), BF16)], axis=1)
        wub_ref[...] = jnp.concatenate([wu_new, jnp.zeros((D_MODEL, old), BF16)], axis=1)
        wdb_ref[...] = jnp.concatenate([wd_new, jnp.zeros((old, D_MODEL), BF16)], axis=0)
        _swiglu_accumulate(xn_ref, xo_ref, wg_new, wu_new, wd_new)

    if with_hn:
        @pl.when(f == steps - 1)
        def _():
            hn_ref[...] = _rms(xo_ref[...], g2_ref[...]).astype(BF16)


def _ffn_first(x, g1, w_gu, w_down, g2, with_hn, name):
    assert x.shape[0] == ROW_TILE
    tile = CAST_TILE
    cols = (pl.Element(D_MODEL), pl.Element(tile))
    rows = (pl.Element(tile), pl.Element(D_MODEL))
    vec = pl.BlockSpec((1, D_MODEL), lambda f: (0, 0))
    row = pl.BlockSpec((ROW_TILE, D_MODEL), lambda f: (0, 0))
    gate_win = pl.BlockSpec(cols, lambda f: (0, _ff_start(f, tile)))
    up_win = pl.BlockSpec(cols, lambda f: (0, pl.multiple_of(D_FF + _ff_start(f, tile), LANES)))
    down_win = pl.BlockSpec(rows, lambda f: (_ff_start(f, tile), 0))
    out_shape = [jax.ShapeDtypeStruct((ROW_TILE, D_MODEL), F32)]
    if with_hn:
        out_shape.append(jax.ShapeDtypeStruct((ROW_TILE, D_MODEL), BF16))
    steps = _ff_steps(tile)
    weights_shape = ([jax.ShapeDtypeStruct((D_MODEL, steps * tile), BF16)] * 2
                     + [jax.ShapeDtypeStruct((steps * tile, D_MODEL), BF16)])
    col_blk = pl.BlockSpec((D_MODEL, tile), lambda f: (0, f))
    row_blk = pl.BlockSpec((tile, D_MODEL), lambda f: (f, 0))
    return pl.pallas_call(
        functools.partial(_ffn_first_kernel, with_hn),
        grid=(steps,),
        in_specs=[row, vec, gate_win, up_win, down_win, vec],
        out_specs=[row] * len(out_shape) + [col_blk, col_blk, row_blk],
        out_shape=out_shape + weights_shape,
        scratch_shapes=[pltpu.VMEM((ROW_TILE, D_MODEL), BF16)],
        compiler_params=_params("arbitrary"),
        name=name,
    )(x, g1, w_gu, w_gu, w_down, g2)


def _ffn_kernel(n_prompt_tiles, with_hn, tail_done, n_side, xp_ref, xs_ref, *rest):
    if tail_done:
        hns_ref, rest = rest[0], rest[1:]
    g1_ref, wg_ref, wu_ref, wd_ref, g2_ref = rest[:5]
    side_in, outs = rest[5:5 + n_side], rest[5 + n_side:]
    xo_ref = outs[0]
    hn_ref = outs[1] if with_hn else None
    side_out = outs[1 + with_hn:1 + with_hn + n_side]
    xn_ref = outs[-1]
    i, f = pl.program_id(0), pl.program_id(1)
    _side_cast(side_in, side_out)

    def on(cond):
        return jnp.logical_and(cond, i < n_prompt_tiles) if tail_done else cond

    @pl.when(on(f == 0))
    def _():
        x = xp_ref[...] if tail_done else jnp.where(i < n_prompt_tiles, xp_ref[...], xs_ref[...])
        xn_ref[...] = _rms(x, g1_ref[...]).astype(BF16)
        xo_ref[...] = x

    if tail_done:
        @pl.when(jnp.logical_and(f == 0, i >= n_prompt_tiles))
        def _():
            xo_ref[...] = xs_ref[...]
            hn_ref[...] = hns_ref[...]

    old = FF_STEPS * FF_TILE - D_FF
    pl.when(on(f < FF_STEPS - 1))(
        lambda: _swiglu_accumulate(xn_ref, xo_ref, wg_ref[...], wu_ref[...], wd_ref[...]))
    pl.when(on(f == FF_STEPS - 1))(
        lambda: _swiglu_accumulate(xn_ref, xo_ref, wg_ref[:, old:], wu_ref[:, old:], wd_ref[old:, :]))

    if with_hn:
        @pl.when(on(f == FF_STEPS - 1))
        def _():
            hn_ref[...] = _rms(xo_ref[...], g2_ref[...]).astype(BF16)


def _ffn(xp, xs, hn_s, g1, wg, wu, wd, g2, with_hn, name, side=()):
    n_p, n_s = xp.shape[0] // ROW_TILE, xs.shape[0] // ROW_TILE
    assert n_s == 1 and xs.shape[0] == ROW_TILE
    m = xp.shape[0] + xs.shape[0]
    tail_done = hn_s is not None
    assert with_hn or not tail_done

    def start(i, f):
        return _ff_start(jnp.where(i < n_p, f, FF_STEPS - 1) if tail_done else f, FF_TILE)

    cols = (pl.Element(D_MODEL), pl.Element(FF_TILE))
    rows = (pl.Element(FF_TILE), pl.Element(D_MODEL))
    vec = pl.BlockSpec((1, D_MODEL), lambda i, f: (0, 0))
    row = pl.BlockSpec((ROW_TILE, D_MODEL), lambda i, f: (i, 0))
    if tail_done:
        tail_specs, tail_args = [_resident((ROW_TILE, D_MODEL))] * 2, [xs, hn_s]
    else:
        tail_specs, tail_args = [pl.BlockSpec((ROW_TILE, D_MODEL), lambda i, f: (0, 0))], [xs]
    out_shape = [jax.ShapeDtypeStruct((m, D_MODEL), F32)]
    if with_hn:
        out_shape.append(jax.ShapeDtypeStruct((m, D_MODEL), BF16))
    side_specs, side_shape = _side_cast_specs(side, (n_p + n_s) * FF_STEPS, lambda i, f: i * FF_STEPS + f)
    return pl.pallas_call(
        functools.partial(_ffn_kernel, n_p, with_hn, tail_done, len(side)),
        grid=(n_p + n_s, FF_STEPS),
        in_specs=[pl.BlockSpec((ROW_TILE, D_MODEL), lambda i, f: (jnp.minimum(i, n_p - 1), 0)),
                  *tail_specs, vec,
                  pl.BlockSpec(cols, lambda i, f: (0, pl.multiple_of(wg[1] + start(i, f), LANES))),
                  pl.BlockSpec(cols, lambda i, f: (0, pl.multiple_of(wu[1] + start(i, f), LANES))),
                  pl.BlockSpec(rows, lambda i, f: (start(i, f), 0)),
                  vec, *side_specs],
        out_specs=[row] * len(out_shape) + side_specs,
        out_shape=out_shape + side_shape,
        scratch_shapes=[pltpu.VMEM((ROW_TILE, D_MODEL), BF16)],
        compiler_params=_params("arbitrary", "arbitrary"),
        name=name,
    )(xp, *tail_args, g1, wg[0], wu[0], wd, g2, *[a for a, _ in side])


def _gates_kernel(n_side, h_ref, w_ref, *rest):
    side_in, o_ref, side_out = rest[:n_side], rest[n_side], rest[n_side + 1:]
    _side_cast(side_in, side_out)
    o_ref[...] = jax.nn.sigmoid(_dot(h_ref[...], w_ref[...])).astype(BF16)


def _gates(hn, w_in, side=()):
    m, tn = hn.shape[0], IN_COL_TILE
    col0 = 2 * D_CONV + Q_W + 2 * KV_W
    nj = 2 * D_MODEL // tn
    side_specs, side_shape = _side_cast_specs(side, m // IN_TILE * nj, lambda i, j: i * nj + j)
    return pl.pallas_call(
        functools.partial(_gates_kernel, len(side)),
        grid=(m // IN_TILE, nj),
        in_specs=[pl.BlockSpec((IN_TILE, D_MODEL), lambda i, j: (i, 0)),
                  pl.BlockSpec((pl.Element(D_MODEL), pl.Element(tn)),
                               lambda i, j: (0, pl.multiple_of(col0 + j * tn, LANES))),
                  *side_specs],
        out_specs=[pl.BlockSpec((IN_TILE, tn), lambda i, j: (i, j))] + side_specs,
        out_shape=[jax.ShapeDtypeStruct((m, 2 * D_MODEL), BF16)] + side_shape,
        compiler_params=_params("arbitrary", "arbitrary"),
        name="gates",
    )(hn, w_in, *[a for a, _ in side])


def _rotate(x, cos, sin_lo, sin_hi):
    cols = []
    for c in range(x.shape[1] // LANES):
        xb = x[:, c * LANES:(c + 1) * LANES]
        cols.append(xb * cos
                    + pltpu.roll(xb, LANES - ROT_DIM // 2, 1) * sin_lo
                    + pltpu.roll(xb, ROT_DIM // 2, 1) * sin_hi)
    return jnp.concatenate(cols, axis=1)


def _uqkv_kernel(h_ref, w_ref, cos_ref, slo_ref, shi_ref, u_ref, q_ref, k_ref, v_ref):
    h = h_ref[...]
    cos, slo, shi = cos_ref[...], slo_ref[...], shi_ref[...]
    c0 = 2 * D_CONV
    u_ref[...] = _dot(h, w_ref[:, :D_CONV]) * jax.nn.sigmoid(_dot(h, w_ref[:, D_CONV:c0]))
    q = _rotate(_dot(h, w_ref[:, c0:c0 + Q_W]), cos, slo, shi)
    q_ref[...] = (q * (LOG2E * HEAD_DIM ** -0.5)).astype(BF16)
    k_ref[...] = _rotate(_dot(h, w_ref[:, c0 + Q_W:c0 + Q_W + KV_W]), cos, slo, shi)
    v_ref[...] = _dot(h, w_ref[:, c0 + Q_W + KV_W:])


def _uqkv(hn, w_in, cos, slo, shi):
    m, tm = hn.shape[0], UQKV_TILE
    n_w = 2 * D_CONV + Q_W + 2 * KV_W
    tab = pl.BlockSpec((tm, LANES), lambda i: (i, 0))
    return pl.pallas_call(
        _uqkv_kernel,
        grid=(m // tm,),
        in_specs=[pl.BlockSpec((tm, D_MODEL), lambda i: (i, 0)),
                  pl.BlockSpec((pl.Element(D_MODEL), pl.Element(n_w)), lambda i: (0, 0),
                               pipeline_mode=pl.Buffered(1)),
                  tab, tab, tab],
        out_specs=[pl.BlockSpec((tm, D_CONV), lambda i: (i, 0)),
                   pl.BlockSpec((tm, Q_W), lambda i: (i, 0)),
                   pl.BlockSpec((tm, KV_W), lambda i: (i, 0)),
                   pl.BlockSpec((tm, KV_W), lambda i: (i, 0))],
        out_shape=[jax.ShapeDtypeStruct((m, D_CONV), F32),
                   jax.ShapeDtypeStruct((m, Q_W), BF16),
                   jax.ShapeDtypeStruct((m, KV_W), F32),
                   jax.ShapeDtypeStruct((m, KV_W), F32)],
        compiler_params=_params("parallel"),
        name="uqkv",
    )(hn, w_in, cos, slo, shi)


def _rope_tables(batch, seq, dec_batch, dec_seq):
    inv = ROPE_THETA ** (-jnp.arange(0, ROT_DIM, 2, dtype=F32) / ROT_DIM)
    pos = jnp.concatenate([jnp.tile(jnp.arange(seq, dtype=jnp.int32), batch),
                           jnp.tile(PAST_LEN + jnp.arange(dec_seq, dtype=jnp.int32), dec_batch)])
    ang = pos.astype(F32)[:, None] * inv[None, :]
    cos, sin = jnp.cos(ang), jnp.sin(ang)
    one = jnp.ones((pos.shape[0], HEAD_DIM - ROT_DIM), F32)
    zero = jnp.zeros_like(one)
    zh = jnp.zeros_like(sin)
    c = jnp.concatenate([cos, cos, one], 1)
    lo = jnp.concatenate([-sin, zh, zero], 1)
    hi = jnp.concatenate([zh, sin, zero], 1)
    return [jnp.tile(t, (1, LANES // HEAD_DIM)) for t in (c, lo, hi)]


def _conv_stage(s_ref, history, u_ref):
    rows0 = CONV_HALO - history.shape[0]
    for cb in range(D_CONV // LANES):
        lanes = slice(cb * LANES, (cb + 1) * LANES)
        s_ref[cb, rows0:CONV_HALO, :] = history[:, lanes]
        s_ref[cb, CONV_HALO:, :] = u_ref[:, lanes]


def _conv_tile(s_ref, row0, c_ref, w_ref, b_ref, lg_ref, lb_ref, rows):
    base = CONV_HALO - (CONV_WIDTH - 1)
    n = rows // SUBLANES
    for cb in range(D_CONV // LANES):
        lanes = slice(cb * LANES, (cb + 1) * LANES)
        accs = [jnp.broadcast_to(b_ref[:, lanes], (n, LANES)) for _ in range(SUBLANES)]
        for j in range(CONV_WIDTH):
            w = jnp.broadcast_to(w_ref[j:j + 1, lanes], (n, LANES))
            for p in range(SUBLANES):
                accs[p] = accs[p] + s_ref[cb, pl.ds(row0 + base + j + p, n, stride=SUBLANES), :] * w
        for p in range(SUBLANES):
            c_ref[cb, pl.ds(p, n, stride=SUBLANES), :] = accs[p]
    acc = jnp.concatenate([c_ref[cb] for cb in range(D_CONV // LANES)], axis=1)
    mu = jnp.mean(acc, axis=-1, keepdims=True)
    xc = acc - mu
    y = xc * lax.rsqrt(jnp.mean(xc * xc, axis=-1, keepdims=True) + EPS) * lg_ref[...] + lb_ref[...]
    return (y * jax.nn.sigmoid(y)).astype(BF16)


def _conv_common_specs():
    return [pl.BlockSpec((CONV_WIDTH, D_CONV), lambda i: (0, 0)),
            pl.BlockSpec((1, D_CONV), lambda i: (0, 0)),
            pl.BlockSpec((1, D_CONV), lambda i: (0, 0)),
            pl.BlockSpec((1, D_CONV), lambda i: (0, 0))]


def _conv_scratch(block_rows, tile_rows):
    return [pltpu.VMEM((D_CONV // LANES, CONV_HALO + block_rows, LANES), F32),
            pltpu.VMEM((D_CONV // LANES, tile_rows, LANES), F32)]


def _attn_tile(q, k, v, sink_ref, valid):
    r = q.shape[0]
    lane_head = lax.broadcasted_iota(jnp.int32, (1, KV_W), 1) // HEAD_DIM
    sel_row = lax.broadcasted_iota(jnp.int32, (KV_W, Q_W), 0)
    sel_col = lax.broadcasted_iota(jnp.int32, (KV_W, Q_W), 1)
    sel = jnp.where(sel_row == sel_col // KV_W * HEAD_DIM + sel_col % HEAD_DIM, 1.0, 0.0).astype(BF16)
    krep = _dot(k, sel).astype(BF16)
    vrep = _dot(v, sel).astype(BF16)
    head_mask = [jnp.where(lane_head == g, 1.0, 0.0).astype(BF16) for g in range(GROUP)]
    scores = []
    for kv in range(N_KV):
        cols = slice(kv * KV_W, (kv + 1) * KV_W)
        lhs = jnp.concatenate([q[:, cols] * head_mask[g] for g in range(GROUP)], axis=0)
        scores.append(lax.dot_general(lhs, krep[:, cols], (((1,), (1,)), ((), ())),
                                      preferred_element_type=F32))
    weights, dens = [], []
    for kv in range(N_KV):
        es = []
        for g in range(GROUP):
            sg = scores[kv][g * r:(g + 1) * r]
            if valid is not None:
                sg = jnp.where(valid, sg, NEG)
            sink = sink_ref[kv * GROUP + g] * LOG2E
            mx = jnp.maximum(jnp.max(sg, axis=-1, keepdims=True), sink)
            e = jnp.exp2(sg - mx)
            dens.append(jnp.sum(e, axis=-1, keepdims=True) + jnp.exp2(sink - mx))
            es.append(e.astype(BF16))
        weights.append(jnp.concatenate(es, axis=0))
    outs = []
    for kv in range(N_KV):
        o4 = _dot(weights[kv], vrep[:, kv * KV_W:(kv + 1) * KV_W])
        og = [o4[g * r:(g + 1) * r] / dens[kv * GROUP + g] for g in range(GROUP)]
        oh = og[GROUP - 1]
        for g in range(GROUP - 2, -1, -1):
            oh = jnp.where(lane_head == g, og[g], oh)
        outs.append(oh)
    return jnp.concatenate(outs, axis=1).astype(BF16)


def _merge_out(yc, o, gc, ga, x, wc_ref, wa_ref, wo_ref):
    m = gc.astype(F32) * _dot(yc, wc_ref[...]) + ga.astype(F32) * _dot(o, wa_ref[...])
    return x + _dot(m.astype(BF16), wo_ref[...])


def _mix_kernel(per_seq, sink_ref, u_ref, uh_ref, q_ref, kp_ref, kc_ref, vp_ref, vc_ref,
                cw_ref, cb_ref, lg_ref, lb_ref, gc_ref, ga_ref, x_ref, wc_ref, wa_ref, wo_ref,
                out_ref, s_ref, c_ref):
    first = pl.program_id(0) % per_seq == 0
    _conv_stage(s_ref, jnp.where(first, 0.0, uh_ref[...]), u_ref)

    k = jnp.concatenate([kp_ref[...], kc_ref[...]], axis=0).astype(BF16)
    v = jnp.concatenate([vp_ref[...], vc_ref[...]], axis=0).astype(BF16)
    nk = WINDOW + MIX_TILE
    col = lax.broadcasted_iota(jnp.int32, (MIX_TILE, nk), 1)
    d = col // CHUNK - lax.broadcasted_iota(jnp.int32, (MIX_TILE, nk), 0) // CHUNK
    band = (d >= 0) & (d <= WINDOW // CHUNK)

    for r0 in range(0, MIX_ROWS, MIX_TILE):
        rows = slice(r0, r0 + MIX_TILE)
        yc = _conv_tile(s_ref, r0, c_ref, cw_ref, cb_ref, lg_ref, lb_ref, MIX_TILE)
        valid = band & ((col >= WINDOW) | jnp.logical_not(first)) if r0 == 0 else band
        o = _attn_tile(q_ref[rows, :], k[r0:r0 + nk], v[r0:r0 + nk], sink_ref, valid)
        out_ref[rows, :] = _merge_out(yc, o, gc_ref[rows, :], ga_ref[rows, :], x_ref[rows, :],
                                      wc_ref, wa_ref, wo_ref)


def _mix_prompt(u, q, k, v, gates, x, sinks, cw, cb, lg, lb, wc, wa, wo, n_rows, seq):
    rows = MIX_ROWS
    halo_per_block = rows // CONV_HALO
    pre_per_block = rows // WINDOW
    kv_prev = pl.BlockSpec((WINDOW, KV_W), lambda i: (jnp.maximum(i * pre_per_block - 1, 0), 0))
    kv_cur = pl.BlockSpec((rows, KV_W), lambda i: (i, 0))
    return pl.pallas_call(
        functools.partial(_mix_kernel, seq // rows),
        grid=(n_rows // rows,),
        in_specs=[pl.BlockSpec(memory_space=pltpu.SMEM),
                  pl.BlockSpec((rows, D_CONV), lambda i: (i, 0)),
                  pl.BlockSpec((CONV_HALO, D_CONV), lambda i: (jnp.maximum(i * halo_per_block - 1, 0), 0)),
                  pl.BlockSpec((rows, Q_W), lambda i: (i, 0)),
                  kv_prev, kv_cur, kv_prev, kv_cur,
                  *_conv_common_specs(),
                  pl.BlockSpec((rows, D_MODEL), lambda i: (i, 0)),
                  pl.BlockSpec((rows, D_MODEL), lambda i: (i, 1)),
                  pl.BlockSpec((rows, D_MODEL), lambda i: (i, 0)),
                  _resident((D_CONV, D_MODEL)), _resident((Q_W, D_MODEL)), _resident((D_MODEL, D_MODEL))],
        out_specs=pl.BlockSpec((rows, D_MODEL), lambda i: (i, 0)),
        out_shape=jax.ShapeDtypeStruct((n_rows, D_MODEL), F32),
        scratch_shapes=_conv_scratch(rows, MIX_TILE),
        compiler_params=_params("parallel"),
        name="mix_prompt",
    )(sinks, u, u, q, k, k, v, v, cw, cb, lg, lb, gates, gates, x, wc, wa, wo)


def _mix_sample_kernel(sink_ref, u_ref, past_ref, q_ref, kp_ref, kc_ref, vp_ref, vc_ref,
                       cw_ref, cb_ref, lg_ref, lb_ref, gc_ref, ga_ref, x_ref, wc_ref, wa_ref, wo_ref,
                       out_ref, s_ref, c_ref):
    ycs, os_ = [], []
    for s in range(u_ref.shape[0] // CHUNK):
        rows = slice(s * CHUNK, (s + 1) * CHUNK)
        _conv_stage(s_ref, past_ref[s], u_ref[rows, :])
        ycs.append(_conv_tile(s_ref, 0, c_ref, cw_ref, cb_ref, lg_ref, lb_ref, CHUNK))
        k = jnp.concatenate([kp_ref[s], kc_ref[rows, :]], axis=0).astype(BF16)
        v = jnp.concatenate([vp_ref[s], vc_ref[rows, :]], axis=0).astype(BF16)
        os_.append(_attn_tile(q_ref[rows, :], k, v, sink_ref, None))
    out_ref[...] = _merge_out(jnp.concatenate(ycs, axis=0), jnp.concatenate(os_, axis=0),
                              gc_ref[...], ga_ref[...], x_ref[...], wc_ref, wa_ref, wo_ref)


def _mix_sample(u, q, k, v, gates, x, sinks, cw, cb, lg, lb, wc, wa, wo, state, cache_k, cache_v, row0):
    n_seq, hist, _ = state.shape
    assert hist == CONV_WIDTH - 1 and cache_k.shape[1] == WINDOW
    rows = SAMPLE_ROWS
    seqs = rows // CHUNK
    blk0 = row0 // rows

    def cur(width, col=0):
        return pl.BlockSpec((rows, width), lambda i: (blk0 + i, col))

    cache = pl.BlockSpec((seqs, WINDOW, KV_W), lambda i: (i, 0, 0))
    return pl.pallas_call(
        _mix_sample_kernel,
        grid=(n_seq // seqs,),
        in_specs=[pl.BlockSpec(memory_space=pltpu.SMEM),
                  cur(D_CONV), pl.BlockSpec((seqs, hist, D_CONV), lambda i: (i, 0, 0)), cur(Q_W),
                  cache, cur(KV_W), cache, cur(KV_W),
                  *_conv_common_specs(),
                  cur(D_MODEL), cur(D_MODEL, 1), cur(D_MODEL),
                  _resident((D_CONV, D_MODEL)), _resident((Q_W, D_MODEL)), _resident((D_MODEL, D_MODEL))],
        out_specs=pl.BlockSpec((rows, D_MODEL), lambda i: (i, 0)),
        out_shape=jax.ShapeDtypeStruct((n_seq * CHUNK, D_MODEL), F32),
        scratch_shapes=_conv_scratch(CHUNK, CHUNK),
        compiler_params=_params("parallel"),
        name="mix_sample",
    )(sinks, u, state, q, cache_k, k, cache_v, v, cw, cb, lg, lb, gates, gates, x, wc, wa, wo)


def _ple_kernel(n_prompt_tiles, x_ref, pp_ref, ps_ref, gn_ref, wg_ref, wp_ref, fn_ref, yp_ref, ys_ref):
    i = pl.program_id(0)
    x = x_ref[...]
    gate = jax.nn.sigmoid(_dot(_rms(x, gn_ref[...]).astype(BF16), wg_ref[...]))
    pe = jnp.where(i < n_prompt_tiles, pp_ref[...], ps_ref[...]).astype(BF16)
    y = _rms(x + gate * _dot(pe, wp_ref[...]), fn_ref[...])

    @pl.when(i < n_prompt_tiles)
    def _():
        yp_ref[...] = y

    @pl.when(i >= n_prompt_tiles)
    def _():
        ys_ref[...] = y


def _ple(x, pp, ps, gn, wg, wp, fn):
    tm = PLE_TILE
    n_p, n_s = pp.shape[0] // tm, ps.shape[0] // tm
    vec = pl.BlockSpec((1, D_MODEL), lambda i: (0, 0))

    def p_idx(i):
        return (jnp.minimum(i, n_p - 1), 0)

    def s_idx(i):
        return (jnp.maximum(i - n_p, 0), 0)

    return pl.pallas_call(
        functools.partial(_ple_kernel, n_p),
        grid=(n_p + n_s,),
        in_specs=[pl.BlockSpec((tm, D_MODEL), lambda i: (i, 0)),
                  pl.BlockSpec((tm, D_PLE), p_idx),
                  pl.BlockSpec((tm, D_PLE), s_idx),
                  vec, _resident((D_MODEL, D_MODEL)), _resident((D_PLE, D_MODEL)), vec],
        out_specs=[pl.BlockSpec((tm, D_MODEL), p_idx), pl.BlockSpec((tm, D_MODEL), s_idx)],
        out_shape=[jax.ShapeDtypeStruct((pp.shape[0], D_MODEL), F32),
                   jax.ShapeDtypeStruct((ps.shape[0], D_MODEL), F32)],
        compiler_params=_params("arbitrary"),
        name="ple_final",
    )(x, pp, ps, gn, wg, wp, fn)


def kernel(x_prompt, x_sample, p_prompt, p_sample, state_conv, cache_k, cache_v, ffn1_norm, ffn1_w_gu, ffn1_w_down, mix_norm, w_in, conv_w, conv_b, conv_ln_g, conv_ln_b, conv_w_out, attn_sinks, attn_w_out, w_out, ffn2_norm, ffn2_w_gu, ffn2_w_down, ple_norm, ple_w_gate, ple_w_proj, final_norm):
    assert x_prompt.shape[-1] == D_MODEL and ffn1_norm.shape[0] == 1 and w_in.shape[-1] == IN_COLS
    batch, seq, _ = x_prompt.shape
    dec_batch, dec_seq, _ = x_sample.shape
    assert dec_seq == CHUNK and seq % ROW_TILE == 0
    n_p, n_s = batch * seq, dec_batch * dec_seq

    def vec(a):
        return a.reshape(1, -1)

    xp = x_prompt.reshape(n_p, D_MODEL)
    xs = x_sample.reshape(n_s, D_MODEL)

    x1_s, hn_s, wg1, wu1, wd1 = _ffn_first(xs, vec(ffn1_norm), ffn1_w_gu[0], ffn1_w_down[0], vec(mix_norm),
                                           True, "ffn1_first")
    x1, hn, wgu2, wd2, w_in_b = _ffn(
        xp, x1_s, hn_s, vec(ffn1_norm), (wg1, 0), (wu1, 0), wd1, vec(mix_norm), True, "ffn1",
        side=[(ffn2_w_gu[0], WGU_CAST_ROWS), (ffn2_w_down[0], WD_CAST_ROWS), (w_in[0], WGU_CAST_ROWS)])

    cos, slo, shi = _rope_tables(batch, seq, dec_batch, dec_seq)
    u, q, k, v = _uqkv(hn, w_in_b, cos, slo, shi)
    gate_steps = (n_p + n_s) // IN_TILE * (2 * D_MODEL // IN_COL_TILE)
    gates, w_co, w_ao, w_o, w_pg, w_pp = _gates(
        hn, w_in_b, side=[(w, max(w.shape[0] // gate_steps, BF16_ROWS))
                           for w in (conv_w_out[0], attn_w_out[0], w_out[0], ple_w_gate[0], ple_w_proj[0])])

    cw, cb, lg, lb = conv_w[0], vec(conv_b), vec(conv_ln_g), vec(conv_ln_b)
    sinks = attn_sinks[0]
    ck = cache_k[0].reshape(dec_batch, -1, KV_W)
    cv = cache_v[0].reshape(dec_batch, -1, KV_W)
    mix_args = (u, q, k, v, gates, x1, sinks, cw, cb, lg, lb, w_co, w_ao, w_o)
    x2_p = _mix_prompt(*mix_args, n_p, seq)
    x2_s = _mix_sample(*mix_args, state_conv[0], ck, cv, n_p)
    x3, = _ffn(x2_p, x2_s, None, vec(ffn2_norm), (wgu2, 0), (wgu2, D_FF), wd2, vec(ffn2_norm), False, "ffn2")

    yp, ys = _ple(x3, p_prompt[0].reshape(n_p, D_PLE), p_sample[0].reshape(n_s, D_PLE),
                  vec(ple_norm), w_pg, w_pp, vec(final_norm))

    w_rows = ck.shape[1]

    def prompt_tail(a, rows):
        return jnp.stack([a[(b + 1) * seq - rows:(b + 1) * seq] for b in range(batch)])

    k_p = prompt_tail(k, WINDOW).reshape(batch, WINDOW, N_KV, HEAD_DIM)
    v_p = prompt_tail(v, WINDOW).reshape(batch, WINDOW, N_KV, HEAD_DIM)
    c_p = prompt_tail(u, CONV_WIDTH - 1)
    k_s = jnp.concatenate([cache_k[0], k[n_p:].reshape(dec_batch, dec_seq, N_KV, HEAD_DIM)], 1)[:, -w_rows:]
    v_s = jnp.concatenate([cache_v[0], v[n_p:].reshape(dec_batch, dec_seq, N_KV, HEAD_DIM)], 1)[:, -w_rows:]
    c_s = jnp.concatenate([state_conv[0], u[n_p:].reshape(dec_batch, dec_seq, D_CONV)], 1)[:, -(CONV_WIDTH - 1):]
    return (yp.reshape(batch, seq, D_MODEL), ys.reshape(dec_batch, dec_seq, D_MODEL),
            k_p[None], v_p[None], c_p[None], k_s[None], v_s[None], c_s[None])
```

```python
import functools

import jax
import jax.numpy as jnp
from jax import lax
from jax.experimental import pallas as pl
from jax.experimental.pallas import tpu as pltpu

D_MODEL = 2048
D_PLE = 256
D_FF = 5504
D_CONV = 1024
CONV_WIDTH = 31
HEAD_DIM = 64
N_HEADS = 16
N_KV = 4
GROUP = N_HEADS // N_KV
ROT_DIM = 16
ROPE_THETA = 500000.0
CHUNK = 64
WINDOW = 128
PAST_LEN = 1024
EPS = 1e-6
NEG = -1e30
LOG2E = 1.4426950408889634
Q_W = N_HEADS * HEAD_DIM
KV_W = N_KV * HEAD_DIM
IN_COLS = 2 * D_CONV + Q_W + 2 * KV_W + 2 * D_MODEL

LANES = 128
SUBLANES = 8
BF16_ROWS = 16
FF_TILE = 512
FF_STEPS = -(-D_FF // FF_TILE)
CAST_TILE = 256
WGU_CAST_ROWS = 16
WD_CAST_ROWS = 32
ROW_TILE = 512
IN_TILE = 1088
IN_COL_TILE = 2048
UQKV_TILE = 544
SAMPLE_ROWS = 256
PLE_TILE = 512
CONV_HALO = 32
MIX_ROWS = 256
MIX_TILE = WINDOW
VMEM_LIMIT = 56 * 1024 * 1024

F32 = jnp.float32
BF16 = jnp.bfloat16


def _params(*sem):
    return pltpu.CompilerParams(dimension_semantics=sem, vmem_limit_bytes=VMEM_LIMIT)


def _rms(x, g):
    return x * lax.rsqrt(jnp.mean(x * x, axis=-1, keepdims=True) + EPS) * g


def _dot(a, b):
    return jnp.dot(a, b, preferred_element_type=F32)


def _resident(shape):
    return pl.BlockSpec(shape, lambda *_: (0,) * len(shape), pipeline_mode=pl.Buffered(1))


def _side_cast_specs(side, n_steps, step_of):
    in_specs, out_shape = [], []
    for a, rows in side:
        n_blocks = a.shape[0] // rows
        assert a.shape[0] % rows == 0 and n_blocks <= n_steps
        in_specs.append(pl.BlockSpec((rows, a.shape[1]),
                                     lambda *g, n=n_blocks: (jnp.minimum(step_of(*g), n - 1), 0)))
        out_shape.append(jax.ShapeDtypeStruct(a.shape, BF16))
    return in_specs, out_shape


def _side_cast(src_refs, dst_refs):
    for src, dst in zip(src_refs, dst_refs, strict=True):
        dst[...] = src[...].astype(BF16)


def _swiglu_accumulate(xn_ref, xo_ref, wg, wu, wd):
    xn = xn_ref[...]
    g = _dot(xn, wg)
    u = _dot(xn, wu)
    h = (g * jax.nn.sigmoid(g)) * (u * 0.5)
    xo_ref[...] += _dot(h.astype(BF16), wd)


def _ff_steps(tile):
    return -(-D_FF // tile)


def _ff_start(f, tile):
    return pl.multiple_of(jnp.minimum(f * tile, D_FF - tile), LANES)


def _ffn_first_kernel(with_hn, x_ref, g1_ref, wg_ref, wu_ref, wd_ref, g2_ref, xo_ref, *rest):
    hn_ref = rest[0] if with_hn else None
    wgb_ref, wub_ref, wdb_ref, xn_ref = rest[with_hn:]
    f = pl.program_id(0)
    steps = _ff_steps(CAST_TILE)
    wg, wu, wd = wg_ref[...].astype(BF16), wu_ref[...].astype(BF16), wd_ref[...].astype(BF16)

    @pl.when(f == 0)
    def _():
        x = x_ref[...]
        xn_ref[...] = _rms(x, g1_ref[...]).astype(BF16)
        xo_ref[...] = x

    @pl.when(f < steps - 1)
    def _():
        wgb_ref[...] = wg
        wub_ref[...] = wu
        wdb_ref[...] = wd
        _swiglu_accumulate(xn_ref, xo_ref, wg, wu, wd)

    @pl.when(f == steps - 1)
    def _():
        old = steps * CAST_TILE - D_FF
        wg_new, wu_new, wd_new = wg[:, old:], wu[:, old:], wd[old:, :]
        wgb_ref[...] = jnp.concatenate([wg_new, jnp.zeros((D_MODEL, old), BF16)], axis=1)
        wub_ref[...] = jnp.concatenate([wu_new, jnp.zeros((D_MODEL, old), BF16)], axis=1)
        wdb_ref[...] = jnp.concatenate([wd_new, jnp.zeros((old, D_MODEL), BF16)], axis=0)
        _swiglu_accumulate(xn_ref, xo_ref, wg_new, wu_new, wd_new)

    if with_hn:
        @pl.when(f == steps - 1)
        def _():
            hn_ref[...] = _rms(xo_ref[...], g2_ref[...]).astype(BF16)


def _ffn_first(x, g1, w_gu, w_down, g2, with_hn, name):
    assert x.shape[0] == ROW_TILE
    tile = CAST_TILE
    cols = (pl.Element(D_MODEL), pl.Element(tile))
    rows = (pl.Element(tile), pl.Element(D_MODEL))
    vec = pl.BlockSpec((1, D_MODEL), lambda f: (0, 0))
    row = pl.BlockSpec((ROW_TILE, D_MODEL), lambda f: (0, 0))
    gate_win = pl.BlockSpec(cols, lambda f: (0, _ff_start(f, tile)))
    up_win = pl.BlockSpec(cols, lambda f: (0, pl.multiple_of(D_FF + _ff_start(f, tile), LANES)))
    down_win = pl.BlockSpec(rows, lambda f: (_ff_start(f, tile), 0))
    out_shape = [jax.ShapeDtypeStruct((ROW_TILE, D_MODEL), F32)]
    if with_hn:
        out_shape.append(jax.ShapeDtypeStruct((ROW_TILE, D_MODEL), BF16))
    steps = _ff_steps(tile)
    weights_shape = ([jax.ShapeDtypeStruct((D_MODEL, steps * tile), BF16)] * 2
                     + [jax.ShapeDtypeStruct((steps * tile, D_MODEL), BF16)])
    col_blk = pl.BlockSpec((D_MODEL, tile), lambda f: (0, f))
    row_blk = pl.BlockSpec((tile, D_MODEL), lambda f: (f, 0))
    return pl.pallas_call(
        functools.partial(_ffn_first_kernel, with_hn),
        grid=(steps,),
        in_specs=[row, vec, gate_win, up_win, down_win, vec],
        out_specs=[row] * len(out_shape) + [col_blk, col_blk, row_blk],
        out_shape=out_shape + weights_shape,
        scratch_shapes=[pltpu.VMEM((ROW_TILE, D_MODEL), BF16)],
        compiler_params=_params("arbitrary"),
        name=name,
    )(x, g1, w_gu, w_gu, w_down, g2)


def _ffn_kernel(n_prompt_tiles, with_hn, tail_done, n_side, xp_ref, xs_ref, *rest):
    if tail_done:
        hns_ref, rest = rest[0], rest[1:]
    g1_ref, wg_ref, wu_ref, wd_ref, g2_ref = rest[:5]
    side_in, outs = rest[5:5 + n_side], rest[5 + n_side:]
    xo_ref = outs[0]
    hn_ref = outs[1] if with_hn else None
    side_out = outs[1 + with_hn:1 + with_hn + n_side]
    xn_ref = outs[-1]
    i, f = pl.program_id(0), pl.program_id(1)
    _side_cast(side_in, side_out)

    def on(cond):
        return jnp.logical_and(cond, i < n_prompt_tiles) if tail_done else cond

    @pl.when(on(f == 0))
    def _():
        x = xp_ref[...] if tail_done else jnp.where(i < n_prompt_tiles, xp_ref[...], xs_ref[...])
        xn_ref[...] = _rms(x, g1_ref[...]).astype(BF16)
        xo_ref[...] = x

    if tail_done:
        @pl.when(jnp.logical_and(f == 0, i >= n_prompt_tiles))
        def _():
            xo_ref[...] = xs_ref[...]
            hn_ref[...] = hns_ref[...]

    old = FF_STEPS * FF_TILE - D_FF
    pl.when(on(f < FF_STEPS - 1))(
        lambda: _swiglu_accumulate(xn_ref, xo_ref, wg_ref[...], wu_ref[...], wd_ref[...]))
    pl.when(on(f == FF_STEPS - 1))(
        lambda: _swiglu_accumulate(xn_ref, xo_ref, wg_ref[:, old:], wu_ref[:, old:], wd_ref[old:, :]))

    if with_hn:
        @pl.when(on(f == FF_STEPS - 1))
        def _():
            hn_ref[...] = _rms(xo_ref[...], g2_ref[...]).astype(BF16)


def _ffn(xp, xs, hn_s, g1, wg, wu, wd, g2, with_hn, name, side=()):
    n_p, n_s = xp.shape[0] // ROW_TILE, xs.shape[0] // ROW_TILE
    assert n_s == 1 and xs.shape[0] == ROW_TILE
    m = xp.shape[0] + xs.shape[0]
    tail_done = hn_s is not None
    assert with_hn or not tail_done

    def start(i, f):
        return _ff_start(jnp.where(i < n_p, f, FF_STEPS - 1) if tail_done else f, FF_TILE)

    cols = (pl.Element(D_MODEL), pl.Element(FF_TILE))
    rows = (pl.Element(FF_TILE), pl.Element(D_MODEL))
    vec = pl.BlockSpec((1, D_MODEL), lambda i, f: (0, 0))
    row = pl.BlockSpec((ROW_TILE, D_MODEL), lambda i, f: (i, 0))
    if tail_done:
        tail_specs, tail_args = [_resident((ROW_TILE, D_MODEL))] * 2, [xs, hn_s]
    else:
        tail_specs, tail_args = [pl.BlockSpec((ROW_TILE, D_MODEL), lambda i, f: (0, 0))], [xs]
    out_shape = [jax.ShapeDtypeStruct((m, D_MODEL), F32)]
    if with_hn:
        out_shape.append(jax.ShapeDtypeStruct((m, D_MODEL), BF16))
    side_specs, side_shape = _side_cast_specs(side, (n_p + n_s) * FF_STEPS, lambda i, f: i * FF_STEPS + f)
    return pl.pallas_call(
        functools.partial(_ffn_kernel, n_p, with_hn, tail_done, len(side)),
        grid=(n_p + n_s, FF_STEPS),
        in_specs=[pl.BlockSpec((ROW_TILE, D_MODEL), lambda i, f: (jnp.minimum(i, n_p - 1), 0)),
                  *tail_specs, vec,
                  pl.BlockSpec(cols, lambda i, f: (0, pl.multiple_of(wg[1] + start(i, f), LANES))),
                  pl.BlockSpec(cols, lambda i, f: (0, pl.multiple_of(wu[1] + start(i, f), LANES))),
                  pl.BlockSpec(rows, lambda i, f: (start(i, f), 0)),
                  vec, *side_specs],
        out_specs=[row] * len(out_shape) + side_specs,
        out_shape=out_shape + side_shape,
        scratch_shapes=[pltpu.VMEM((ROW_TILE, D_MODEL), BF16)],
        compiler_params=_params("arbitrary", "arbitrary"),
        name=name,
    )(xp, *tail_args, g1, wg[0], wu[0], wd, g2, *[a for a, _ in side])


def _gates_kernel(n_side, h_ref, w_ref, *rest):
    side_in, o_ref, side_out = rest[:n_side], rest[n_side], rest[n_side + 1:]
    _side_cast(side_in, side_out)
    o_ref[...] = jax.nn.sigmoid(_dot(h_ref[...], w_ref[...])).astype(BF16)


def _gates(hn, w_in, side=()):
    m, tn = hn.shape[0], IN_COL_TILE
    col0 = 2 * D_CONV + Q_W + 2 * KV_W
    nj = 2 * D_MODEL // tn
    side_specs, side_shape = _side_cast_specs(side, m // IN_TILE * nj, lambda i, j: i * nj + j)
    return pl.pallas_call(
        functools.partial(_gates_kernel, len(side)),
        grid=(m // IN_TILE, nj),
        in_specs=[pl.BlockSpec((IN_TILE, D_MODEL), lambda i, j: (i, 0)),
                  pl.BlockSpec((pl.Element(D_MODEL), pl.Element(tn)),
                               lambda i, j: (0, pl.multiple_of(col0 + j * tn, LANES))),
                  *side_specs],
        out_specs=[pl.BlockSpec((IN_TILE, tn), lambda i, j: (i, j))] + side_specs,
        out_shape=[jax.ShapeDtypeStruct((m, 2 * D_MODEL), BF16)] + side_shape,
        compiler_params=_params("arbitrary", "arbitrary"),
        name="gates",
    )(hn, w_in, *[a for a, _ in side])


def _rotate(x, cos, sin_lo, sin_hi):
    cols = []
    for c in range(x.shape[1] // LANES):
        xb = x[:, c * LANES:(c + 1) * LANES]
        cols.append(xb * cos
                    + pltpu.roll(xb, LANES - ROT_DIM // 2, 1) * sin_lo
                    + pltpu.roll(xb, ROT_DIM // 2, 1) * sin_hi)
    return jnp.concatenate(cols, axis=1)


def _uqkv_kernel(h_ref, w_ref, cos_ref, slo_ref, shi_ref, u_ref, q_ref, k_ref, v_ref):
    h = h_ref[...]
    cos, slo, shi = cos_ref[...], slo_ref[...], shi_ref[...]
    c0 = 2 * D_CONV
    u_ref[...] = _dot(h, w_ref[:, :D_CONV]) * jax.nn.sigmoid(_dot(h, w_ref[:, D_CONV:c0]))
    q = _rotate(_dot(h, w_ref[:, c0:c0 + Q_W]), cos, slo, shi)
    q_ref[...] = (q * (LOG2E * HEAD_DIM ** -0.5)).astype(BF16)
    k_ref[...] = _rotate(_dot(h, w_ref[:, c0 + Q_W:c0 + Q_W + KV_W]), cos, slo, shi)
    v_ref[...] = _dot(h, w_ref[:, c0 + Q_W + KV_W:])


def _uqkv(hn, w_in, cos, slo, shi):
    m, tm = hn.shape[0], UQKV_TILE
    n_w = 2 * D_CONV + Q_W + 2 * KV_W
    tab = pl.BlockSpec((tm, LANES), lambda i: (i, 0))
    return pl.pallas_call(
        _uqkv_kernel,
        grid=(m // tm,),
        in_specs=[pl.BlockSpec((tm, D_MODEL), lambda i: (i, 0)),
                  pl.BlockSpec((pl.Element(D_MODEL), pl.Element(n_w)), lambda i: (0, 0),
                               pipeline_mode=pl.Buffered(1)),
                  tab, tab, tab],
        out_specs=[pl.BlockSpec((tm, D_CONV), lambda i: (i, 0)),
                   pl.BlockSpec((tm, Q_W), lambda i: (i, 0)),
                   pl.BlockSpec((tm, KV_W), lambda i: (i, 0)),
                   pl.BlockSpec((tm, KV_W), lambda i: (i, 0))],
        out_shape=[jax.ShapeDtypeStruct((m, D_CONV), F32),
                   jax.ShapeDtypeStruct((m, Q_W), BF16),
                   jax.ShapeDtypeStruct((m, KV_W), F32),
                   jax.ShapeDtypeStruct((m, KV_W), F32)],
        compiler_params=_params("parallel"),
        name="uqkv",
    )(hn, w_in, cos, slo, shi)


def _rope_tables(batch, seq, dec_batch, dec_seq):
    half = ROT_DIM // 2
    lane = jnp.arange(LANES, dtype=jnp.int32) % HEAD_DIM
    freq = ROPE_THETA ** (-(2 * (lane % half)).astype(F32) / ROT_DIM)
    inv = jnp.where(lane < ROT_DIM, freq, 0.0)
    pos = jnp.concatenate([jnp.tile(jnp.arange(seq, dtype=jnp.int32), batch),
                           jnp.tile(PAST_LEN + jnp.arange(dec_seq, dtype=jnp.int32), dec_batch)])
    ang = pos.astype(F32)[:, None] * inv[None, :]
    cos, sin = jnp.cos(ang), jnp.sin(ang)
    lo = jnp.where(lane < half, -sin, 0.0)
    hi = jnp.where(lane >= half, sin, 0.0)
    return cos, lo, hi


def _conv_stage(s_ref, history, u_ref):
    rows0 = CONV_HALO - history.shape[0]
    for cb in range(D_CONV // LANES):
        lanes = slice(cb * LANES, (cb + 1) * LANES)
        s_ref[cb, rows0:CONV_HALO, :] = history[:, lanes]
        s_ref[cb, CONV_HALO:, :] = u_ref[:, lanes]


def _conv_tile(s_ref, row0, c_ref, w_ref, b_ref, lg_ref, lb_ref, rows):
    base = CONV_HALO - (CONV_WIDTH - 1)
    n = rows // SUBLANES
    for cb in range(D_CONV // LANES):
        lanes = slice(cb * LANES, (cb + 1) * LANES)
        accs = [jnp.broadcast_to(b_ref[:, lanes], (n, LANES)) for _ in range(SUBLANES)]
        for j in range(CONV_WIDTH):
            w = jnp.broadcast_to(w_ref[j:j + 1, lanes], (n, LANES))
            for p in range(SUBLANES):
                accs[p] = accs[p] + s_ref[cb, pl.ds(row0 + base + j + p, n, stride=SUBLANES), :] * w
        for p in range(SUBLANES):
            c_ref[cb, pl.ds(p, n, stride=SUBLANES), :] = accs[p]
    acc = jnp.concatenate([c_ref[cb] for cb in range(D_CONV // LANES)], axis=1)
    mu = jnp.mean(acc, axis=-1, keepdims=True)
    xc = acc - mu
    y = xc * lax.rsqrt(jnp.mean(xc * xc, axis=-1, keepdims=True) + EPS) * lg_ref[...] + lb_ref[...]
    return (y * jax.nn.sigmoid(y)).astype(BF16)


def _conv_common_specs():
    return [pl.BlockSpec((CONV_WIDTH, D_CONV), lambda i: (0, 0)),
            pl.BlockSpec((1, D_CONV), lambda i: (0, 0)),
            pl.BlockSpec((1, D_CONV), lambda i: (0, 0)),
            pl.BlockSpec((1, D_CONV), lambda i: (0, 0))]


def _conv_scratch(block_rows, tile_rows):
    return [pltpu.VMEM((D_CONV // LANES, CONV_HALO + block_rows, LANES), F32),
            pltpu.VMEM((D_CONV // LANES, tile_rows, LANES), F32)]


def _attn_tile(q, k, v, sink_ref, valid):
    r = q.shape[0]
    lane_head = lax.broadcasted_iota(jnp.int32, (1, KV_W), 1) // HEAD_DIM
    sel_row = lax.broadcasted_iota(jnp.int32, (KV_W, Q_W), 0)
    sel_col = lax.broadcasted_iota(jnp.int32, (KV_W, Q_W), 1)
    sel = jnp.where(sel_row == sel_col // KV_W * HEAD_DIM + sel_col % HEAD_DIM, 1.0, 0.0).astype(BF16)
    krep = _dot(k, sel).astype(BF16)
    vrep = _dot(v, sel).astype(BF16)
    head_mask = [jnp.where(lane_head == g, 1.0, 0.0).astype(BF16) for g in range(GROUP)]
    scores = []
    for kv in range(N_KV):
        cols = slice(kv * KV_W, (kv + 1) * KV_W)
        lhs = jnp.concatenate([q[:, cols] * head_mask[g] for g in range(GROUP)], axis=0)
        scores.append(lax.dot_general(lhs, krep[:, cols], (((1,), (1,)), ((), ())),
                                      preferred_element_type=F32))
    weights, dens = [], []
    for kv in range(N_KV):
        es = []
        for g in range(GROUP):
            sg = scores[kv][g * r:(g + 1) * r]
            if valid is not None:
                sg = jnp.where(valid, sg, NEG)
            sink = sink_ref[kv * GROUP + g] * LOG2E
            mx = jnp.maximum(jnp.max(sg, axis=-1, keepdims=True), sink)
            e = jnp.exp2(sg - mx)
            dens.append(jnp.sum(e, axis=-1, keepdims=True) + jnp.exp2(sink - mx))
            es.append(e.astype(BF16))
        weights.append(jnp.concatenate(es, axis=0))
    outs = []
    for kv in range(N_KV):
        o4 = _dot(weights[kv], vrep[:, kv * KV_W:(kv + 1) * KV_W])
        og = [o4[g * r:(g + 1) * r] / dens[kv * GROUP + g] for g in range(GROUP)]
        oh = og[GROUP - 1]
        for g in range(GROUP - 2, -1, -1):
            oh = jnp.where(lane_head == g, og[g], oh)
        outs.append(oh)
    return jnp.concatenate(outs, axis=1).astype(BF16)


def _merge_out(yc, o, gc, ga, x, wc_ref, wa_ref, wo_ref):
    m = gc.astype(F32) * _dot(yc, wc_ref[...]) + ga.astype(F32) * _dot(o, wa_ref[...])
    return x + _dot(m.astype(BF16), wo_ref[...])


def _mix_kernel(per_seq, sink_ref, u_ref, uh_ref, q_ref, kp_ref, kc_ref, vp_ref, vc_ref,
                cw_ref, cb_ref, lg_ref, lb_ref, gc_ref, ga_ref, x_ref, wc_ref, wa_ref, wo_ref,
                out_ref, s_ref, c_ref):
    first = pl.program_id(0) % per_seq == 0
    _conv_stage(s_ref, jnp.where(first, 0.0, uh_ref[...]), u_ref)

    k = jnp.concatenate([kp_ref[...], kc_ref[...]], axis=0).astype(BF16)
    v = jnp.concatenate([vp_ref[...], vc_ref[...]], axis=0).astype(BF16)
    nk = WINDOW + MIX_TILE
    col = lax.broadcasted_iota(jnp.int32, (MIX_TILE, nk), 1)
    d = col // CHUNK - lax.broadcasted_iota(jnp.int32, (MIX_TILE, nk), 0) // CHUNK
    band = (d >= 0) & (d <= WINDOW // CHUNK)

    for r0 in range(0, MIX_ROWS, MIX_TILE):
        rows = slice(r0, r0 + MIX_TILE)
        yc = _conv_tile(s_ref, r0, c_ref, cw_ref, cb_ref, lg_ref, lb_ref, MIX_TILE)
        valid = band & ((col >= WINDOW) | jnp.logical_not(first)) if r0 == 0 else band
        o = _attn_tile(q_ref[rows, :], k[r0:r0 + nk], v[r0:r0 + nk], sink_ref, valid)
        out_ref[rows, :] = _merge_out(yc, o, gc_ref[rows, :], ga_ref[rows, :], x_ref[rows, :],
                                      wc_ref, wa_ref, wo_ref)


def _mix_prompt(u, q, k, v, gates, x, sinks, cw, cb, lg, lb, wc, wa, wo, n_rows, seq):
    rows = MIX_ROWS
    halo_per_block = rows // CONV_HALO
    pre_per_block = rows // WINDOW
    kv_prev = pl.BlockSpec((WINDOW, KV_W), lambda i: (jnp.maximum(i * pre_per_block - 1, 0), 0))
    kv_cur = pl.BlockSpec((rows, KV_W), lambda i: (i, 0))
    return pl.pallas_call(
        functools.partial(_mix_kernel, seq // rows),
        grid=(n_rows // rows,),
        in_specs=[pl.BlockSpec(memory_space=pltpu.SMEM),
                  pl.BlockSpec((rows, D_CONV), lambda i: (i, 0)),
                  pl.BlockSpec((CONV_HALO, D_CONV), lambda i: (jnp.maximum(i * halo_per_block - 1, 0), 0)),
                  pl.BlockSpec((rows, Q_W), lambda i: (i, 0)),
                  kv_prev, kv_cur, kv_prev, kv_cur,
                  *_conv_common_specs(),
                  pl.BlockSpec((rows, D_MODEL), lambda i: (i, 0)),
                  pl.BlockSpec((rows, D_MODEL), lambda i: (i, 1)),
                  pl.BlockSpec((rows, D_MODEL), lambda i: (i, 0)),
                  _resident((D_CONV, D_MODEL)), _resident((Q_W, D_MODEL)), _resident((D_MODEL, D_MODEL))],
        out_specs=pl.BlockSpec((rows, D_MODEL), lambda i: (i, 0)),
        out_shape=jax.ShapeDtypeStruct((n_rows, D_MODEL), F32),
        scratch_shapes=_conv_scratch(rows, MIX_TILE),
        compiler_params=_params("parallel"),
        name="mix_prompt",
    )(sinks, u, u, q, k, k, v, v, cw, cb, lg, lb, gates, gates, x, wc, wa, wo)


def _mix_sample_kernel(sink_ref, u_ref, past_ref, q_ref, kp_ref, kc_ref, vp_ref, vc_ref,
                       cw_ref, cb_ref, lg_ref, lb_ref, gc_ref, ga_ref, x_ref, wc_ref, wa_ref, wo_ref,
                       out_ref, s_ref, c_ref):
    ycs, os_ = [], []
    for s in range(u_ref.shape[0] // CHUNK):
        rows = slice(s * CHUNK, (s + 1) * CHUNK)
        _conv_stage(s_ref, past_ref[s], u_ref[rows, :])
        ycs.append(_conv_tile(s_ref, 0, c_ref, cw_ref, cb_ref, lg_ref, lb_ref, CHUNK))
        k = jnp.concatenate([kp_ref[s], kc_ref[rows, :]], axis=0).astype(BF16)
        v = jnp.concatenate([vp_ref[s], vc_ref[rows, :]], axis=0).astype(BF16)
        os_.append(_attn_tile(q_ref[rows, :], k, v, sink_ref, None))
    out_ref[...] = _merge_out(jnp.concatenate(ycs, axis=0), jnp.concatenate(os_, axis=0),
                              gc_ref[...], ga_ref[...], x_ref[...], wc_ref, wa_ref, wo_ref)


def _mix_sample(u, q, k, v, gates, x, sinks, cw, cb, lg, lb, wc, wa, wo, state, cache_k, cache_v, row0):
    n_seq, hist, _ = state.shape
    assert hist == CONV_WIDTH - 1 and cache_k.shape[1] == WINDOW
    rows = SAMPLE_ROWS
    seqs = rows // CHUNK
    blk0 = row0 // rows

    def cur(width, col=0):
        return pl.BlockSpec((rows, width), lambda i: (blk0 + i, col))

    cache = pl.BlockSpec((seqs, WINDOW, KV_W), lambda i: (i, 0, 0))
    return pl.pallas_call(
        _mix_sample_kernel,
        grid=(n_seq // seqs,),
        in_specs=[pl.BlockSpec(memory_space=pltpu.SMEM),
                  cur(D_CONV), pl.BlockSpec((seqs, hist, D_CONV), lambda i: (i, 0, 0)), cur(Q_W),
                  cache, cur(KV_W), cache, cur(KV_W),
                  *_conv_common_specs(),
                  cur(D_MODEL), cur(D_MODEL, 1), cur(D_MODEL),
                  _resident((D_CONV, D_MODEL)), _resident((Q_W, D_MODEL)), _resident((D_MODEL, D_MODEL))],
        out_specs=pl.BlockSpec((rows, D_MODEL), lambda i: (i, 0)),
        out_shape=jax.ShapeDtypeStruct((n_seq * CHUNK, D_MODEL), F32),
        scratch_shapes=_conv_scratch(CHUNK, CHUNK),
        compiler_params=_params("parallel"),
        name="mix_sample",
    )(sinks, u, state, q, cache_k, k, cache_v, v, cw, cb, lg, lb, gates, gates, x, wc, wa, wo)


def _ple_kernel(n_prompt_tiles, x_ref, pp_ref, ps_ref, gn_ref, wg_ref, wp_ref, fn_ref, yp_ref, ys_ref):
    i = pl.program_id(0)
    x = x_ref[...]
    gate = jax.nn.sigmoid(_dot(_rms(x, gn_ref[...]).astype(BF16), wg_ref[...]))
    pe = jnp.where(i < n_prompt_tiles, pp_ref[...], ps_ref[...]).astype(BF16)
    y = _rms(x + gate * _dot(pe, wp_ref[...]), fn_ref[...])

    @pl.when(i < n_prompt_tiles)
    def _():
        yp_ref[...] = y

    @pl.when(i >= n_prompt_tiles)
    def _():
        ys_ref[...] = y


def _ple(x, pp, ps, gn, wg, wp, fn):
    tm = PLE_TILE
    n_p, n_s = pp.shape[0] // tm, ps.shape[0] // tm
    vec = pl.BlockSpec((1, D_MODEL), lambda i: (0, 0))

    def p_idx(i):
        return (jnp.minimum(i, n_p - 1), 0)

    def s_idx(i):
        return (jnp.maximum(i - n_p, 0), 0)

    return pl.pallas_call(
        functools.partial(_ple_kernel, n_p),
        grid=(n_p + n_s,),
        in_specs=[pl.BlockSpec((tm, D_MODEL), lambda i: (i, 0)),
                  pl.BlockSpec((tm, D_PLE), p_idx),
                  pl.BlockSpec((tm, D_PLE), s_idx),
                  vec, _resident((D_MODEL, D_MODEL)), _resident((D_PLE, D_MODEL)), vec],
        out_specs=[pl.BlockSpec((tm, D_MODEL), p_idx), pl.BlockSpec((tm, D_MODEL), s_idx)],
        out_shape=[jax.ShapeDtypeStruct((pp.shape[0], D_MODEL), F32),
                   jax.ShapeDtypeStruct((ps.shape[0], D_MODEL), F32)],
        compiler_params=_params("arbitrary"),
        name="ple_final",
    )(x, pp, ps, gn, wg, wp, fn)


def kernel(x_prompt, x_sample, p_prompt, p_sample, state_conv, cache_k, cache_v, ffn1_norm, ffn1_w_gu, ffn1_w_down, mix_norm, w_in, conv_w, conv_b, conv_ln_g, conv_ln_b, conv_w_out, attn_sinks, attn_w_out, w_out, ffn2_norm, ffn2_w_gu, ffn2_w_down, ple_norm, ple_w_gate, ple_w_proj, final_norm):
    assert x_prompt.shape[-1] == D_MODEL and ffn1_norm.shape[0] == 1 and w_in.shape[-1] == IN_COLS
    batch, seq, _ = x_prompt.shape
    dec_batch, dec_seq, _ = x_sample.shape
    assert dec_seq == CHUNK and seq % ROW_TILE == 0
    n_p, n_s = batch * seq, dec_batch * dec_seq

    def vec(a):
        return a.reshape(1, -1)

    xp = x_prompt.reshape(n_p, D_MODEL)
    xs = x_sample.reshape(n_s, D_MODEL)

    x1_s, hn_s, wg1, wu1, wd1 = _ffn_first(xs, vec(ffn1_norm), ffn1_w_gu[0], ffn1_w_down[0], vec(mix_norm),
                                           True, "ffn1_first")
    x1, hn, wgu2, wd2, w_in_b = _ffn(
        xp, x1_s, hn_s, vec(ffn1_norm), (wg1, 0), (wu1, 0), wd1, vec(mix_norm), True, "ffn1",
        side=[(ffn2_w_gu[0], WGU_CAST_ROWS), (ffn2_w_down[0], WD_CAST_ROWS), (w_in[0], WGU_CAST_ROWS)])

    cos, slo, shi = _rope_tables(batch, seq, dec_batch, dec_seq)
    u, q, k, v = _uqkv(hn, w_in_b, cos, slo, shi)
    gate_steps = (n_p + n_s) // IN_TILE * (2 * D_MODEL // IN_COL_TILE)
    gates, w_co, w_ao, w_o, w_pg, w_pp = _gates(
        hn, w_in_b, side=[(w, max(w.shape[0] // gate_steps, BF16_ROWS))
                           for w in (conv_w_out[0], attn_w_out[0], w_out[0], ple_w_gate[0], ple_w_proj[0])])

    cw, cb, lg, lb = conv_w[0], vec(conv_b), vec(conv_ln_g), vec(conv_ln_b)
    sinks = attn_sinks[0]
    ck = cache_k[0].reshape(dec_batch, -1, KV_W)
    cv = cache_v[0].reshape(dec_batch, -1, KV_W)
    mix_args = (u, q, k, v, gates, x1, sinks, cw, cb, lg, lb, w_co, w_ao, w_o)
    x2_p = _mix_prompt(*mix_args, n_p, seq)
    x2_s = _mix_sample(*mix_args, state_conv[0], ck, cv, n_p)
    x3, = _ffn(x2_p, x2_s, None, vec(ffn2_norm), (wgu2, 0), (wgu2, D_FF), wd2, vec(ffn2_norm), False, "ffn2")

    yp, ys = _ple(x3, p_prompt[0].reshape(n_p, D_PLE), p_sample[0].reshape(n_s, D_PLE),
                  vec(ple_norm), w_pg, w_pp, vec(final_norm))

    w_rows = ck.shape[1]

    def prompt_tail(a, rows):
        return jnp.stack([a[(b + 1) * seq - rows:(b + 1) * seq] for b in range(batch)])

    k_p = prompt_tail(k, WINDOW).reshape(batch, WINDOW, N_KV, HEAD_DIM)
    v_p = prompt_tail(v, WINDOW).reshape(batch, WINDOW, N_KV, HEAD_DIM)
    c_p = prompt_tail(u, CONV_WIDTH - 1)
    k_s = jnp.concatenate([cache_k[0], k[n_p:].reshape(dec_batch, dec_seq, N_KV, HEAD_DIM)], 1)[:, -w_rows:]
    v_s = jnp.concatenate([cache_v[0], v[n_p:].reshape(dec_batch, dec_seq, N_KV, HEAD_DIM)], 1)[:, -w_rows:]
    c_s = jnp.concatenate([state_conv[0], u[n_p:].reshape(dec_batch, dec_seq, D_CONV)], 1)[:, -(CONV_WIDTH - 1):]
    return (yp.reshape(batch, seq, D_MODEL), ys.reshape(dec_batch, dec_seq, D_MODEL),
            k_p[None], v_p[None], c_p[None], k_s[None], v_s[None], c_s[None])
```

```python
import functools

import jax
import jax.numpy as jnp
from jax import lax
from jax.experimental import pallas as pl
from jax.experimental.pallas import tpu as pltpu

D_MODEL = 2048
D_PLE = 256
D_FF = 5504
D_CONV = 1024
CONV_WIDTH = 31
HEAD_DIM = 64
N_HEADS = 16
N_KV = 4
GROUP = N_HEADS // N_KV
ROT_DIM = 16
ROPE_THETA = 500000.0
CHUNK = 64
WINDOW = 128
PAST_LEN = 1024
EPS = 1e-6
NEG = -1e30
LOG2E = 1.4426950408889634
Q_W = N_HEADS * HEAD_DIM
KV_W = N_KV * HEAD_DIM
IN_COLS = 2 * D_CONV + Q_W + 2 * KV_W + 2 * D_MODEL

LANES = 128
SUBLANES = 8
BF16_ROWS = 16
FF1_TILE = 512
FF2_TILE = 768
CAST_TILE = 256
WGU_CAST_ROWS = 16
WD_CAST_ROWS = 64
ROW_TILE = 512
IN_TILE = 1088
IN_COL_TILE = 1024
UQKV_TILE = 544
SAMPLE_ROWS = 256
PLE_TILE = 512
CONV_HALO = 32
MIX_ROWS = 256
MIX_TILE = WINDOW
VMEM_LIMIT = 56 * 1024 * 1024

F32 = jnp.float32
BF16 = jnp.bfloat16


def _params(*sem):
    return pltpu.CompilerParams(dimension_semantics=sem, vmem_limit_bytes=VMEM_LIMIT)


def _rms(x, g):
    return x * lax.rsqrt(jnp.mean(x * x, axis=-1, keepdims=True) + EPS) * g


def _dot(a, b):
    return jnp.dot(a, b, preferred_element_type=F32)


def _resident(shape):
    return pl.BlockSpec(shape, lambda *_: (0,) * len(shape), pipeline_mode=pl.Buffered(1))


def _side_cast_specs(side, n_steps, step_of):
    in_specs, out_shape = [], []
    for a, rows in side:
        n_blocks = a.shape[0] // rows
        assert a.shape[0] % rows == 0 and n_blocks <= n_steps
        in_specs.append(pl.BlockSpec((rows, a.shape[1]),
                                     lambda *g, n=n_blocks: (jnp.minimum(step_of(*g), n - 1), 0)))
        out_shape.append(jax.ShapeDtypeStruct(a.shape, BF16))
    return in_specs, out_shape


def _side_cast(src_refs, dst_refs):
    for src, dst in zip(src_refs, dst_refs, strict=True):
        dst[...] = src[...].astype(BF16)


def _swiglu_accumulate(xn_ref, xo_ref, wg, wu, wd):
    xn = xn_ref[...]
    g = _dot(xn, wg)
    u = _dot(xn, wu)
    h = (g * jax.nn.sigmoid(g)) * (u * 0.5)
    xo_ref[...] += _dot(h.astype(BF16), wd)


def _ff_steps(tile):
    return -(-D_FF // tile)


def _ff_start(f, tile):
    return pl.multiple_of(jnp.minimum(f * tile, D_FF - tile), LANES)


def _ffn_first_kernel(with_hn, x_ref, g1_ref, wg_ref, wu_ref, wd_ref, g2_ref, xo_ref, *rest):
    hn_ref = rest[0] if with_hn else None
    wgb_ref, wub_ref, wdb_ref, xn_ref = rest[with_hn:]
    f = pl.program_id(0)
    steps = _ff_steps(CAST_TILE)
    wg, wu, wd = wg_ref[...].astype(BF16), wu_ref[...].astype(BF16), wd_ref[...].astype(BF16)

    @pl.when(f == 0)
    def _():
        x = x_ref[...]
        xn_ref[...] = _rms(x, g1_ref[...]).astype(BF16)
        xo_ref[...] = x

    @pl.when(f < steps - 1)
    def _():
        wgb_ref[...] = wg
        wub_ref[...] = wu
        wdb_ref[...] = wd
        _swiglu_accumulate(xn_ref, xo_ref, wg, wu, wd)

    @pl.when(f == steps - 1)
    def _():
        old = steps * CAST_TILE - D_FF
        wg_new, wu_new, wd_new = wg[:, old:], wu[:, old:], wd[old:, :]
        wgb_ref[...] = jnp.concatenate([wg_new, jnp.zeros((D_MODEL, old), BF16)], axis=1)
        wub_ref[...] = jnp.concatenate([wu_new, jnp.zeros((D_MODEL, old), BF16)], axis=1)
        wdb_ref[...] = jnp.concatenate([wd_new, jnp.zeros((old, D_MODEL), BF16)], axis=0)
        _swiglu_accumulate(xn_ref, xo_ref, wg_new, wu_new, wd_new)

    if with_hn:
        @pl.when(f == steps - 1)
        def _():
            hn_ref[...] = _rms(xo_ref[...], g2_ref[...]).astype(BF16)


def _ffn_first(x, g1, w_gu, w_down, g2, with_hn, name):
    assert x.shape[0] == ROW_TILE
    tile = CAST_TILE
    cols = (pl.Element(D_MODEL), pl.Element(tile))
    rows = (pl.Element(tile), pl.Element(D_MODEL))
    vec = pl.BlockSpec((1, D_MODEL), lambda f: (0, 0))
    row = pl.BlockSpec((ROW_TILE, D_MODEL), lambda f: (0, 0))
    gate_win = pl.BlockSpec(cols, lambda f: (0, _ff_start(f, tile)))
    up_win = pl.BlockSpec(cols, lambda f: (0, pl.multiple_of(D_FF + _ff_start(f, tile), LANES)))
    down_win = pl.BlockSpec(rows, lambda f: (_ff_start(f, tile), 0))
    out_shape = [jax.ShapeDtypeStruct((ROW_TILE, D_MODEL), F32)]
    if with_hn:
        out_shape.append(jax.ShapeDtypeStruct((ROW_TILE, D_MODEL), BF16))
    steps = _ff_steps(tile)
    weights_shape = ([jax.ShapeDtypeStruct((D_MODEL, steps * tile), BF16)] * 2
                     + [jax.ShapeDtypeStruct((steps * tile, D_MODEL), BF16)])
    col_blk = pl.BlockSpec((D_MODEL, tile), lambda f: (0, f))
    row_blk = pl.BlockSpec((tile, D_MODEL), lambda f: (f, 0))
    return pl.pallas_call(
        functools.partial(_ffn_first_kernel, with_hn),
        grid=(steps,),
        in_specs=[row, vec, gate_win, up_win, down_win, vec],
        out_specs=[row] * len(out_shape) + [col_blk, col_blk, row_blk],
        out_shape=out_shape + weights_shape,
        scratch_shapes=[pltpu.VMEM((ROW_TILE, D_MODEL), BF16)],
        compiler_params=_params("arbitrary"),
        name=name,
    )(x, g1, w_gu, w_gu, w_down, g2)


def _ffn_kernel(n_prompt_tiles, with_hn, tail_done, n_side, steps, xp_ref, xs_ref, *rest):
    if tail_done:
        hns_ref, rest = rest[0], rest[1:]
    g1_ref, wg_ref, wu_ref, wd_ref, wg_tail_ref, wu_tail_ref, wd_tail_ref, g2_ref = rest[:8]
    side_in, outs = rest[8:8 + n_side], rest[8 + n_side:]
    xo_ref = outs[0]
    hn_ref = outs[1] if with_hn else None
    side_out = outs[1 + with_hn:1 + with_hn + n_side]
    xn_ref = outs[-1]
    i, f = pl.program_id(0), pl.program_id(1)
    _side_cast(side_in, side_out)

    def on(cond):
        return jnp.logical_and(cond, i < n_prompt_tiles) if tail_done else cond

    @pl.when(on(f == 0))
    def _():
        x = xp_ref[...] if tail_done else jnp.where(i < n_prompt_tiles, xp_ref[...], xs_ref[...])
        xn_ref[...] = _rms(x, g1_ref[...]).astype(BF16)
        xo_ref[...] = x

    if tail_done:
        @pl.when(jnp.logical_and(f == 0, i >= n_prompt_tiles))
        def _():
            xo_ref[...] = xs_ref[...]
            hn_ref[...] = hns_ref[...]

    pl.when(on(f < steps - 1))(
        lambda: _swiglu_accumulate(xn_ref, xo_ref, wg_ref[...], wu_ref[...], wd_ref[...]))

    @pl.when(on(f == steps - 1))
    def _():
        _swiglu_accumulate(xn_ref, xo_ref, wg_ref[...], wu_ref[...], wd_ref[...])
        _swiglu_accumulate(xn_ref, xo_ref, wg_tail_ref[...], wu_tail_ref[...], wd_tail_ref[...])
        if with_hn:
            hn_ref[...] = _rms(xo_ref[...], g2_ref[...]).astype(BF16)


def _ffn(xp, xs, hn_s, g1, wg, wu, wd, g2, with_hn, name, tile, side=()):
    n_p, n_s = xp.shape[0] // ROW_TILE, xs.shape[0] // ROW_TILE
    assert n_s == 1 and xs.shape[0] == ROW_TILE
    m = xp.shape[0] + xs.shape[0]
    tail_done = hn_s is not None
    steps = D_FF // tile
    tail0 = steps * tile
    n_tail = D_FF - tail0
    assert with_hn or not tail_done

    def start(i, f):
        return pl.multiple_of((jnp.where(i < n_p, f, steps - 1) if tail_done else f) * tile, LANES)

    cols = (pl.Element(D_MODEL), pl.Element(tile))
    rows = (pl.Element(tile), pl.Element(D_MODEL))
    once = pl.Buffered(1)
    tail_w = [pl.BlockSpec((pl.Element(D_MODEL), pl.Element(n_tail)), lambda i, f: (0, wg[1] + tail0),
                           pipeline_mode=once),
              pl.BlockSpec((pl.Element(D_MODEL), pl.Element(n_tail)), lambda i, f: (0, wu[1] + tail0),
                           pipeline_mode=once),
              pl.BlockSpec((pl.Element(n_tail), pl.Element(D_MODEL)), lambda i, f: (tail0, 0),
                           pipeline_mode=once)]
    vec = pl.BlockSpec((1, D_MODEL), lambda i, f: (0, 0))
    row = pl.BlockSpec((ROW_TILE, D_MODEL), lambda i, f: (i, 0))
    if tail_done:
        tail_specs, tail_args = [_resident((ROW_TILE, D_MODEL))] * 2, [xs, hn_s]
    else:
        tail_specs, tail_args = [pl.BlockSpec((ROW_TILE, D_MODEL), lambda i, f: (0, 0))], [xs]
    out_shape = [jax.ShapeDtypeStruct((m, D_MODEL), F32)]
    if with_hn:
        out_shape.append(jax.ShapeDtypeStruct((m, D_MODEL), BF16))
    side_specs, side_shape = _side_cast_specs(side, (n_p + n_s) * steps, lambda i, f: i * steps + f)
    return pl.pallas_call(
        functools.partial(_ffn_kernel, n_p, with_hn, tail_done, len(side), steps),
        grid=(n_p + n_s, steps),
        in_specs=[pl.BlockSpec((ROW_TILE, D_MODEL), lambda i, f: (jnp.minimum(i, n_p - 1), 0)),
                  *tail_specs, vec,
                  pl.BlockSpec(cols, lambda i, f: (0, pl.multiple_of(wg[1] + start(i, f), LANES))),
                  pl.BlockSpec(cols, lambda i, f: (0, pl.multiple_of(wu[1] + start(i, f), LANES))),
                  pl.BlockSpec(rows, lambda i, f: (start(i, f), 0)),
                  *tail_w, vec, *side_specs],
        out_specs=[row] * len(out_shape) + side_specs,
        out_shape=out_shape + side_shape,
        scratch_shapes=[pltpu.VMEM((ROW_TILE, D_MODEL), BF16)],
        compiler_params=_params("arbitrary", "arbitrary"),
        name=name,
    )(xp, *tail_args, g1, wg[0], wu[0], wd, wg[0], wu[0], wd, g2, *[a for a, _ in side])


def _gates_kernel(n_side, h_ref, w_ref, *rest):
    side_in, o_ref, side_out = rest[:n_side], rest[n_side], rest[n_side + 1:]
    _side_cast(side_in, side_out)
    o_ref[...] = jax.nn.sigmoid(_dot(h_ref[...], w_ref[...])).astype(BF16)


def _gates(hn, w_in, side=()):
    m, tn = hn.shape[0], IN_COL_TILE
    col0 = 2 * D_CONV + Q_W + 2 * KV_W
    nj = 2 * D_MODEL // tn
    side_specs, side_shape = _side_cast_specs(side, m // IN_TILE * nj, lambda i, j: i * nj + j)
    return pl.pallas_call(
        functools.partial(_gates_kernel, len(side)),
        grid=(m // IN_TILE, nj),
        in_specs=[pl.BlockSpec((IN_TILE, D_MODEL), lambda i, j: (i, 0)),
                  pl.BlockSpec((pl.Element(D_MODEL), pl.Element(tn)),
                               lambda i, j: (0, pl.multiple_of(col0 + j * tn, LANES))),
                  *side_specs],
        out_specs=[pl.BlockSpec((IN_TILE, tn), lambda i, j: (i, j))] + side_specs,
        out_shape=[jax.ShapeDtypeStruct((m, 2 * D_MODEL), BF16)] + side_shape,
        compiler_params=_params("arbitrary", "arbitrary"),
        name="gates",
    )(hn, w_in, *[a for a, _ in side])


def _rotate(x, cos, sin_lo, sin_hi):
    cols = []
    for c in range(x.shape[1] // LANES):
        xb = x[:, c * LANES:(c + 1) * LANES]
        cols.append(xb * cos
                    + pltpu.roll(xb, LANES - ROT_DIM // 2, 1) * sin_lo
                    + pltpu.roll(xb, ROT_DIM // 2, 1) * sin_hi)
    return jnp.concatenate(cols, axis=1)


def _uqkv_kernel(h_ref, w_ref, cos_ref, slo_ref, shi_ref, u_ref, q_ref, k_ref, v_ref):
    h = h_ref[...]
    cos, slo, shi = cos_ref[...], slo_ref[...], shi_ref[...]
    c0 = 2 * D_CONV
    u_ref[...] = _dot(h, w_ref[:, :D_CONV]) * jax.nn.sigmoid(_dot(h, w_ref[:, D_CONV:c0]))
    q = _rotate(_dot(h, w_ref[:, c0:c0 + Q_W]), cos, slo, shi)
    q_ref[...] = (q * (LOG2E * HEAD_DIM ** -0.5)).astype(BF16)
    k_ref[...] = _rotate(_dot(h, w_ref[:, c0 + Q_W:c0 + Q_W + KV_W]), cos, slo, shi)
    v_ref[...] = _dot(h, w_ref[:, c0 + Q_W + KV_W:])


def _uqkv(hn, w_in, cos, slo, shi):
    m, tm = hn.shape[0], UQKV_TILE
    n_w = 2 * D_CONV + Q_W + 2 * KV_W
    tab = pl.BlockSpec((tm, LANES), lambda i: (i, 0))
    return pl.pallas_call(
        _uqkv_kernel,
        grid=(m // tm,),
        in_specs=[pl.BlockSpec((tm, D_MODEL), lambda i: (i, 0)),
                  pl.BlockSpec((pl.Element(D_MODEL), pl.Element(n_w)), lambda i: (0, 0),
                               pipeline_mode=pl.Buffered(1)),
                  tab, tab, tab],
        out_specs=[pl.BlockSpec((tm, D_CONV), lambda i: (i, 0)),
                   pl.BlockSpec((tm, Q_W), lambda i: (i, 0)),
                   pl.BlockSpec((tm, KV_W), lambda i: (i, 0)),
                   pl.BlockSpec((tm, KV_W), lambda i: (i, 0))],
        out_shape=[jax.ShapeDtypeStruct((m, D_CONV), F32),
                   jax.ShapeDtypeStruct((m, Q_W), BF16),
                   jax.ShapeDtypeStruct((m, KV_W), F32),
                   jax.ShapeDtypeStruct((m, KV_W), F32)],
        compiler_params=_params("parallel"),
        name="uqkv",
    )(hn, w_in, cos, slo, shi)


def _rope_tables(batch, seq, dec_batch, dec_seq):
    inv = ROPE_THETA ** (-jnp.arange(0, ROT_DIM, 2, dtype=F32) / ROT_DIM)
    pos = jnp.concatenate([jnp.tile(jnp.arange(seq, dtype=jnp.int32), batch),
                           jnp.tile(PAST_LEN + jnp.arange(dec_seq, dtype=jnp.int32), dec_batch)])
    ang = pos.astype(F32)[:, None] * inv[None, :]
    cos, sin = jnp.cos(ang), jnp.sin(ang)
    one = jnp.ones((pos.shape[0], HEAD_DIM - ROT_DIM), F32)
    zero = jnp.zeros_like(one)
    zh = jnp.zeros_like(sin)
    c = jnp.concatenate([cos, cos, one], 1)
    lo = jnp.concatenate([-sin, zh, zero], 1)
    hi = jnp.concatenate([zh, sin, zero], 1)
    return [jnp.tile(t, (1, LANES // HEAD_DIM)) for t in (c, lo, hi)]


def _conv_stage(s_ref, history, u_ref):
    rows0 = CONV_HALO - history.shape[0]
    for cb in range(D_CONV // LANES):
        lanes = slice(cb * LANES, (cb + 1) * LANES)
        s_ref[cb, rows0:CONV_HALO, :] = history[:, lanes]
        s_ref[cb, CONV_HALO:, :] = u_ref[:, lanes]


def _conv_tile(s_ref, row0, c_ref, w_ref, b_ref, lg_ref, lb_ref, rows):
    base = CONV_HALO - (CONV_WIDTH - 1)
    n = rows // SUBLANES
    for cb in range(D_CONV // LANES):
        lanes = slice(cb * LANES, (cb + 1) * LANES)
        accs = [jnp.broadcast_to(b_ref[:, lanes], (n, LANES)) for _ in range(SUBLANES)]
        for j in range(CONV_WIDTH):
            w = jnp.broadcast_to(w_ref[j:j + 1, lanes], (n, LANES))
            for p in range(SUBLANES):
                accs[p] = accs[p] + s_ref[cb, pl.ds(row0 + base + j + p, n, stride=SUBLANES), :] * w
        for p in range(SUBLANES):
            c_ref[cb, pl.ds(p, n, stride=SUBLANES), :] = accs[p]
    acc = jnp.concatenate([c_ref[cb] for cb in range(D_CONV // LANES)], axis=1)
    mu = jnp.mean(acc, axis=-1, keepdims=True)
    xc = acc - mu
    y = xc * lax.rsqrt(jnp.mean(xc * xc, axis=-1, keepdims=True) + EPS) * lg_ref[...] + lb_ref[...]
    return (y * jax.nn.sigmoid(y)).astype(BF16)


def _conv_common_specs():
    return [pl.BlockSpec((CONV_WIDTH, D_CONV), lambda i: (0, 0)),
            pl.BlockSpec((1, D_CONV), lambda i: (0, 0)),
            pl.BlockSpec((1, D_CONV), lambda i: (0, 0)),
            pl.BlockSpec((1, D_CONV), lambda i: (0, 0))]


def _conv_scratch(block_rows, tile_rows):
    return [pltpu.VMEM((D_CONV // LANES, CONV_HALO + block_rows, LANES), F32),
            pltpu.VMEM((D_CONV // LANES, tile_rows, LANES), F32)]


def _attn_tile(q, k, v, sink_ref, valid):
    r = q.shape[0]
    lane_head = lax.broadcasted_iota(jnp.int32, (1, KV_W), 1) // HEAD_DIM
    sel_row = lax.broadcasted_iota(jnp.int32, (KV_W, Q_W), 0)
    sel_col = lax.broadcasted_iota(jnp.int32, (KV_W, Q_W), 1)
    sel = jnp.where(sel_row == sel_col // KV_W * HEAD_DIM + sel_col % HEAD_DIM, 1.0, 0.0).astype(BF16)
    krep = _dot(k, sel).astype(BF16)
    vrep = _dot(v, sel).astype(BF16)
    head_mask = [jnp.where(lane_head == g, 1.0, 0.0).astype(BF16) for g in range(GROUP)]
    scores = []
    for kv in range(N_KV):
        cols = slice(kv * KV_W, (kv + 1) * KV_W)
        lhs = jnp.concatenate([q[:, cols] * head_mask[g] for g in range(GROUP)], axis=0)
        scores.append(lax.dot_general(lhs, krep[:, cols], (((1,), (1,)), ((), ())),
                                      preferred_element_type=F32))
    weights, dens = [], []
    for kv in range(N_KV):
        es = []
        for g in range(GROUP):
            sg = scores[kv][g * r:(g + 1) * r]
            if valid is not None:
                sg = jnp.where(valid, sg, NEG)
            sink = sink_ref[kv * GROUP + g] * LOG2E
            mx = jnp.maximum(jnp.max(sg, axis=-1, keepdims=True), sink)
            e = jnp.exp2(sg - mx)
            dens.append(jnp.sum(e, axis=-1, keepdims=True) + jnp.exp2(sink - mx))
            es.append(e.astype(BF16))
        weights.append(jnp.concatenate(es, axis=0))
    outs = []
    for kv in range(N_KV):
        o4 = _dot(weights[kv], vrep[:, kv * KV_W:(kv + 1) * KV_W])
        og = [o4[g * r:(g + 1) * r] / dens[kv * GROUP + g] for g in range(GROUP)]
        oh = og[GROUP - 1]
        for g in range(GROUP - 2, -1, -1):
            oh = jnp.where(lane_head == g, og[g], oh)
        outs.append(oh)
    return jnp.concatenate(outs, axis=1).astype(BF16)


def _merge_out(yc, o, gc, ga, x, wc_ref, wa_ref, wo_ref):
    m = gc.astype(F32) * _dot(yc, wc_ref[...]) + ga.astype(F32) * _dot(o, wa_ref[...])
    return x + _dot(m.astype(BF16), wo_ref[...])


def _mix_kernel(per_seq, sink_ref, u_ref, uh_ref, q_ref, kp_ref, kc_ref, vp_ref, vc_ref,
                cw_ref, cb_ref, lg_ref, lb_ref, gc_ref, ga_ref, x_ref, wc_ref, wa_ref, wo_ref,
                out_ref, s_ref, c_ref):
    first = pl.program_id(0) % per_seq == 0
    _conv_stage(s_ref, jnp.where(first, 0.0, uh_ref[...]), u_ref)

    k = jnp.concatenate([kp_ref[...], kc_ref[...]], axis=0).astype(BF16)
    v = jnp.concatenate([vp_ref[...], vc_ref[...]], axis=0).astype(BF16)
    nk = WINDOW + MIX_TILE
    col = lax.broadcasted_iota(jnp.int32, (MIX_TILE, nk), 1)
    d = col // CHUNK - lax.broadcasted_iota(jnp.int32, (MIX_TILE, nk), 0) // CHUNK
    band = (d >= 0) & (d <= WINDOW // CHUNK)

    for r0 in range(0, MIX_ROWS, MIX_TILE):
        rows = slice(r0, r0 + MIX_TILE)
        yc = _conv_tile(s_ref, r0, c_ref, cw_ref, cb_ref, lg_ref, lb_ref, MIX_TILE)
        valid = band & ((col >= WINDOW) | jnp.logical_not(first)) if r0 == 0 else band
        o = _attn_tile(q_ref[rows, :], k[r0:r0 + nk], v[r0:r0 + nk], sink_ref, valid)
        out_ref[rows, :] = _merge_out(yc, o, gc_ref[rows, :], ga_ref[rows, :], x_ref[rows, :],
                                      wc_ref, wa_ref, wo_ref)


def _mix_prompt(u, q, k, v, gates, x, sinks, cw, cb, lg, lb, wc, wa, wo, n_rows, seq):
    rows = MIX_ROWS
    halo_per_block = rows // CONV_HALO
    pre_per_block = rows // WINDOW
    kv_prev = pl.BlockSpec((WINDOW, KV_W), lambda i: (jnp.maximum(i * pre_per_block - 1, 0), 0))
    kv_cur = pl.BlockSpec((rows, KV_W), lambda i: (i, 0))
    return pl.pallas_call(
        functools.partial(_mix_kernel, seq // rows),
        grid=(n_rows // rows,),
        in_specs=[pl.BlockSpec(memory_space=pltpu.SMEM),
                  pl.BlockSpec((rows, D_CONV), lambda i: (i, 0)),
                  pl.BlockSpec((CONV_HALO, D_CONV), lambda i: (jnp.maximum(i * halo_per_block - 1, 0), 0)),
                  pl.BlockSpec((rows, Q_W), lambda i: (i, 0)),
                  kv_prev, kv_cur, kv_prev, kv_cur,
                  *_conv_common_specs(),
                  pl.BlockSpec((rows, D_MODEL), lambda i: (i, 0)),
                  pl.BlockSpec((rows, D_MODEL), lambda i: (i, 1)),
                  pl.BlockSpec((rows, D_MODEL), lambda i: (i, 0)),
                  _resident((D_CONV, D_MODEL)), _resident((Q_W, D_MODEL)), _resident((D_MODEL, D_MODEL))],
        out_specs=pl.BlockSpec((rows, D_MODEL), lambda i: (i, 0)),
        out_shape=jax.ShapeDtypeStruct((n_rows, D_MODEL), F32),
        scratch_shapes=_conv_scratch(rows, MIX_TILE),
        compiler_params=_params("parallel"),
        name="mix_prompt",
    )(sinks, u, u, q, k, k, v, v, cw, cb, lg, lb, gates, gates, x, wc, wa, wo)


def _mix_sample_kernel(sink_ref, u_ref, past_ref, q_ref, kp_ref, kc_ref, vp_ref, vc_ref,
                       cw_ref, cb_ref, lg_ref, lb_ref, gc_ref, ga_ref, x_ref, wc_ref, wa_ref, wo_ref,
                       out_ref, s_ref, c_ref):
    ycs, os_ = [], []
    for s in range(u_ref.shape[0] // CHUNK):
        rows = slice(s * CHUNK, (s + 1) * CHUNK)
        _conv_stage(s_ref, past_ref[s], u_ref[rows, :])
        ycs.append(_conv_tile(s_ref, 0, c_ref, cw_ref, cb_ref, lg_ref, lb_ref, CHUNK))
        k = jnp.concatenate([kp_ref[s], kc_ref[rows, :]], axis=0).astype(BF16)
        v = jnp.concatenate([vp_ref[s], vc_ref[rows, :]], axis=0).astype(BF16)
        os_.append(_attn_tile(q_ref[rows, :], k, v, sink_ref, None))
    out_ref[...] = _merge_out(jnp.concatenate(ycs, axis=0), jnp.concatenate(os_, axis=0),
                              gc_ref[...], ga_ref[...], x_ref[...], wc_ref, wa_ref, wo_ref)


def _mix_sample(u, q, k, v, gates, x, sinks, cw, cb, lg, lb, wc, wa, wo, state, cache_k, cache_v, row0):
    n_seq, hist, _ = state.shape
    assert hist == CONV_WIDTH - 1 and cache_k.shape[1] == WINDOW
    rows = SAMPLE_ROWS
    seqs = rows // CHUNK
    blk0 = row0 // rows

    def cur(width, col=0):
        return pl.BlockSpec((rows, width), lambda i: (blk0 + i, col))

    cache = pl.BlockSpec((seqs, WINDOW, KV_W), lambda i: (i, 0, 0))
    return pl.pallas_call(
        _mix_sample_kernel,
        grid=(n_seq // seqs,),
        in_specs=[pl.BlockSpec(memory_space=pltpu.SMEM),
                  cur(D_CONV), pl.BlockSpec((seqs, hist, D_CONV), lambda i: (i, 0, 0)), cur(Q_W),
                  cache, cur(KV_W), cache, cur(KV_W),
                  *_conv_common_specs(),
                  cur(D_MODEL), cur(D_MODEL, 1), cur(D_MODEL),
                  _resident((D_CONV, D_MODEL)), _resident((Q_W, D_MODEL)), _resident((D_MODEL, D_MODEL))],
        out_specs=pl.BlockSpec((rows, D_MODEL), lambda i: (i, 0)),
        out_shape=jax.ShapeDtypeStruct((n_seq * CHUNK, D_MODEL), F32),
        scratch_shapes=_conv_scratch(CHUNK, CHUNK),
        compiler_params=_params("parallel"),
        name="mix_sample",
    )(sinks, u, state, q, cache_k, k, cache_v, v, cw, cb, lg, lb, gates, gates, x, wc, wa, wo)


def _ple_kernel(n_prompt_tiles, x_ref, pp_ref, ps_ref, gn_ref, wg_ref, wp_ref, fn_ref, yp_ref, ys_ref):
    i = pl.program_id(0)
    x = x_ref[...]
    gate = jax.nn.sigmoid(_dot(_rms(x, gn_ref[...]).astype(BF16), wg_ref[...]))
    pe = jnp.where(i < n_prompt_tiles, pp_ref[...], ps_ref[...]).astype(BF16)
    y = _rms(x + gate * _dot(pe, wp_ref[...]), fn_ref[...])

    @pl.when(i < n_prompt_tiles)
    def _():
        yp_ref[...] = y

    @pl.when(i >= n_prompt_tiles)
    def _():
        ys_ref[...] = y


def _ple(x, pp, ps, gn, wg, wp, fn):
    tm = PLE_TILE
    n_p, n_s = pp.shape[0] // tm, ps.shape[0] // tm
    vec = pl.BlockSpec((1, D_MODEL), lambda i: (0, 0))

    def p_idx(i):
        return (jnp.minimum(i, n_p - 1), 0)

    def s_idx(i):
        return (jnp.maximum(i - n_p, 0), 0)

    return pl.pallas_call(
        functools.partial(_ple_kernel, n_p),
        grid=(n_p + n_s,),
        in_specs=[pl.BlockSpec((tm, D_MODEL), lambda i: (i, 0)),
                  pl.BlockSpec((tm, D_PLE), p_idx),
                  pl.BlockSpec((tm, D_PLE), s_idx),
                  vec, _resident((D_MODEL, D_MODEL)), _resident((D_PLE, D_MODEL)), vec],
        out_specs=[pl.BlockSpec((tm, D_MODEL), p_idx), pl.BlockSpec((tm, D_MODEL), s_idx)],
        out_shape=[jax.ShapeDtypeStruct((pp.shape[0], D_MODEL), F32),
                   jax.ShapeDtypeStruct((ps.shape[0], D_MODEL), F32)],
        compiler_params=_params("arbitrary"),
        name="ple_final",
    )(x, pp, ps, gn, wg, wp, fn)


def kernel(x_prompt, x_sample, p_prompt, p_sample, state_conv, cache_k, cache_v, ffn1_norm, ffn1_w_gu, ffn1_w_down, mix_norm, w_in, conv_w, conv_b, conv_ln_g, conv_ln_b, conv_w_out, attn_sinks, attn_w_out, w_out, ffn2_norm, ffn2_w_gu, ffn2_w_down, ple_norm, ple_w_gate, ple_w_proj, final_norm):
    assert x_prompt.shape[-1] == D_MODEL and ffn1_norm.shape[0] == 1 and w_in.shape[-1] == IN_COLS
    batch, seq, _ = x_prompt.shape
    dec_batch, dec_seq, _ = x_sample.shape
    assert dec_seq == CHUNK and seq % ROW_TILE == 0
    n_p, n_s = batch * seq, dec_batch * dec_seq

    def vec(a):
        return a.reshape(1, -1)

    xp = x_prompt.reshape(n_p, D_MODEL)
    xs = x_sample.reshape(n_s, D_MODEL)

    x1_s, hn_s, wg1, wu1, wd1 = _ffn_first(xs, vec(ffn1_norm), ffn1_w_gu[0], ffn1_w_down[0], vec(mix_norm),
                                           True, "ffn1_first")
    x1, hn, wgu2, wd2, w_in_b = _ffn(
        xp, x1_s, hn_s, vec(ffn1_norm), (wg1, 0), (wu1, 0), wd1, vec(mix_norm), True, "ffn1", FF1_TILE,
        side=[(ffn2_w_gu[0], WGU_CAST_ROWS), (ffn2_w_down[0], WD_CAST_ROWS), (w_in[0], WGU_CAST_ROWS)])

    cos, slo, shi = _rope_tables(batch, seq, dec_batch, dec_seq)
    u, q, k, v = _uqkv(hn, w_in_b, cos, slo, shi)
    gate_steps = (n_p + n_s) // IN_TILE * (2 * D_MODEL // IN_COL_TILE)
    gates, w_co, w_ao, w_o, w_pg, w_pp = _gates(
        hn, w_in_b, side=[(w, max(w.shape[0] // gate_steps, BF16_ROWS))
                           for w in (conv_w_out[0], attn_w_out[0], w_out[0], ple_w_gate[0], ple_w_proj[0])])

    cw, cb, lg, lb = conv_w[0], vec(conv_b), vec(conv_ln_g), vec(conv_ln_b)
    sinks = attn_sinks[0]
    ck = cache_k[0].reshape(dec_batch, -1, KV_W)
    cv = cache_v[0].reshape(dec_batch, -1, KV_W)
    mix_args = (u, q, k, v, gates, x1, sinks, cw, cb, lg, lb, w_co, w_ao, w_o)
    x2_p = _mix_prompt(*mix_args, n_p, seq)
    x2_s = _mix_sample(*mix_args, state_conv[0], ck, cv, n_p)
    x3, = _ffn(x2_p, x2_s, None, vec(ffn2_norm), (wgu2, 0), (wgu2, D_FF), wd2, vec(ffn2_norm), False, "ffn2",
               FF2_TILE)

    yp, ys = _ple(x3, p_prompt[0].reshape(n_p, D_PLE), p_sample[0].reshape(n_s, D_PLE),
                  vec(ple_norm), w_pg, w_pp, vec(final_norm))

    w_rows = ck.shape[1]

    def prompt_tail(a, rows):
        return jnp.stack([a[(b + 1) * seq - rows:(b + 1) * seq] for b in range(batch)])

    k_p = prompt_tail(k, WINDOW).reshape(batch, WINDOW, N_KV, HEAD_DIM)
    v_p = prompt_tail(v, WINDOW).reshape(batch, WINDOW, N_KV, HEAD_DIM)
    c_p = prompt_tail(u, CONV_WIDTH - 1)
    k_s = jnp.concatenate([cache_k[0], k[n_p:].reshape(dec_batch, dec_seq, N_KV, HEAD_DIM)], 1)[:, -w_rows:]
    v_s = jnp.concatenate([cache_v[0], v[n_p:].reshape(dec_batch, dec_seq, N_KV, HEAD_DIM)], 1)[:, -w_rows:]
    c_s = jnp.concatenate([state_conv[0], u[n_p:].reshape(dec_batch, dec_seq, D_CONV)], 1)[:, -(CONV_WIDTH - 1):]
    return (yp.reshape(batch, seq, D_MODEL), ys.reshape(dec_batch, dec_seq, D_MODEL),
            k_p[None], v_p[None], c_p[None], k_s[None], v_s[None], c_s[None])
```

```python
import functools

import jax
import jax.numpy as jnp
from jax import lax
from jax.experimental import pallas as pl
from jax.experimental.pallas import tpu as pltpu

D_MODEL = 2048
D_PLE = 256
D_FF = 5504
D_CONV = 1024
CONV_WIDTH = 31
HEAD_DIM = 64
N_HEADS = 16
N_KV = 4
GROUP = N_HEADS // N_KV
ROT_DIM = 16
ROPE_THETA = 500000.0
CHUNK = 64
WINDOW = 128
PAST_LEN = 1024
EPS = 1e-6
NEG = -1e30
LOG2E = 1.4426950408889634
Q_W = N_HEADS * HEAD_DIM
KV_W = N_KV * HEAD_DIM
IN_COLS = 2 * D_CONV + Q_W + 2 * KV_W + 2 * D_MODEL

LANES = 128
SUBLANES = 8
BF16_ROWS = 16
FF1_TILE = 512
FF2_TILE = 1024
CAST_TILE = 256
WGU_CAST_ROWS = 16
WD_CAST_ROWS = 64
ROW_TILE = 512
IN_TILE = 1088
IN_COL_TILE = 1024
UQKV_TILE = 544
SAMPLE_ROWS = 256
PLE_TILE = 512
CONV_HALO = 32
MIX_ROWS = 256
MIX_TILE = WINDOW
VMEM_LIMIT = 56 * 1024 * 1024

F32 = jnp.float32
BF16 = jnp.bfloat16


def _params(*sem):
    return pltpu.CompilerParams(dimension_semantics=sem, vmem_limit_bytes=VMEM_LIMIT)


def _rms(x, g):
    return x * lax.rsqrt(jnp.mean(x * x, axis=-1, keepdims=True) + EPS) * g


def _dot(a, b):
    return jnp.dot(a, b, preferred_element_type=F32)


def _resident(shape):
    return pl.BlockSpec(shape, lambda *_: (0,) * len(shape), pipeline_mode=pl.Buffered(1))


def _side_cast_specs(side, n_steps, step_of):
    in_specs, out_shape = [], []
    for a, rows in side:
        n_blocks = a.shape[0] // rows
        assert a.shape[0] % rows == 0 and n_blocks <= n_steps
        in_specs.append(pl.BlockSpec((rows, a.shape[1]),
                                     lambda *g, n=n_blocks: (jnp.minimum(step_of(*g), n - 1), 0)))
        out_shape.append(jax.ShapeDtypeStruct(a.shape, BF16))
    return in_specs, out_shape


def _side_cast(src_refs, dst_refs):
    for src, dst in zip(src_refs, dst_refs, strict=True):
        dst[...] = src[...].astype(BF16)


def _swiglu_accumulate(xn_ref, xo_ref, wg, wu, wd):
    xn = xn_ref[...]
    g = _dot(xn, wg)
    u = _dot(xn, wu)
    h = (g * jax.nn.sigmoid(g)) * (u * 0.5)
    xo_ref[...] += _dot(h.astype(BF16), wd)


def _ff_steps(tile):
    return -(-D_FF // tile)


def _ff_start(f, tile):
    return pl.multiple_of(jnp.minimum(f * tile, D_FF - tile), LANES)


def _ffn_first_kernel(with_hn, x_ref, g1_ref, wg_ref, wu_ref, wd_ref, g2_ref, xo_ref, *rest):
    hn_ref = rest[0] if with_hn else None
    wgb_ref, wub_ref, wdb_ref, xn_ref = rest[with_hn:]
    f = pl.program_id(0)
    steps = _ff_steps(CAST_TILE)
    wg, wu, wd = wg_ref[...].astype(BF16), wu_ref[...].astype(BF16), wd_ref[...].astype(BF16)

    @pl.when(f == 0)
    def _():
        x = x_ref[...]
        xn_ref[...] = _rms(x, g1_ref[...]).astype(BF16)
        xo_ref[...] = x

    @pl.when(f < steps - 1)
    def _():
        wgb_ref[...] = wg
        wub_ref[...] = wu
        wdb_ref[...] = wd
        _swiglu_accumulate(xn_ref, xo_ref, wg, wu, wd)

    @pl.when(f == steps - 1)
    def _():
        old = steps * CAST_TILE - D_FF
        wg_new, wu_new, wd_new = wg[:, old:], wu[:, old:], wd[old:, :]
        wgb_ref[...] = jnp.concatenate([wg_new, jnp.zeros((D_MODEL, old), BF16)], axis=1)
        wub_ref[...] = jnp.concatenate([wu_new, jnp.zeros((D_MODEL, old), BF16)], axis=1)
        wdb_ref[...] = jnp.concatenate([wd_new, jnp.zeros((old, D_MODEL), BF16)], axis=0)
        _swiglu_accumulate(xn_ref, xo_ref, wg_new, wu_new, wd_new)

    if with_hn:
        @pl.when(f == steps - 1)
        def _():
            hn_ref[...] = _rms(xo_ref[...], g2_ref[...]).astype(BF16)


def _ffn_first(x, g1, w_gu, w_down, g2, with_hn, name):
    assert x.shape[0] == ROW_TILE
    tile = CAST_TILE
    cols = (pl.Element(D_MODEL), pl.Element(tile))
    rows = (pl.Element(tile), pl.Element(D_MODEL))
    vec = pl.BlockSpec((1, D_MODEL), lambda f: (0, 0))
    row = pl.BlockSpec((ROW_TILE, D_MODEL), lambda f: (0, 0))
    gate_win = pl.BlockSpec(cols, lambda f: (0, _ff_start(f, tile)))
    up_win = pl.BlockSpec(cols, lambda f: (0, pl.multiple_of(D_FF + _ff_start(f, tile), LANES)))
    down_win = pl.BlockSpec(rows, lambda f: (_ff_start(f, tile), 0))
    out_shape = [jax.ShapeDtypeStruct((ROW_TILE, D_MODEL), F32)]
    if with_hn:
        out_shape.append(jax.ShapeDtypeStruct((ROW_TILE, D_MODEL), BF16))
    steps = _ff_steps(tile)
    weights_shape = ([jax.ShapeDtypeStruct((D_MODEL, steps * tile), BF16)] * 2
                     + [jax.ShapeDtypeStruct((steps * tile, D_MODEL), BF16)])
    col_blk = pl.BlockSpec((D_MODEL, tile), lambda f: (0, f))
    row_blk = pl.BlockSpec((tile, D_MODEL), lambda f: (f, 0))
    return pl.pallas_call(
        functools.partial(_ffn_first_kernel, with_hn),
        grid=(steps,),
        in_specs=[row, vec, gate_win, up_win, down_win, vec],
        out_specs=[row] * len(out_shape) + [col_blk, col_blk, row_blk],
        out_shape=out_shape + weights_shape,
        scratch_shapes=[pltpu.VMEM((ROW_TILE, D_MODEL), BF16)],
        compiler_params=_params("arbitrary"),
        name=name,
    )(x, g1, w_gu, w_gu, w_down, g2)


def _ffn_kernel(n_prompt_tiles, with_hn, tail_done, n_side, steps, xp_ref, xs_ref, *rest):
    if tail_done:
        hns_ref, rest = rest[0], rest[1:]
    g1_ref, wg_ref, wu_ref, wd_ref, wg_tail_ref, wu_tail_ref, wd_tail_ref, g2_ref = rest[:8]
    side_in, outs = rest[8:8 + n_side], rest[8 + n_side:]
    xo_ref = outs[0]
    hn_ref = outs[1] if with_hn else None
    side_out = outs[1 + with_hn:1 + with_hn + n_side]
    xn_ref = outs[-1]
    i, f = pl.program_id(0), pl.program_id(1)
    _side_cast(side_in, side_out)

    def on(cond):
        return jnp.logical_and(cond, i < n_prompt_tiles) if tail_done else cond

    @pl.when(on(f == 0))
    def _():
        x = xp_ref[...] if tail_done else jnp.where(i < n_prompt_tiles, xp_ref[...], xs_ref[...])
        xn_ref[...] = _rms(x, g1_ref[...]).astype(BF16)
        xo_ref[...] = x

    if tail_done:
        @pl.when(jnp.logical_and(f == 0, i >= n_prompt_tiles))
        def _():
            xo_ref[...] = xs_ref[...]
            hn_ref[...] = hns_ref[...]

    pl.when(on(f < steps - 1))(
        lambda: _swiglu_accumulate(xn_ref, xo_ref, wg_ref[...], wu_ref[...], wd_ref[...]))

    @pl.when(on(f == steps - 1))
    def _():
        _swiglu_accumulate(xn_ref, xo_ref, wg_ref[...], wu_ref[...], wd_ref[...])
        _swiglu_accumulate(xn_ref, xo_ref, wg_tail_ref[...], wu_tail_ref[...], wd_tail_ref[...])
        if with_hn:
            hn_ref[...] = _rms(xo_ref[...], g2_ref[...]).astype(BF16)


def _ffn(xp, xs, hn_s, g1, wg, wu, wd, g2, with_hn, name, tile, side=()):
    n_p, n_s = xp.shape[0] // ROW_TILE, xs.shape[0] // ROW_TILE
    assert n_s == 1 and xs.shape[0] == ROW_TILE
    m = xp.shape[0] + xs.shape[0]
    tail_done = hn_s is not None
    steps = D_FF // tile
    tail0 = steps * tile
    n_tail = D_FF - tail0
    assert with_hn or not tail_done

    def start(i, f):
        return pl.multiple_of((jnp.where(i < n_p, f, steps - 1) if tail_done else f) * tile, LANES)

    cols = (pl.Element(D_MODEL), pl.Element(tile))
    rows = (pl.Element(tile), pl.Element(D_MODEL))
    once = pl.Buffered(1)
    tail_w = [pl.BlockSpec((pl.Element(D_MODEL), pl.Element(n_tail)), lambda i, f: (0, wg[1] + tail0),
                           pipeline_mode=once),
              pl.BlockSpec((pl.Element(D_MODEL), pl.Element(n_tail)), lambda i, f: (0, wu[1] + tail0),
                           pipeline_mode=once),
              pl.BlockSpec((pl.Element(n_tail), pl.Element(D_MODEL)), lambda i, f: (tail0, 0),
                           pipeline_mode=once)]
    vec = pl.BlockSpec((1, D_MODEL), lambda i, f: (0, 0))
    row = pl.BlockSpec((ROW_TILE, D_MODEL), lambda i, f: (i, 0))
    if tail_done:
        tail_specs, tail_args = [_resident((ROW_TILE, D_MODEL))] * 2, [xs, hn_s]
    else:
        tail_specs, tail_args = [pl.BlockSpec((ROW_TILE, D_MODEL), lambda i, f: (0, 0))], [xs]
    out_shape = [jax.ShapeDtypeStruct((m, D_MODEL), F32)]
    if with_hn:
        out_shape.append(jax.ShapeDtypeStruct((m, D_MODEL), BF16))
    side_specs, side_shape = _side_cast_specs(side, (n_p + n_s) * steps, lambda i, f: i * steps + f)
    return pl.pallas_call(
        functools.partial(_ffn_kernel, n_p, with_hn, tail_done, len(side), steps),
        grid=(n_p + n_s, steps),
        in_specs=[pl.BlockSpec((ROW_TILE, D_MODEL), lambda i, f: (jnp.minimum(i, n_p - 1), 0)),
                  *tail_specs, vec,
                  pl.BlockSpec(cols, lambda i, f: (0, pl.multiple_of(wg[1] + start(i, f), LANES))),
                  pl.BlockSpec(cols, lambda i, f: (0, pl.multiple_of(wu[1] + start(i, f), LANES))),
                  pl.BlockSpec(rows, lambda i, f: (start(i, f), 0)),
                  *tail_w, vec, *side_specs],
        out_specs=[row] * len(out_shape) + side_specs,
        out_shape=out_shape + side_shape,
        scratch_shapes=[pltpu.VMEM((ROW_TILE, D_MODEL), BF16)],
        compiler_params=_params("arbitrary", "arbitrary"),
        name=name,
    )(xp, *tail_args, g1, wg[0], wu[0], wd, wg[0], wu[0], wd, g2, *[a for a, _ in side])


def _gates_kernel(n_side, h_ref, w_ref, *rest):
    side_in, o_ref, side_out = rest[:n_side], rest[n_side], rest[n_side + 1:]
    _side_cast(side_in, side_out)
    o_ref[...] = jax.nn.sigmoid(_dot(h_ref[...], w_ref[...])).astype(BF16)


def _gates(hn, w_in, side=()):
    m, tn = hn.shape[0], IN_COL_TILE
    col0 = 2 * D_CONV + Q_W + 2 * KV_W
    nj = 2 * D_MODEL // tn
    side_specs, side_shape = _side_cast_specs(side, m // IN_TILE * nj, lambda i, j: i * nj + j)
    return pl.pallas_call(
        functools.partial(_gates_kernel, len(side)),
        grid=(m // IN_TILE, nj),
        in_specs=[pl.BlockSpec((IN_TILE, D_MODEL), lambda i, j: (i, 0)),
                  pl.BlockSpec((pl.Element(D_MODEL), pl.Element(tn)),
                               lambda i, j: (0, pl.multiple_of(col0 + j * tn, LANES))),
                  *side_specs],
        out_specs=[pl.BlockSpec((IN_TILE, tn), lambda i, j: (i, j))] + side_specs,
        out_shape=[jax.ShapeDtypeStruct((m, 2 * D_MODEL), BF16)] + side_shape,
        compiler_params=_params("arbitrary", "arbitrary"),
        name="gates",
    )(hn, w_in, *[a for a, _ in side])


def _rotate(x, cos, sin_lo, sin_hi):
    cols = []
    for c in range(x.shape[1] // LANES):
        xb = x[:, c * LANES:(c + 1) * LANES]
        cols.append(xb * cos
                    + pltpu.roll(xb, LANES - ROT_DIM // 2, 1) * sin_lo
                    + pltpu.roll(xb, ROT_DIM // 2, 1) * sin_hi)
    return jnp.concatenate(cols, axis=1)


def _uqkv_kernel(h_ref, w_ref, cos_ref, slo_ref, shi_ref, u_ref, q_ref, k_ref, v_ref):
    h = h_ref[...]
    cos, slo, shi = cos_ref[...], slo_ref[...], shi_ref[...]
    c0 = 2 * D_CONV
    u_ref[...] = _dot(h, w_ref[:, :D_CONV]) * jax.nn.sigmoid(_dot(h, w_ref[:, D_CONV:c0]))
    q = _rotate(_dot(h, w_ref[:, c0:c0 + Q_W]), cos, slo, shi)
    q_ref[...] = (q * (LOG2E * HEAD_DIM ** -0.5)).astype(BF16)
    k_ref[...] = _rotate(_dot(h, w_ref[:, c0 + Q_W:c0 + Q_W + KV_W]), cos, slo, shi)
    v_ref[...] = _dot(h, w_ref[:, c0 + Q_W + KV_W:])


def _uqkv(hn, w_in, cos, slo, shi):
    m, tm = hn.shape[0], UQKV_TILE
    n_w = 2 * D_CONV + Q_W + 2 * KV_W
    tab = pl.BlockSpec((tm, LANES), lambda i: (i, 0))
    return pl.pallas_call(
        _uqkv_kernel,
        grid=(m // tm,),
        in_specs=[pl.BlockSpec((tm, D_MODEL), lambda i: (i, 0)),
                  pl.BlockSpec((pl.Element(D_MODEL), pl.Element(n_w)), lambda i: (0, 0),
                               pipeline_mode=pl.Buffered(1)),
                  tab, tab, tab],
        out_specs=[pl.BlockSpec((tm, D_CONV), lambda i: (i, 0)),
                   pl.BlockSpec((tm, Q_W), lambda i: (i, 0)),
                   pl.BlockSpec((tm, KV_W), lambda i: (i, 0)),
                   pl.BlockSpec((tm, KV_W), lambda i: (i, 0))],
        out_shape=[jax.ShapeDtypeStruct((m, D_CONV), F32),
                   jax.ShapeDtypeStruct((m, Q_W), BF16),
                   jax.ShapeDtypeStruct((m, KV_W), F32),
                   jax.ShapeDtypeStruct((m, KV_W), F32)],
        compiler_params=_params("parallel"),
        name="uqkv",
    )(hn, w_in, cos, slo, shi)


def _rope_tables(batch, seq, dec_batch, dec_seq):
    inv = ROPE_THETA ** (-jnp.arange(0, ROT_DIM, 2, dtype=F32) / ROT_DIM)
    pos = jnp.concatenate([jnp.tile(jnp.arange(seq, dtype=jnp.int32), batch),
                           jnp.tile(PAST_LEN + jnp.arange(dec_seq, dtype=jnp.int32), dec_batch)])
    ang = pos.astype(F32)[:, None] * inv[None, :]
    cos, sin = jnp.cos(ang), jnp.sin(ang)
    one = jnp.ones((pos.shape[0], HEAD_DIM - ROT_DIM), F32)
    zero = jnp.zeros_like(one)
    zh = jnp.zeros_like(sin)
    c = jnp.concatenate([cos, cos, one], 1)
    lo = jnp.concatenate([-sin, zh, zero], 1)
    hi = jnp.concatenate([zh, sin, zero], 1)
    return [jnp.tile(t, (1, LANES // HEAD_DIM)) for t in (c, lo, hi)]


def _conv_stage(s_ref, history, u_ref):
    rows0 = CONV_HALO - history.shape[0]
    for cb in range(D_CONV // LANES):
        lanes = slice(cb * LANES, (cb + 1) * LANES)
        s_ref[cb, rows0:CONV_HALO, :] = history[:, lanes]
        s_ref[cb, CONV_HALO:, :] = u_ref[:, lanes]


def _conv_tile(s_ref, row0, c_ref, w_ref, b_ref, lg_ref, lb_ref, rows):
    base = CONV_HALO - (CONV_WIDTH - 1)
    n = rows // SUBLANES
    for cb in range(D_CONV // LANES):
        lanes = slice(cb * LANES, (cb + 1) * LANES)
        accs = [jnp.broadcast_to(b_ref[:, lanes], (n, LANES)) for _ in range(SUBLANES)]
        for j in range(CONV_WIDTH):
            w = jnp.broadcast_to(w_ref[j:j + 1, lanes], (n, LANES))
            for p in range(SUBLANES):
                accs[p] = accs[p] + s_ref[cb, pl.ds(row0 + base + j + p, n, stride=SUBLANES), :] * w
        for p in range(SUBLANES):
            c_ref[cb, pl.ds(p, n, stride=SUBLANES), :] = accs[p]
    acc = jnp.concatenate([c_ref[cb] for cb in range(D_CONV // LANES)], axis=1)
    mu = jnp.mean(acc, axis=-1, keepdims=True)
    xc = acc - mu
    y = xc * lax.rsqrt(jnp.mean(xc * xc, axis=-1, keepdims=True) + EPS) * lg_ref[...] + lb_ref[...]
    return (y * jax.nn.sigmoid(y)).astype(BF16)


def _conv_common_specs():
    return [pl.BlockSpec((CONV_WIDTH, D_CONV), lambda i: (0, 0)),
            pl.BlockSpec((1, D_CONV), lambda i: (0, 0)),
            pl.BlockSpec((1, D_CONV), lambda i: (0, 0)),
            pl.BlockSpec((1, D_CONV), lambda i: (0, 0))]


def _conv_scratch(block_rows, tile_rows):
    return [pltpu.VMEM((D_CONV // LANES, CONV_HALO + block_rows, LANES), F32),
            pltpu.VMEM((D_CONV // LANES, tile_rows, LANES), F32)]


def _attn_tile(q, k, v, sink_ref, valid):
    r = q.shape[0]
    lane_head = lax.broadcasted_iota(jnp.int32, (1, KV_W), 1) // HEAD_DIM
    sel_row = lax.broadcasted_iota(jnp.int32, (KV_W, Q_W), 0)
    sel_col = lax.broadcasted_iota(jnp.int32, (KV_W, Q_W), 1)
    sel = jnp.where(sel_row == sel_col // KV_W * HEAD_DIM + sel_col % HEAD_DIM, 1.0, 0.0).astype(BF16)
    krep = _dot(k, sel).astype(BF16)
    vrep = _dot(v, sel).astype(BF16)
    head_mask = [jnp.where(lane_head == g, 1.0, 0.0).astype(BF16) for g in range(GROUP)]
    scores = []
    for kv in range(N_KV):
        cols = slice(kv * KV_W, (kv + 1) * KV_W)
        lhs = jnp.concatenate([q[:, cols] * head_mask[g] for g in range(GROUP)], axis=0)
        scores.append(lax.dot_general(lhs, krep[:, cols], (((1,), (1,)), ((), ())),
                                      preferred_element_type=F32))
    weights, dens = [], []
    for kv in range(N_KV):
        es = []
        for g in range(GROUP):
            sg = scores[kv][g * r:(g + 1) * r]
            if valid is not None:
                sg = jnp.where(valid, sg, NEG)
            sink = sink_ref[kv * GROUP + g] * LOG2E
            mx = jnp.maximum(jnp.max(sg, axis=-1, keepdims=True), sink)
            e = jnp.exp2(sg - mx)
            dens.append(jnp.sum(e, axis=-1, keepdims=True) + jnp.exp2(sink - mx))
            es.append(e.astype(BF16))
        weights.append(jnp.concatenate(es, axis=0))
    outs = []
    for kv in range(N_KV):
        o4 = _dot(weights[kv], vrep[:, kv * KV_W:(kv + 1) * KV_W])
        og = [o4[g * r:(g + 1) * r] / dens[kv * GROUP + g] for g in range(GROUP)]
        oh = og[GROUP - 1]
        for g in range(GROUP - 2, -1, -1):
            oh = jnp.where(lane_head == g, og[g], oh)
        outs.append(oh)
    return jnp.concatenate(outs, axis=1).astype(BF16)


def _merge_out(yc, o, gc, ga, x, wc_ref, wa_ref, wo_ref):
    m = gc.astype(F32) * _dot(yc, wc_ref[...]) + ga.astype(F32) * _dot(o, wa_ref[...])
    return x + _dot(m.astype(BF16), wo_ref[...])


def _mix_kernel(per_seq, sink_ref, u_ref, uh_ref, q_ref, kp_ref, kc_ref, vp_ref, vc_ref,
                cw_ref, cb_ref, lg_ref, lb_ref, gc_ref, ga_ref, x_ref, wc_ref, wa_ref, wo_ref,
                out_ref, s_ref, c_ref):
    first = pl.program_id(0) % per_seq == 0
    _conv_stage(s_ref, jnp.where(first, 0.0, uh_ref[...]), u_ref)

    k = jnp.concatenate([kp_ref[...], kc_ref[...]], axis=0).astype(BF16)
    v = jnp.concatenate([vp_ref[...], vc_ref[...]], axis=0).astype(BF16)
    nk = WINDOW + MIX_TILE
    col = lax.broadcasted_iota(jnp.int32, (MIX_TILE, nk), 1)
    d = col // CHUNK - lax.broadcasted_iota(jnp.int32, (MIX_TILE, nk), 0) // CHUNK
    band = (d >= 0) & (d <= WINDOW // CHUNK)

    for r0 in range(0, MIX_ROWS, MIX_TILE):
        rows = slice(r0, r0 + MIX_TILE)
        yc = _conv_tile(s_ref, r0, c_ref, cw_ref, cb_ref, lg_ref, lb_ref, MIX_TILE)
        valid = band & ((col >= WINDOW) | jnp.logical_not(first)) if r0 == 0 else band
        o = _attn_tile(q_ref[rows, :], k[r0:r0 + nk], v[r0:r0 + nk], sink_ref, valid)
        out_ref[rows, :] = _merge_out(yc, o, gc_ref[rows, :], ga_ref[rows, :], x_ref[rows, :],
                                      wc_ref, wa_ref, wo_ref)


def _mix_prompt(u, q, k, v, gates, x, sinks, cw, cb, lg, lb, wc, wa, wo, n_rows, seq):
    rows = MIX_ROWS
    halo_per_block = rows // CONV_HALO
    pre_per_block = rows // WINDOW
    kv_prev = pl.BlockSpec((WINDOW, KV_W), lambda i: (jnp.maximum(i * pre_per_block - 1, 0), 0))
    kv_cur = pl.BlockSpec((rows, KV_W), lambda i: (i, 0))
    return pl.pallas_call(
        functools.partial(_mix_kernel, seq // rows),
        grid=(n_rows // rows,),
        in_specs=[pl.BlockSpec(memory_space=pltpu.SMEM),
                  pl.BlockSpec((rows, D_CONV), lambda i: (i, 0)),
                  pl.BlockSpec((CONV_HALO, D_CONV), lambda i: (jnp.maximum(i * halo_per_block - 1, 0), 0)),
                  pl.BlockSpec((rows, Q_W), lambda i: (i, 0)),
                  kv_prev, kv_cur, kv_prev, kv_cur,
                  *_conv_common_specs(),
                  pl.BlockSpec((rows, D_MODEL), lambda i: (i, 0)),
                  pl.BlockSpec((rows, D_MODEL), lambda i: (i, 1)),
                  pl.BlockSpec((rows, D_MODEL), lambda i: (i, 0)),
                  _resident((D_CONV, D_MODEL)), _resident((Q_W, D_MODEL)), _resident((D_MODEL, D_MODEL))],
        out_specs=pl.BlockSpec((rows, D_MODEL), lambda i: (i, 0)),
        out_shape=jax.ShapeDtypeStruct((n_rows, D_MODEL), F32),
        scratch_shapes=_conv_scratch(rows, MIX_TILE),
        compiler_params=_params("parallel"),
        name="mix_prompt",
    )(sinks, u, u, q, k, k, v, v, cw, cb, lg, lb, gates, gates, x, wc, wa, wo)


def _mix_sample_kernel(sink_ref, u_ref, past_ref, q_ref, kp_ref, kc_ref, vp_ref, vc_ref,
                       cw_ref, cb_ref, lg_ref, lb_ref, gc_ref, ga_ref, x_ref, wc_ref, wa_ref, wo_ref,
                       out_ref, s_ref, c_ref):
    ycs, os_ = [], []
    for s in range(u_ref.shape[0] // CHUNK):
        rows = slice(s * CHUNK, (s + 1) * CHUNK)
        _conv_stage(s_ref, past_ref[s], u_ref[rows, :])
        ycs.append(_conv_tile(s_ref, 0, c_ref, cw_ref, cb_ref, lg_ref, lb_ref, CHUNK))
        k = jnp.concatenate([kp_ref[s], kc_ref[rows, :]], axis=0).astype(BF16)
        v = jnp.concatenate([vp_ref[s], vc_ref[rows, :]], axis=0).astype(BF16)
        os_.append(_attn_tile(q_ref[rows, :], k, v, sink_ref, None))
    out_ref[...] = _merge_out(jnp.concatenate(ycs, axis=0), jnp.concatenate(os_, axis=0),
                              gc_ref[...], ga_ref[...], x_ref[...], wc_ref, wa_ref, wo_ref)


def _mix_sample(u, q, k, v, gates, x, sinks, cw, cb, lg, lb, wc, wa, wo, state, cache_k, cache_v, row0):
    n_seq, hist, _ = state.shape
    assert hist == CONV_WIDTH - 1 and cache_k.shape[1] == WINDOW
    rows = SAMPLE_ROWS
    seqs = rows // CHUNK
    blk0 = row0 // rows

    def cur(width, col=0):
        return pl.BlockSpec((rows, width), lambda i: (blk0 + i, col))

    cache = pl.BlockSpec((seqs, WINDOW, KV_W), lambda i: (i, 0, 0))
    return pl.pallas_call(
        _mix_sample_kernel,
        grid=(n_seq // seqs,),
        in_specs=[pl.BlockSpec(memory_space=pltpu.SMEM),
                  cur(D_CONV), pl.BlockSpec((seqs, hist, D_CONV), lambda i: (i, 0, 0)), cur(Q_W),
                  cache, cur(KV_W), cache, cur(KV_W),
                  *_conv_common_specs(),
                  cur(D_MODEL), cur(D_MODEL, 1), cur(D_MODEL),
                  _resident((D_CONV, D_MODEL)), _resident((Q_W, D_MODEL)), _resident((D_MODEL, D_MODEL))],
        out_specs=pl.BlockSpec((rows, D_MODEL), lambda i: (i, 0)),
        out_shape=jax.ShapeDtypeStruct((n_seq * CHUNK, D_MODEL), F32),
        scratch_shapes=_conv_scratch(CHUNK, CHUNK),
        compiler_params=_params("parallel"),
        name="mix_sample",
    )(sinks, u, state, q, cache_k, k, cache_v, v, cw, cb, lg, lb, gates, gates, x, wc, wa, wo)


def _ple_kernel(n_prompt_tiles, x_ref, pp_ref, ps_ref, gn_ref, wg_ref, wp_ref, fn_ref, yp_ref, ys_ref):
    i = pl.program_id(0)
    x = x_ref[...]
    gate = jax.nn.sigmoid(_dot(_rms(x, gn_ref[...]).astype(BF16), wg_ref[...]))
    pe = jnp.where(i < n_prompt_tiles, pp_ref[...], ps_ref[...]).astype(BF16)
    y = _rms(x + gate * _dot(pe, wp_ref[...]), fn_ref[...])

    @pl.when(i < n_prompt_tiles)
    def _():
        yp_ref[...] = y

    @pl.when(i >= n_prompt_tiles)
    def _():
        ys_ref[...] = y


def _ple(x, pp, ps, gn, wg, wp, fn):
    tm = PLE_TILE
    n_p, n_s = pp.shape[0] // tm, ps.shape[0] // tm
    vec = pl.BlockSpec((1, D_MODEL), lambda i: (0, 0))

    def p_idx(i):
        return (jnp.minimum(i, n_p - 1), 0)

    def s_idx(i):
        return (jnp.maximum(i - n_p, 0), 0)

    return pl.pallas_call(
        functools.partial(_ple_kernel, n_p),
        grid=(n_p + n_s,),
        in_specs=[pl.BlockSpec((tm, D_MODEL), lambda i: (i, 0)),
                  pl.BlockSpec((tm, D_PLE), p_idx),
                  pl.BlockSpec((tm, D_PLE), s_idx),
                  vec, _resident((D_MODEL, D_MODEL)), _resident((D_PLE, D_MODEL)), vec],
        out_specs=[pl.BlockSpec((tm, D_MODEL), p_idx), pl.BlockSpec((tm, D_MODEL), s_idx)],
        out_shape=[jax.ShapeDtypeStruct((pp.shape[0], D_MODEL), F32),
                   jax.ShapeDtypeStruct((ps.shape[0], D_MODEL), F32)],
        compiler_params=_params("arbitrary"),
        name="ple_final",
    )(x, pp, ps, gn, wg, wp, fn)


def kernel(x_prompt, x_sample, p_prompt, p_sample, state_conv, cache_k, cache_v, ffn1_norm, ffn1_w_gu, ffn1_w_down, mix_norm, w_in, conv_w, conv_b, conv_ln_g, conv_ln_b, conv_w_out, attn_sinks, attn_w_out, w_out, ffn2_norm, ffn2_w_gu, ffn2_w_down, ple_norm, ple_w_gate, ple_w_proj, final_norm):
    assert x_prompt.shape[-1] == D_MODEL and ffn1_norm.shape[0] == 1 and w_in.shape[-1] == IN_COLS
    batch, seq, _ = x_prompt.shape
    dec_batch, dec_seq, _ = x_sample.shape
    assert dec_seq == CHUNK and seq % ROW_TILE == 0
    n_p, n_s = batch * seq, dec_batch * dec_seq

    def vec(a):
        return a.reshape(1, -1)

    xp = x_prompt.reshape(n_p, D_MODEL)
    xs = x_sample.reshape(n_s, D_MODEL)

    x1_s, hn_s, wg1, wu1, wd1 = _ffn_first(xs, vec(ffn1_norm), ffn1_w_gu[0], ffn1_w_down[0], vec(mix_norm),
                                           True, "ffn1_first")
    x1, hn, wgu2, wd2, w_in_b = _ffn(
        xp, x1_s, hn_s, vec(ffn1_norm), (wg1, 0), (wu1, 0), wd1, vec(mix_norm), True, "ffn1", FF1_TILE,
        side=[(ffn2_w_gu[0], WGU_CAST_ROWS), (ffn2_w_down[0], WD_CAST_ROWS), (w_in[0], WGU_CAST_ROWS)])

    cos, slo, shi = _rope_tables(batch, seq, dec_batch, dec_seq)
    u, q, k, v = _uqkv(hn, w_in_b, cos, slo, shi)
    gate_steps = (n_p + n_s) // IN_TILE * (2 * D_MODEL // IN_COL_TILE)
    gates, w_co, w_ao, w_o, w_pg, w_pp = _gates(
        hn, w_in_b, side=[(w, max(w.shape[0] // gate_steps, BF16_ROWS))
                           for w in (conv_w_out[0], attn_w_out[0], w_out[0], ple_w_gate[0], ple_w_proj[0])])

    cw, cb, lg, lb = conv_w[0], vec(conv_b), vec(conv_ln_g), vec(conv_ln_b)
    sinks = attn_sinks[0]
    ck = cache_k[0].reshape(dec_batch, -1, KV_W)
    cv = cache_v[0].reshape(dec_batch, -1, KV_W)
    mix_args = (u, q, k, v, gates, x1, sinks, cw, cb, lg, lb, w_co, w_ao, w_o)
    x2_p = _mix_prompt(*mix_args, n_p, seq)
    x2_s = _mix_sample(*mix_args, state_conv[0], ck, cv, n_p)
    x3, = _ffn(x2_p, x2_s, None, vec(ffn2_norm), (wgu2, 0), (wgu2, D_FF), wd2, vec(ffn2_norm), False, "ffn2",
               FF2_TILE)

    yp, ys = _ple(x3, p_prompt[0].reshape(n_p, D_PLE), p_sample[0].reshape(n_s, D_PLE),
                  vec(ple_norm), w_pg, w_pp, vec(final_norm))

    w_rows = ck.shape[1]

    def prompt_tail(a, rows):
        return jnp.stack([a[(b + 1) * seq - rows:(b + 1) * seq] for b in range(batch)])

    k_p = prompt_tail(k, WINDOW).reshape(batch, WINDOW, N_KV, HEAD_DIM)
    v_p = prompt_tail(v, WINDOW).reshape(batch, WINDOW, N_KV, HEAD_DIM)
    c_p = prompt_tail(u, CONV_WIDTH - 1)
    k_s = jnp.concatenate([cache_k[0], k[n_p:].reshape(dec_batch, dec_seq, N_KV, HEAD_DIM)], 1)[:, -w_rows:]
    v_s = jnp.concatenate([cache_v[0], v[n_p:].reshape(dec_batch, dec_seq, N_KV, HEAD_DIM)], 1)[:, -w_rows:]
    c_s = jnp.concatenate([state_conv[0], u[n_p:].reshape(dec_batch, dec_seq, D_CONV)], 1)[:, -(CONV_WIDTH - 1):]
    return (yp.reshape(batch, seq, D_MODEL), ys.reshape(dec_batch, dec_seq, D_MODEL),
            k_p[None], v_p[None], c_p[None], k_s[None], v_s[None], c_s[None])
```

```python
import functools

import jax
import jax.numpy as jnp
from jax import lax
from jax.experimental import pallas as pl
from jax.experimental.pallas import tpu as pltpu

D_MODEL = 2048
D_PLE = 256
D_FF = 5504
D_CONV = 1024
CONV_WIDTH = 31
HEAD_DIM = 64
N_HEADS = 16
N_KV = 4
GROUP = N_HEADS // N_KV
ROT_DIM = 16
ROPE_THETA = 500000.0
CHUNK = 64
WINDOW = 128
PAST_LEN = 1024
EPS = 1e-6
NEG = -1e30
LOG2E = 1.4426950408889634
Q_W = N_HEADS * HEAD_DIM
KV_W = N_KV * HEAD_DIM
IN_COLS = 2 * D_CONV + Q_W + 2 * KV_W + 2 * D_MODEL

LANES = 128
SUBLANES = 8
BF16_ROWS = 16
FF1_TILE = 512
FF2_TILE = 768
CAST_TILE = 256
WGU_CAST_ROWS = 16
WD_CAST_ROWS = 64
ROW_TILE = 512
IN_TILE = 1088
IN_COL_TILE = 1024
UQKV_TILE = 544
SAMPLE_ROWS = 256
PLE_TILE = 512
CONV_HALO = 32
MIX_ROWS = 256
MIX_TILE = WINDOW
VMEM_LIMIT = 56 * 1024 * 1024

F32 = jnp.float32
BF16 = jnp.bfloat16


def _params(*sem):
    return pltpu.CompilerParams(dimension_semantics=sem, vmem_limit_bytes=VMEM_LIMIT)


def _rms(x, g):
    return x * lax.rsqrt(jnp.mean(x * x, axis=-1, keepdims=True) + EPS) * g


def _dot(a, b):
    return jnp.dot(a, b, preferred_element_type=F32)


def _resident(shape):
    return pl.BlockSpec(shape, lambda *_: (0,) * len(shape), pipeline_mode=pl.Buffered(1))


def _side_cast_specs(side, n_steps, step_of):
    in_specs, out_shape = [], []
    for a, rows in side:
        n_blocks = a.shape[0] // rows
        assert a.shape[0] % rows == 0 and n_blocks <= n_steps
        in_specs.append(pl.BlockSpec((rows, a.shape[1]),
                                     lambda *g, n=n_blocks: (jnp.minimum(step_of(*g), n - 1), 0)))
        out_shape.append(jax.ShapeDtypeStruct(a.shape, BF16))
    return in_specs, out_shape


def _side_cast(src_refs, dst_refs):
    for src, dst in zip(src_refs, dst_refs, strict=True):
        dst[...] = src[...].astype(BF16)


def _swiglu_accumulate(xn_ref, xo_ref, wg, wu, wd):
    xn = xn_ref[...]
    g = _dot(xn, wg)
    u = _dot(xn, wu)
    h = (g * jax.nn.sigmoid(g)) * (u * 0.5)
    xo_ref[...] += _dot(h.astype(BF16), wd)


def _ff_steps(tile):
    return -(-D_FF // tile)


def _ff_start(f, tile):
    return pl.multiple_of(jnp.minimum(f * tile, D_FF - tile), LANES)


def _ffn_first_kernel(with_hn, x_ref, g1_ref, wg_ref, wu_ref, wd_ref, g2_ref, xo_ref, *rest):
    hn_ref = rest[0] if with_hn else None
    wgb_ref, wub_ref, wdb_ref, xn_ref = rest[with_hn:]
    f = pl.program_id(0)
    steps = _ff_steps(CAST_TILE)
    wg, wu, wd = wg_ref[...].astype(BF16), wu_ref[...].astype(BF16), wd_ref[...].astype(BF16)

    @pl.when(f == 0)
    def _():
        x = x_ref[...]
        xn_ref[...] = _rms(x, g1_ref[...]).astype(BF16)
        xo_ref[...] = x

    @pl.when(f < steps - 1)
    def _():
        wgb_ref[...] = wg
        wub_ref[...] = wu
        wdb_ref[...] = wd
        _swiglu_accumulate(xn_ref, xo_ref, wg, wu, wd)

    @pl.when(f == steps - 1)
    def _():
        old = steps * CAST_TILE - D_FF
        wg_new, wu_new, wd_new = wg[:, old:], wu[:, old:], wd[old:, :]
        wgb_ref[...] = jnp.concatenate([wg_new, jnp.zeros((D_MODEL, old), BF16)], axis=1)
        wub_ref[...] = jnp.concatenate([wu_new, jnp.zeros((D_MODEL, old), BF16)], axis=1)
        wdb_ref[...] = jnp.concatenate([wd_new, jnp.zeros((old, D_MODEL), BF16)], axis=0)
        _swiglu_accumulate(xn_ref, xo_ref, wg_new, wu_new, wd_new)

    if with_hn:
        @pl.when(f == steps - 1)
        def _():
            hn_ref[...] = _rms(xo_ref[...], g2_ref[...]).astype(BF16)


def _ffn_first(x, g1, w_gu, w_down, g2, with_hn, name):
    assert x.shape[0] == ROW_TILE
    tile = CAST_TILE
    cols = (pl.Element(D_MODEL), pl.Element(tile))
    rows = (pl.Element(tile), pl.Element(D_MODEL))
    vec = pl.BlockSpec((1, D_MODEL), lambda f: (0, 0))
    row = pl.BlockSpec((ROW_TILE, D_MODEL), lambda f: (0, 0))
    gate_win = pl.BlockSpec(cols, lambda f: (0, _ff_start(f, tile)))
    up_win = pl.BlockSpec(cols, lambda f: (0, pl.multiple_of(D_FF + _ff_start(f, tile), LANES)))
    down_win = pl.BlockSpec(rows, lambda f: (_ff_start(f, tile), 0))
    out_shape = [jax.ShapeDtypeStruct((ROW_TILE, D_MODEL), F32)]
    if with_hn:
        out_shape.append(jax.ShapeDtypeStruct((ROW_TILE, D_MODEL), BF16))
    steps = _ff_steps(tile)
    weights_shape = ([jax.ShapeDtypeStruct((D_MODEL, steps * tile), BF16)] * 2
                     + [jax.ShapeDtypeStruct((steps * tile, D_MODEL), BF16)])
    col_blk = pl.BlockSpec((D_MODEL, tile), lambda f: (0, f))
    row_blk = pl.BlockSpec((tile, D_MODEL), lambda f: (f, 0))
    return pl.pallas_call(
        functools.partial(_ffn_first_kernel, with_hn),
        grid=(steps,),
        in_specs=[row, vec, gate_win, up_win, down_win, vec],
        out_specs=[row] * len(out_shape) + [col_blk, col_blk, row_blk],
        out_shape=out_shape + weights_shape,
        scratch_shapes=[pltpu.VMEM((ROW_TILE, D_MODEL), BF16)],
        compiler_params=_params("arbitrary"),
        name=name,
    )(x, g1, w_gu, w_gu, w_down, g2)


def _ffn_kernel(n_prompt_tiles, with_hn, tail_done, n_side, steps, xp_ref, xs_ref, *rest):
    if tail_done:
        hns_ref, rest = rest[0], rest[1:]
    g1_ref, wg_ref, wu_ref, wd_ref, wg_tail_ref, wu_tail_ref, wd_tail_ref, g2_ref = rest[:8]
    side_in, outs = rest[8:8 + n_side], rest[8 + n_side:]
    xo_ref = outs[0]
    hn_ref = outs[1] if with_hn else None
    side_out = outs[1 + with_hn:1 + with_hn + n_side]
    xn_ref = outs[-1]
    i, f = pl.program_id(0), pl.program_id(1)
    _side_cast(side_in, side_out)

    def on(cond):
        return jnp.logical_and(cond, i < n_prompt_tiles) if tail_done else cond

    @pl.when(on(f == 0))
    def _():
        x = xp_ref[...] if tail_done else jnp.where(i < n_prompt_tiles, xp_ref[...], xs_ref[...])
        xn_ref[...] = _rms(x, g1_ref[...]).astype(BF16)
        xo_ref[...] = x

    if tail_done:
        @pl.when(jnp.logical_and(f == 0, i >= n_prompt_tiles))
        def _():
            xo_ref[...] = xs_ref[...]
            hn_ref[...] = hns_ref[...]

    pl.when(on(f < steps - 1))(
        lambda: _swiglu_accumulate(xn_ref, xo_ref, wg_ref[...], wu_ref[...], wd_ref[...]))

    @pl.when(on(f == steps - 1))
    def _():
        _swiglu_accumulate(xn_ref, xo_ref, wg_ref[...], wu_ref[...], wd_ref[...])
        _swiglu_accumulate(xn_ref, xo_ref, wg_tail_ref[...], wu_tail_ref[...], wd_tail_ref[...])
        if with_hn:
            hn_ref[...] = _rms(xo_ref[...], g2_ref[...]).astype(BF16)


def _ffn(xp, xs, hn_s, g1, wg, wu, wd, g2, with_hn, name, tile, side=()):
    n_p, n_s = xp.shape[0] // ROW_TILE, xs.shape[0] // ROW_TILE
    assert n_s == 1 and xs.shape[0] == ROW_TILE
    m = xp.shape[0] + xs.shape[0]
    tail_done = hn_s is not None
    steps = D_FF // tile
    tail0 = steps * tile
    n_tail = D_FF - tail0
    assert with_hn or not tail_done

    def start(i, f):
        return pl.multiple_of((jnp.where(i < n_p, f, steps - 1) if tail_done else f) * tile, LANES)

    cols = (pl.Element(D_MODEL), pl.Element(tile))
    rows = (pl.Element(tile), pl.Element(D_MODEL))
    once = pl.Buffered(1)
    tail_w = [pl.BlockSpec((pl.Element(D_MODEL), pl.Element(n_tail)), lambda i, f: (0, wg[1] + tail0),
                           pipeline_mode=once),
              pl.BlockSpec((pl.Element(D_MODEL), pl.Element(n_tail)), lambda i, f: (0, wu[1] + tail0),
                           pipeline_mode=once),
              pl.BlockSpec((pl.Element(n_tail), pl.Element(D_MODEL)), lambda i, f: (tail0, 0),
                           pipeline_mode=once)]
    vec = pl.BlockSpec((1, D_MODEL), lambda i, f: (0, 0))
    row = pl.BlockSpec((ROW_TILE, D_MODEL), lambda i, f: (i, 0))
    if tail_done:
        tail_specs, tail_args = [_resident((ROW_TILE, D_MODEL))] * 2, [xs, hn_s]
    else:
        tail_specs, tail_args = [pl.BlockSpec((ROW_TILE, D_MODEL), lambda i, f: (0, 0))], [xs]
    out_shape = [jax.ShapeDtypeStruct((m, D_MODEL), F32)]
    if with_hn:
        out_shape.append(jax.ShapeDtypeStruct((m, D_MODEL), BF16))
    side_specs, side_shape = _side_cast_specs(side, (n_p + n_s) * steps, lambda i, f: i * steps + f)
    return pl.pallas_call(
        functools.partial(_ffn_kernel, n_p, with_hn, tail_done, len(side), steps),
        grid=(n_p + n_s, steps),
        in_specs=[pl.BlockSpec((ROW_TILE, D_MODEL), lambda i, f: (jnp.minimum(i, n_p - 1), 0)),
                  *tail_specs, vec,
                  pl.BlockSpec(cols, lambda i, f: (0, pl.multiple_of(wg[1] + start(i, f), LANES))),
                  pl.BlockSpec(cols, lambda i, f: (0, pl.multiple_of(wu[1] + start(i, f), LANES))),
                  pl.BlockSpec(rows, lambda i, f: (start(i, f), 0)),
                  *tail_w, vec, *side_specs],
        out_specs=[row] * len(out_shape) + side_specs,
        out_shape=out_shape + side_shape,
        scratch_shapes=[pltpu.VMEM((ROW_TILE, D_MODEL), BF16)],
        compiler_params=_params("arbitrary", "arbitrary"),
        name=name,
    )(xp, *tail_args, g1, wg[0], wu[0], wd, wg[0], wu[0], wd, g2, *[a for a, _ in side])


def _gates_kernel(n_side, h_ref, w_ref, *rest):
    side_in, o_ref, side_out = rest[:n_side], rest[n_side], rest[n_side + 1:]
    _side_cast(side_in, side_out)
    o_ref[...] = jax.nn.sigmoid(_dot(h_ref[...], w_ref[...])).astype(BF16)


def _gates(hn, w_in, side=()):
    m, tn = hn.shape[0], IN_COL_TILE
    col0 = 2 * D_CONV + Q_W + 2 * KV_W
    nj = 2 * D_MODEL // tn
    side_specs, side_shape = _side_cast_specs(side, m // IN_TILE * nj, lambda i, j: i * nj + j)
    return pl.pallas_call(
        functools.partial(_gates_kernel, len(side)),
        grid=(m // IN_TILE, nj),
        in_specs=[pl.BlockSpec((IN_TILE, D_MODEL), lambda i, j: (i, 0)),
                  pl.BlockSpec((pl.Element(D_MODEL), pl.Element(tn)),
                               lambda i, j: (0, pl.multiple_of(col0 + j * tn, LANES))),
                  *side_specs],
        out_specs=[pl.BlockSpec((IN_TILE, tn), lambda i, j: (i, j))] + side_specs,
        out_shape=[jax.ShapeDtypeStruct((m, 2 * D_MODEL), BF16)] + side_shape,
        compiler_params=_params("arbitrary", "arbitrary"),
        name="gates",
    )(hn, w_in, *[a for a, _ in side])


def _rotate(x, cos, sin_lo, sin_hi):
    cols = []
    for c in range(x.shape[1] // LANES):
        xb = x[:, c * LANES:(c + 1) * LANES]
        cols.append(xb * cos
                    + pltpu.roll(xb, LANES - ROT_DIM // 2, 1) * sin_lo
                    + pltpu.roll(xb, ROT_DIM // 2, 1) * sin_hi)
    return jnp.concatenate(cols, axis=1)


def _uqkv_kernel(h_ref, w_ref, cos_ref, slo_ref, shi_ref, u_ref, q_ref, k_ref, v_ref):
    h = h_ref[...]
    cos, slo, shi = cos_ref[...], slo_ref[...], shi_ref[...]
    c0 = 2 * D_CONV
    u_ref[...] = _dot(h, w_ref[:, :D_CONV]) * jax.nn.sigmoid(_dot(h, w_ref[:, D_CONV:c0]))
    q = _rotate(_dot(h, w_ref[:, c0:c0 + Q_W]), cos, slo, shi)
    q_ref[...] = (q * (LOG2E * HEAD_DIM ** -0.5)).astype(BF16)
    k_ref[...] = _rotate(_dot(h, w_ref[:, c0 + Q_W:c0 + Q_W + KV_W]), cos, slo, shi)
    v_ref[...] = _dot(h, w_ref[:, c0 + Q_W + KV_W:])


def _uqkv(hn, w_in, cos, slo, shi):
    m, tm = hn.shape[0], UQKV_TILE
    n_w = 2 * D_CONV + Q_W + 2 * KV_W
    tab = pl.BlockSpec((tm, LANES), lambda i: (i, 0))
    return pl.pallas_call(
        _uqkv_kernel,
        grid=(m // tm,),
        in_specs=[pl.BlockSpec((tm, D_MODEL), lambda i: (i, 0)),
                  pl.BlockSpec((pl.Element(D_MODEL), pl.Element(n_w)), lambda i: (0, 0),
                               pipeline_mode=pl.Buffered(1)),
                  tab, tab, tab],
        out_specs=[pl.BlockSpec((tm, D_CONV), lambda i: (i, 0)),
                   pl.BlockSpec((tm, Q_W), lambda i: (i, 0)),
                   pl.BlockSpec((tm, KV_W), lambda i: (i, 0)),
                   pl.BlockSpec((tm, KV_W), lambda i: (i, 0))],
        out_shape=[jax.ShapeDtypeStruct((m, D_CONV), F32),
                   jax.ShapeDtypeStruct((m, Q_W), BF16),
                   jax.ShapeDtypeStruct((m, KV_W), F32),
                   jax.ShapeDtypeStruct((m, KV_W), F32)],
        compiler_params=_params("parallel"),
        name="uqkv",
    )(hn, w_in, cos, slo, shi)


def _rope_tables(batch, seq, dec_batch, dec_seq):
    inv = ROPE_THETA ** (-jnp.arange(0, ROT_DIM, 2, dtype=F32) / ROT_DIM)
    pos = jnp.concatenate([jnp.tile(jnp.arange(seq, dtype=jnp.int32), batch),
                           jnp.tile(PAST_LEN + jnp.arange(dec_seq, dtype=jnp.int32), dec_batch)])
    ang = pos.astype(F32)[:, None] * inv[None, :]
    cos, sin = jnp.cos(ang), jnp.sin(ang)
    one = jnp.ones((pos.shape[0], HEAD_DIM - ROT_DIM), F32)
    zero = jnp.zeros_like(one)
    zh = jnp.zeros_like(sin)
    c = jnp.concatenate([cos, cos, one], 1)
    lo = jnp.concatenate([-sin, zh, zero], 1)
    hi = jnp.concatenate([zh, sin, zero], 1)
    return [jnp.tile(t, (1, LANES // HEAD_DIM)) for t in (c, lo, hi)]


def _conv_stage(s_ref, history, u_ref):
    rows0 = CONV_HALO - history.shape[0]
    for cb in range(D_CONV // LANES):
        lanes = slice(cb * LANES, (cb + 1) * LANES)
        s_ref[cb, rows0:CONV_HALO, :] = history[:, lanes]
        s_ref[cb, CONV_HALO:, :] = u_ref[:, lanes]


def _conv_tile(s_ref, row0, c_ref, w_ref, b_ref, lg_ref, lb_ref, rows):
    base = CONV_HALO - (CONV_WIDTH - 1)
    n = rows // SUBLANES
    for cb in range(D_CONV // LANES):
        lanes = slice(cb * LANES, (cb + 1) * LANES)
        accs = [jnp.broadcast_to(b_ref[:, lanes], (n, LANES)) for _ in range(SUBLANES)]
        for j in range(CONV_WIDTH):
            w = jnp.broadcast_to(w_ref[j:j + 1, lanes], (n, LANES))
            for p in range(SUBLANES):
                accs[p] = accs[p] + s_ref[cb, pl.ds(row0 + base + j + p, n, stride=SUBLANES), :] * w
        for p in range(SUBLANES):
            c_ref[cb, pl.ds(p, n, stride=SUBLANES), :] = accs[p]
    acc = jnp.concatenate([c_ref[cb] for cb in range(D_CONV // LANES)], axis=1)
    mu = jnp.mean(acc, axis=-1, keepdims=True)
    xc = acc - mu
    y = xc * lax.rsqrt(jnp.mean(xc * xc, axis=-1, keepdims=True) + EPS) * lg_ref[...] + lb_ref[...]
    return (y * jax.nn.sigmoid(y)).astype(BF16)


def _conv_common_specs():
    return [pl.BlockSpec((CONV_WIDTH, D_CONV), lambda i: (0, 0)),
            pl.BlockSpec((1, D_CONV), lambda i: (0, 0)),
            pl.BlockSpec((1, D_CONV), lambda i: (0, 0)),
            pl.BlockSpec((1, D_CONV), lambda i: (0, 0))]


def _conv_scratch(block_rows, tile_rows):
    return [pltpu.VMEM((D_CONV // LANES, CONV_HALO + block_rows, LANES), F32),
            pltpu.VMEM((D_CONV // LANES, tile_rows, LANES), F32)]


def _attn_tile(q, k, v, sink_ref, valid):
    r = q.shape[0]
    lane_head = lax.broadcasted_iota(jnp.int32, (1, KV_W), 1) // HEAD_DIM
    sel_row = lax.broadcasted_iota(jnp.int32, (KV_W, Q_W), 0)
    sel_col = lax.broadcasted_iota(jnp.int32, (KV_W, Q_W), 1)
    sel = jnp.where(sel_row == sel_col // KV_W * HEAD_DIM + sel_col % HEAD_DIM, 1.0, 0.0).astype(BF16)
    krep = _dot(k, sel).astype(BF16)
    vrep = _dot(v, sel).astype(BF16)
    head_mask = [jnp.where(lane_head == g, 1.0, 0.0).astype(BF16) for g in range(GROUP)]
    scores = []
    for kv in range(N_KV):
        cols = slice(kv * KV_W, (kv + 1) * KV_W)
        lhs = jnp.concatenate([q[:, cols] * head_mask[g] for g in range(GROUP)], axis=0)
        scores.append(lax.dot_general(lhs, krep[:, cols], (((1,), (1,)), ((), ())),
                                      preferred_element_type=F32))
    weights, dens = [], []
    for kv in range(N_KV):
        es = []
        for g in range(GROUP):
            sg = scores[kv][g * r:(g + 1) * r]
            if valid is not None:
                sg = jnp.where(valid, sg, NEG)
            sink = sink_ref[kv * GROUP + g] * LOG2E
            mx = jnp.maximum(jnp.max(sg, axis=-1, keepdims=True), sink)
            e = jnp.exp2(sg - mx)
            dens.append(jnp.sum(e, axis=-1, keepdims=True) + jnp.exp2(sink - mx))
            es.append(e.astype(BF16))
        weights.append(jnp.concatenate(es, axis=0))
    outs = []
    for kv in range(N_KV):
        o4 = _dot(weights[kv], vrep[:, kv * KV_W:(kv + 1) * KV_W])
        og = [o4[g * r:(g + 1) * r] / dens[kv * GROUP + g] for g in range(GROUP)]
        oh = og[GROUP - 1]
        for g in range(GROUP - 2, -1, -1):
            oh = jnp.where(lane_head == g, og[g], oh)
        outs.append(oh)
    return jnp.concatenate(outs, axis=1).astype(BF16)


def _merge_out(yc, o, gc, ga, x, wc_ref, wa_ref, wo_ref):
    m = gc.astype(F32) * _dot(yc, wc_ref[...]) + ga.astype(F32) * _dot(o, wa_ref[...])
    return x + _dot(m.astype(BF16), wo_ref[...])


def _mix_kernel(per_seq, sink_ref, u_ref, uh_ref, q_ref, kp_ref, kc_ref, vp_ref, vc_ref,
                cw_ref, cb_ref, lg_ref, lb_ref, gc_ref, ga_ref, x_ref, wc_ref, wa_ref, wo_ref,
                out_ref, s_ref, c_ref):
    first = pl.program_id(0) % per_seq == 0
    _conv_stage(s_ref, jnp.where(first, 0.0, uh_ref[...]), u_ref)

    k = jnp.concatenate([kp_ref[...], kc_ref[...]], axis=0).astype(BF16)
    v = jnp.concatenate([vp_ref[...], vc_ref[...]], axis=0).astype(BF16)
    nk = WINDOW + MIX_TILE
    col = lax.broadcasted_iota(jnp.int32, (MIX_TILE, nk), 1)
    d = col // CHUNK - lax.broadcasted_iota(jnp.int32, (MIX_TILE, nk), 0) // CHUNK
    band = (d >= 0) & (d <= WINDOW // CHUNK)

    for r0 in range(0, MIX_ROWS, MIX_TILE):
        rows = slice(r0, r0 + MIX_TILE)
        yc = _conv_tile(s_ref, r0, c_ref, cw_ref, cb_ref, lg_ref, lb_ref, MIX_TILE)
        valid = band & ((col >= WINDOW) | jnp.logical_not(first)) if r0 == 0 else band
        o = _attn_tile(q_ref[rows, :], k[r0:r0 + nk], v[r0:r0 + nk], sink_ref, valid)
        out_ref[rows, :] = _merge_out(yc, o, gc_ref[rows, :], ga_ref[rows, :], x_ref[rows, :],
                                      wc_ref, wa_ref, wo_ref)


def _mix_prompt(u, q, k, v, gates, x, sinks, cw, cb, lg, lb, wc, wa, wo, n_rows, seq):
    rows = MIX_ROWS
    halo_per_block = rows // CONV_HALO
    pre_per_block = rows // WINDOW
    kv_prev = pl.BlockSpec((WINDOW, KV_W), lambda i: (jnp.maximum(i * pre_per_block - 1, 0), 0))
    kv_cur = pl.BlockSpec((rows, KV_W), lambda i: (i, 0))
    return pl.pallas_call(
        functools.partial(_mix_kernel, seq // rows),
        grid=(n_rows // rows,),
        in_specs=[pl.BlockSpec(memory_space=pltpu.SMEM),
                  pl.BlockSpec((rows, D_CONV), lambda i: (i, 0)),
                  pl.BlockSpec((CONV_HALO, D_CONV), lambda i: (jnp.maximum(i * halo_per_block - 1, 0), 0)),
                  pl.BlockSpec((rows, Q_W), lambda i: (i, 0)),
                  kv_prev, kv_cur, kv_prev, kv_cur,
                  *_conv_common_specs(),
                  pl.BlockSpec((rows, D_MODEL), lambda i: (i, 0)),
                  pl.BlockSpec((rows, D_MODEL), lambda i: (i, 1)),
                  pl.BlockSpec((rows, D_MODEL), lambda i: (i, 0)),
                  _resident((D_CONV, D_MODEL)), _resident((Q_W, D_MODEL)), _resident((D_MODEL, D_MODEL))],
        out_specs=pl.BlockSpec((rows, D_MODEL), lambda i: (i, 0)),
        out_shape=jax.ShapeDtypeStruct((n_rows, D_MODEL), F32),
        scratch_shapes=_conv_scratch(rows, MIX_TILE),
        compiler_params=_params("parallel"),
        name="mix_prompt",
    )(sinks, u, u, q, k, k, v, v, cw, cb, lg, lb, gates, gates, x, wc, wa, wo)


def _mix_sample_kernel(sink_ref, u_ref, past_ref, q_ref, kp_ref, kc_ref, vp_ref, vc_ref,
                       cw_ref, cb_ref, lg_ref, lb_ref, gc_ref, ga_ref, x_ref, wc_ref, wa_ref, wo_ref,
                       out_ref, s_ref, c_ref):
    ycs, os_ = [], []
    for s in range(u_ref.shape[0] // CHUNK):
        rows = slice(s * CHUNK, (s + 1) * CHUNK)
        _conv_stage(s_ref, past_ref[s], u_ref[rows, :])
        ycs.append(_conv_tile(s_ref, 0, c_ref, cw_ref, cb_ref, lg_ref, lb_ref, CHUNK))
        k = jnp.concatenate([kp_ref[s], kc_ref[rows, :]], axis=0).astype(BF16)
        v = jnp.concatenate([vp_ref[s], vc_ref[rows, :]], axis=0).astype(BF16)
        os_.append(_attn_tile(q_ref[rows, :], k, v, sink_ref, None))
    out_ref[...] = _merge_out(jnp.concatenate(ycs, axis=0), jnp.concatenate(os_, axis=0),
                              gc_ref[...], ga_ref[...], x_ref[...], wc_ref, wa_ref, wo_ref)


def _mix_sample(u, q, k, v, gates, x, sinks, cw, cb, lg, lb, wc, wa, wo, state, cache_k, cache_v, row0):
    n_seq, hist, _ = state.shape
    assert hist == CONV_WIDTH - 1 and cache_k.shape[1] == WINDOW
    rows = SAMPLE_ROWS
    seqs = rows // CHUNK
    blk0 = row0 // rows

    def cur(width, col=0):
        return pl.BlockSpec((rows, width), lambda i: (blk0 + i, col))

    cache = pl.BlockSpec((seqs, WINDOW, KV_W), lambda i: (i, 0, 0))
    return pl.pallas_call(
        _mix_sample_kernel,
        grid=(n_seq // seqs,),
        in_specs=[pl.BlockSpec(memory_space=pltpu.SMEM),
                  cur(D_CONV), pl.BlockSpec((seqs, hist, D_CONV), lambda i: (i, 0, 0)), cur(Q_W),
                  cache, cur(KV_W), cache, cur(KV_W),
                  *_conv_common_specs(),
                  cur(D_MODEL), cur(D_MODEL, 1), cur(D_MODEL),
                  _resident((D_CONV, D_MODEL)), _resident((Q_W, D_MODEL)), _resident((D_MODEL, D_MODEL))],
        out_specs=pl.BlockSpec((rows, D_MODEL), lambda i: (i, 0)),
        out_shape=jax.ShapeDtypeStruct((n_seq * CHUNK, D_MODEL), F32),
        scratch_shapes=_conv_scratch(CHUNK, CHUNK),
        compiler_params=_params("parallel"),
        name="mix_sample",
    )(sinks, u, state, q, cache_k, k, cache_v, v, cw, cb, lg, lb, gates, gates, x, wc, wa, wo)


def _ple_kernel(n_prompt_tiles, x_ref, pp_ref, ps_ref, gn_ref, wg_ref, wp_ref, fn_ref, yp_ref, ys_ref):
    i = pl.program_id(0)
    x = x_ref[...]
    gate = jax.nn.sigmoid(_dot(_rms(x, gn_ref[...]).astype(BF16), wg_ref[...]))
    pe = jnp.where(i < n_prompt_tiles, pp_ref[...], ps_ref[...]).astype(BF16)
    y = _rms(x + gate * _dot(pe, wp_ref[...]), fn_ref[...])

    @pl.when(i < n_prompt_tiles)
    def _():
        yp_ref[...] = y

    @pl.when(i >= n_prompt_tiles)
    def _():
        ys_ref[...] = y


def _ple(x, pp, ps, gn, wg, wp, fn):
    tm = PLE_TILE
    n_p, n_s = pp.shape[0] // tm, ps.shape[0] // tm
    vec = pl.BlockSpec((1, D_MODEL), lambda i: (0, 0))

    def p_idx(i):
        return (jnp.minimum(i, n_p - 1), 0)

    def s_idx(i):
        return (jnp.maximum(i - n_p, 0), 0)

    return pl.pallas_call(
        functools.partial(_ple_kernel, n_p),
        grid=(n_p + n_s,),
        in_specs=[pl.BlockSpec((tm, D_MODEL), lambda i: (i, 0)),
                  pl.BlockSpec((tm, D_PLE), p_idx),
                  pl.BlockSpec((tm, D_PLE), s_idx),
                  vec, _resident((D_MODEL, D_MODEL)), _resident((D_PLE, D_MODEL)), vec],
        out_specs=[pl.BlockSpec((tm, D_MODEL), p_idx), pl.BlockSpec((tm, D_MODEL), s_idx)],
        out_shape=[jax.ShapeDtypeStruct((pp.shape[0], D_MODEL), F32),
                   jax.ShapeDtypeStruct((ps.shape[0], D_MODEL), F32)],
        compiler_params=_params("arbitrary"),
        name="ple_final",
    )(x, pp, ps, gn, wg, wp, fn)


def kernel(x_prompt, x_sample, p_prompt, p_sample, state_conv, cache_k, cache_v, ffn1_norm, ffn1_w_gu, ffn1_w_down, mix_norm, w_in, conv_w, conv_b, conv_ln_g, conv_ln_b, conv_w_out, attn_sinks, attn_w_out, w_out, ffn2_norm, ffn2_w_gu, ffn2_w_down, ple_norm, ple_w_gate, ple_w_proj, final_norm):
    assert x_prompt.shape[-1] == D_MODEL and ffn1_norm.shape[0] == 1 and w_in.shape[-1] == IN_COLS
    batch, seq, _ = x_prompt.shape
    dec_batch, dec_seq, _ = x_sample.shape
    assert dec_seq == CHUNK and seq % ROW_TILE == 0
    n_p, n_s = batch * seq, dec_batch * dec_seq

    def vec(a):
        return a.reshape(1, -1)

    xp = x_prompt.reshape(n_p, D_MODEL)
    xs = x_sample.reshape(n_s, D_MODEL)

    x1_s, hn_s, wg1, wu1, wd1 = _ffn_first(xs, vec(ffn1_norm), ffn1_w_gu[0], ffn1_w_down[0], vec(mix_norm),
                                           True, "ffn1_first")
    x1, hn, wgu2, wd2, w_in_b = _ffn(
        xp, x1_s, hn_s, vec(ffn1_norm), (wg1, 0), (wu1, 0), wd1, vec(mix_norm), True, "ffn1", FF1_TILE,
        side=[(ffn2_w_gu[0], WGU_CAST_ROWS), (ffn2_w_down[0], WD_CAST_ROWS), (w_in[0], WGU_CAST_ROWS)])

    cos, slo, shi = _rope_tables(batch, seq, dec_batch, dec_seq)
    u, q, k, v = _uqkv(hn, w_in_b, cos, slo, shi)
    gate_steps = (n_p + n_s) // IN_TILE * (2 * D_MODEL // IN_COL_TILE)
    gates, w_co, w_ao, w_o, w_pg, w_pp = _gates(
        hn, w_in_b, side=[(w, max(w.shape[0] // gate_steps, BF16_ROWS))
                           for w in (conv_w_out[0], attn_w_out[0], w_out[0], ple_w_gate[0], ple_w_proj[0])])

    cw, cb, lg, lb = conv_w[0], vec(conv_b), vec(conv_ln_g), vec(conv_ln_b)
    sinks = attn_sinks[0]
    ck = cache_k[0].reshape(dec_batch, -1, KV_W)
    cv = cache_v[0].reshape(dec_batch, -1, KV_W)
    mix_args = (u, q, k, v, gates, x1, sinks, cw, cb, lg, lb, w_co, w_ao, w_o)
    x2_p = _mix_prompt(*mix_args, n_p, seq)
    x2_s = _mix_sample(*mix_args, state_conv[0], ck, cv, n_p)
    x3, = _ffn(x2_p, x2_s, None, vec(ffn2_norm), (wgu2, 0), (wgu2, D_FF), wd2, vec(ffn2_norm), False, "ffn2",
               FF2_TILE)

    yp, ys = _ple(x3, p_prompt[0].reshape(n_p, D_PLE), p_sample[0].reshape(n_s, D_PLE),
                  vec(ple_norm), w_pg, w_pp, vec(final_norm))

    w_rows = ck.shape[1]

    def prompt_tail(a, rows):
        return jnp.stack([a[(b + 1) * seq - rows:(b + 1) * seq] for b in range(batch)])

    k_p = prompt_tail(k, WINDOW).reshape(batch, WINDOW, N_KV, HEAD_DIM)
    v_p = prompt_tail(v, WINDOW).reshape(batch, WINDOW, N_KV, HEAD_DIM)
    c_p = prompt_tail(u, CONV_WIDTH - 1)
    k_s = jnp.concatenate([cache_k[0], k[n_p:].reshape(dec_batch, dec_seq, N_KV, HEAD_DIM)], 1)[:, -w_rows:]
    v_s = jnp.concatenate([cache_v[0], v[n_p:].reshape(dec_batch, dec_seq, N_KV, HEAD_DIM)], 1)[:, -w_rows:]
    c_s = jnp.concatenate([state_conv[0], u[n_p:].reshape(dec_batch, dec_seq, D_CONV)], 1)[:, -(CONV_WIDTH - 1):]
    return (yp.reshape(batch, seq, D_MODEL), ys.reshape(dec_batch, dec_seq, D_MODEL),
            k_p[None], v_p[None], c_p[None], k_s[None], v_s[None], c_s[None])
```
